```python
import functools
import jax, jax.numpy as jnp
from jax import lax
import numpy as np

D_MODEL = 1024
BATCH = 16
SEQ = 2048
DEPTH = 1
DEC_BATCH = 128
DEC_SEQ = 4
PAST_LEN = 8192
PAGE_SIZE = 128

RET_HEADS = 8
RET_QK_DIM = 32
RET_V_DIM = 64
RET_CHUNK = 128
RET_ROPE_BASE = 10000.0
ATT_HEADS = 8
ATT_HEAD_DIM = 64
ROT_DIMS = ATT_HEAD_DIM // 4
ROPE_THETA = 500000.0
DILATED_PATTERNS = ((128, 1), (512, 4), (2048, 16))
MAX_WINDOW = 2048
RET_QK = RET_HEADS * RET_QK_DIM
RET_V = RET_HEADS * RET_V_DIM
ATT_W = ATT_HEADS * ATT_HEAD_DIM
MIX_WIDTH = RET_V + ATT_W
IN_WIDTHS = (RET_QK, RET_QK, RET_V, RET_V, ATT_W, ATT_W, ATT_W)
IN_COLS = RET_QK * 2 + RET_V * 2 + ATT_W * 3
IN_SPLITS = (RET_QK, 2 * RET_QK, 2 * RET_QK + RET_V, 2 * RET_QK + 2 * RET_V, 2 * RET_QK + 2 * RET_V + ATT_W, 2 * RET_QK + 2 * RET_V + 2 * ATT_W)
N_EXPERTS = 32
TOP_K = 4
D_EXPERT = D_MODEL
SWIGLU_ALPHA = 1.702
SWIGLU_LIMIT = 7.0
MOE_BLOCK = 256
PLE_DIM = 256
LN_EPS = 1e-5
GN_EPS = 1e-6
DEEPNORM_ALPHA = (2 * DEPTH) ** 0.25
DEEPNORM_BETA = (8 * DEPTH) ** -0.25
NEG_INF = -1e30

kernel_name = "hymba_retention_dilated_window_moe_step"


def layer_norm(x, g, b, dtype):
    xf = x.astype(jnp.float32)
    mu = jnp.mean(xf, axis=-1, keepdims=True)
    var = jnp.mean(jnp.square(xf - mu), axis=-1, keepdims=True)
    return ((xf - mu) * lax.rsqrt(var + LN_EPS) * g + b).astype(dtype)


def rotary(x, pos, inv_freq):
    half = inv_freq.shape[0]
    ang = pos.astype(jnp.float32)[:, None] * inv_freq[None, :]
    cos = jnp.cos(ang)[None, :, None, :]
    sin = jnp.sin(ang)[None, :, None, :]
    xf = x.astype(jnp.float32)
    x1 = xf[..., :half]
    x2 = xf[..., half:2 * half]
    out = jnp.concatenate([x1 * cos - x2 * sin, x2 * cos + x1 * sin, xf[..., 2 * half:]], axis=-1)
    return out.astype(x.dtype)


def ret_inv_freq():
    return 1.0 / (RET_ROPE_BASE ** jnp.linspace(0.0, 1.0, RET_QK_DIM // 2, dtype=jnp.float32))


def att_inv_freq():
    return ROPE_THETA ** (-jnp.arange(0, ROT_DIMS, 2, dtype=jnp.float32) / ROT_DIMS)


def ret_log_decay():
    return jnp.log(1.0 - 2.0 ** (-5.0 - jnp.arange(RET_HEADS, dtype=jnp.float32)))


def mixer_projections(x, w_in, pos):
    B, S, _ = x.shape
    h = x @ w_in
    q_r, k_r, v_r, g_r, q_a, k_a, v_a = jnp.split(h, list(IN_SPLITS), axis=-1)
    q_r = rotary(q_r.reshape(B, S, RET_HEADS, RET_QK_DIM), pos, ret_inv_freq())
    k_r = rotary(k_r.reshape(B, S, RET_HEADS, RET_QK_DIM), pos, ret_inv_freq()) * (RET_QK_DIM ** -0.5)
    v_r = v_r.reshape(B, S, RET_HEADS, RET_V_DIM)
    q_a = rotary(q_a.reshape(B, S, ATT_HEADS, ATT_HEAD_DIM), pos, att_inv_freq())
    k_a = rotary(k_a.reshape(B, S, ATT_HEADS, ATT_HEAD_DIM), pos, att_inv_freq())
    v_a = v_a.reshape(B, S, ATT_HEADS, ATT_HEAD_DIM)
    return q_r, k_r, v_r, g_r, q_a, k_a, v_a


def retention_chunk(state, q, k, v):
    q = q.astype(jnp.float32)
    k = k.astype(jnp.float32)
    v = v.astype(jnp.float32)
    C = q.shape[1]
    lg = ret_log_decay()
    idx = jnp.arange(C, dtype=jnp.float32)
    diff = idx[:, None] - idx[None, :]
    decay = jnp.where(diff[None] >= 0, jnp.exp(jnp.maximum(diff, 0.0)[None] * lg[:, None, None]), 0.0)
    inner = jnp.einsum('bihd,bjhd->bhij', q, k) * decay[None]
    o = jnp.einsum('bhij,bjhe->bihe', inner, v)
    cross = jnp.exp((idx + 1.0)[:, None] * lg[None, :])
    o = o + jnp.einsum('bihd,bhde->bihe', q, state) * cross[None, :, :, None]
    k_dec = k * jnp.exp((C - 1.0 - idx)[:, None] * lg[None, :])[None, :, :, None]
    new_state = state * jnp.exp(C * lg)[None, :, None, None] + jnp.einsum('bjhd,bjhe->bhde', k_dec, v)
    return o, new_state


def retention_prompt(q, k, v):
    B, S, H, dk = q.shape
    dv = v.shape[-1]
    nc = S // RET_CHUNK

    def chunks(t):
        return t.reshape(B, nc, RET_CHUNK, H, t.shape[-1]).swapaxes(0, 1)

    def step(st, qkv):
        o, st = retention_chunk(st, *qkv)
        return st, o

    state0 = jnp.zeros((B, H, dk, dv), jnp.float32)
    st, o = lax.scan(step, state0, (chunks(q), chunks(k), chunks(v)))
    return o.swapaxes(0, 1).reshape(B, S, H, dv), st


def combine_by_denominator(lses, outs):
    w = jax.nn.softmax(jnp.stack(lses), axis=0)
    return jnp.sum(w[..., None] * jnp.stack(outs), axis=0)


def dilated_attention_prompt(q, k, v):
    B, S, H, D = q.shape
    q = q.astype(jnp.float32)
    k = k.astype(jnp.float32)
    v = v.astype(jnp.float32)
    scale = D ** -0.5
    lses, outs = [], []
    for window, d in DILATED_PATTERNS:
        n = window // d
        L = S // d
        nb = -(-L // n)
        pad = nb * n - L

        def to_blocks(t):
            t = t.reshape(B, L, d, H, D).transpose(0, 2, 1, 3, 4)
            t = jnp.pad(t, ((0, 0), (0, 0), (0, pad), (0, 0), (0, 0)))
            return t.reshape(B, d, nb, n, H, D)

        def with_prev(t):
            prev = jnp.pad(t[:, :, :-1], ((0, 0), (0, 0), (1, 0), (0, 0), (0, 0), (0, 0)))
            return jnp.concatenate([prev, t], axis=3)

        qb = to_blocks(q)
        kk = with_prev(to_blocks(k))
        vv = with_prev(to_blocks(v))
        a = jnp.arange(n)[:, None]
        c = jnp.arange(2 * n)[None, :]
        dist = n + a - c
        band = (dist >= 0) & (dist <= n)
        valid = jnp.where((jnp.arange(nb) == 0)[:, None, None], band & (c >= n), band)
        s = jnp.einsum('brcqhd,brckhd->brchqk', qb, kk) * scale
        s = jnp.where(valid[None, None, :, None], s, NEG_INF)
        lse = jax.nn.logsumexp(s, axis=-1)
        p = jnp.exp(s - lse[..., None])
        o = jnp.einsum('brchqk,brckhd->brcqhd', p, vv)
        o = o.reshape(B, d, nb * n, H, D)[:, :, :L].transpose(0, 2, 1, 3, 4).reshape(B, S, H, D)
        lse = lse.transpose(0, 1, 2, 4, 3).reshape(B, d, nb * n, H)[:, :, :L].transpose(0, 2, 1, 3).reshape(B, S, H)
        lses.append(lse)
        outs.append(o)
    return combine_by_denominator(lses, outs)


def dilated_attention_sample(q, k_new, v_new, buf_k, buf_v):
    BUF = buf_k.shape[1]
    T = q.shape[1]
    D = q.shape[-1]
    q = q.astype(jnp.float32)
    k_all = jnp.concatenate([buf_k.astype(jnp.float32), k_new.astype(jnp.float32)], axis=1)
    v_all = jnp.concatenate([buf_v.astype(jnp.float32), v_new.astype(jnp.float32)], axis=1)
    scale = D ** -0.5
    lses, outs = [], []
    for window, d in DILATED_PATTERNS:
        n = window // d
        idx = BUF + jnp.arange(T)[:, None] - d * jnp.arange(n + 1)[None, :]
        valid = idx >= 0
        idx = jnp.maximum(idx, 0)
        kg = jnp.take(k_all, idx, axis=1)
        vg = jnp.take(v_all, idx, axis=1)
        s = jnp.einsum('bthd,btjhd->bhtj', q, kg) * scale
        s = jnp.where(valid[None, None], s, NEG_INF)
        lse = jax.nn.logsumexp(s, axis=-1)
        p = jnp.exp(s - lse[..., None])
        o = jnp.einsum('bhtj,btjhd->bthd', p, vg)
        lses.append(lse.transpose(0, 2, 1))
        outs.append(o)
    return combine_by_denominator(lses, outs)


def prompt_token_mixers(q_r, k_r, v_r, q_a, k_a, v_a):
    o_ret, st = retention_prompt(q_r, k_r, v_r)
    o_att = dilated_attention_prompt(q_a, k_a, v_a)
    S = k_a.shape[1]
    keep = min(MAX_WINDOW, S)
    return o_ret, o_att, (k_a[:, S - keep:], v_a[:, S - keep:], st)


def sample_token_mixers(buf_k, buf_v, state, q_r, k_r, v_r, q_a, k_a, v_a):
    o_ret, st = retention_chunk(state.astype(jnp.float32), q_r, k_r, v_r)
    o_att = dilated_attention_sample(q_a, k_a, v_a, buf_k, buf_v)
    return o_ret, o_att, (k_a, v_a, st.astype(state.dtype))


def mixer_output(o_ret, g_r, o_att, gn_w, gn_b, w_out, dtype):
    B, S = o_ret.shape[:2]
    of = o_ret.astype(jnp.float32)
    mu = jnp.mean(of, axis=-1, keepdims=True)
    var = jnp.mean(jnp.square(of - mu), axis=-1, keepdims=True)
    o_n = ((of - mu) * lax.rsqrt(var + GN_EPS)).reshape(B, S, RET_V) * gn_w + gn_b
    ret = jax.nn.silu(g_r.astype(jnp.float32)) * o_n
    mixed = jnp.concatenate([ret, o_att.reshape(B, S, ATT_W).astype(jnp.float32)], axis=-1).astype(dtype)
    return mixed @ w_out


def clamped_swiglu(h):
    glu = jnp.minimum(h[..., ::2], SWIGLU_LIMIT)
    lin = jnp.clip(h[..., 1::2], -SWIGLU_LIMIT, SWIGLU_LIMIT)
    return glu * jax.nn.sigmoid(SWIGLU_ALPHA * glu) * (lin + 1.0)


def moe_ffn(x, w_router, b_router, w1, b1, w2, b2):
    shp = x.shape
    xt = x.reshape(-1, D_MODEL)
    T = xt.shape[0]
    A = T * TOP_K
    logits = xt.astype(jnp.float32) @ w_router.astype(jnp.float32) + b_router.astype(jnp.float32)
    top_v, top_e = lax.top_k(logits, TOP_K)
    gates = jax.nn.softmax(top_v, axis=-1)
    flat_e = top_e.reshape(-1).astype(jnp.int32)
    flat_t = (jnp.arange(A, dtype=jnp.int32) // TOP_K)
    flat_g = gates.reshape(-1)
    order = jnp.argsort(flat_e)
    se, st, sg = flat_e[order], flat_t[order], flat_g[order]
    counts = jnp.zeros((N_EXPERTS,), jnp.int32).at[flat_e].add(1)
    padded = (counts + MOE_BLOCK - 1) // MOE_BLOCK * MOE_BLOCK
    start = jnp.cumsum(counts) - counts
    pend = jnp.cumsum(padded)
    pstart = pend - padded
    dest = pstart[se] + jnp.arange(A, dtype=jnp.int32) - start[se]
    n_blocks = -(-(A + N_EXPERTS * (MOE_BLOCK - 1)) // MOE_BLOCK)
    P = n_blocks * MOE_BLOCK
    slot_tok = jnp.full((P,), T, jnp.int32).at[dest].set(st)
    slot_gate = jnp.zeros((P,), jnp.float32).at[dest].set(sg)
    block_e = jnp.minimum(jnp.searchsorted(pend, jnp.arange(n_blocks, dtype=jnp.int32) * MOE_BLOCK, side='right'), N_EXPERTS - 1)
    x_pad = jnp.concatenate([xt, jnp.zeros((1, D_MODEL), xt.dtype)], axis=0)

    def expert_block(args):
        tok, e = args
        h = x_pad[tok] @ w1[e] + b1[e]
        return clamped_swiglu(h) @ w2[e] + b2[e]

    y = lax.map(expert_block, (slot_tok.reshape(n_blocks, MOE_BLOCK), block_e))
    y = y.reshape(P, D_MODEL).astype(jnp.float32) * slot_gate[:, None]
    out = jnp.zeros((T + 1, D_MODEL), jnp.float32).at[slot_tok].add(y)[:T]
    return out.reshape(shp).astype(x.dtype)


def layer_forward(x, p, pos, token_mixers, lw):
    (w_in, gn_w, gn_b, w_out, ln1_g, ln1_b, w_router, b_router, w_e1, b_e1, w_e2, b_e2,
     w_pl_gate, b_pl_gate, w_pl_proj, ln2_g, ln2_b) = lw
    q_r, k_r, v_r, g_r, q_a, k_a, v_a = mixer_projections(x, w_in, pos)
    o_ret, o_att, new_state = token_mixers(q_r, k_r, v_r, q_a, k_a, v_a)
    mix = mixer_output(o_ret, g_r, o_att, gn_w, gn_b, w_out, x.dtype)
    x1 = layer_norm(DEEPNORM_ALPHA * x + mix, ln1_g, ln1_b, x.dtype)
    ple = jax.nn.sigmoid(x1 @ w_pl_gate + b_pl_gate) * (p @ w_pl_proj)
    ffn = moe_ffn(x1, w_router, b_router, w_e1, b_e1, w_e2, b_e2)
    x2 = layer_norm(DEEPNORM_ALPHA * x1 + ffn + ple, ln2_g, ln2_b, x.dtype)
    return x2, new_state


def _normal(key, shape, scale):
    return scale * jax.random.normal(key, shape, jnp.float32)


def setup_inputs(seed: int = 0) -> dict:
    key = jax.random.key(seed)
    ks = jax.random.split(key, 26)
    att_buf = min(MAX_WINDOW, PAST_LEN)
    return {
        "x_prompt": _normal(ks[0], (BATCH, SEQ, D_MODEL), 1.0),
        "x_sample": _normal(ks[1], (DEC_BATCH, DEC_SEQ, D_MODEL), 1.0),
        "cache_att_k": _normal(ks[2], (DEPTH, DEC_BATCH, att_buf, ATT_HEADS, ATT_HEAD_DIM), 1.0),
        "cache_att_v": _normal(ks[3], (DEPTH, DEC_BATCH, att_buf, ATT_HEADS, ATT_HEAD_DIM), 1.0),
        "state_ret": _normal(ks[4], (DEPTH, DEC_BATCH, RET_HEADS, RET_QK_DIM, RET_V_DIM), 0.5),
        "p_prompt": _normal(ks[5], (DEPTH, BATCH, SEQ, PLE_DIM), 1.0),
        "p_sample": _normal(ks[6], (DEPTH, DEC_BATCH, DEC_SEQ, PLE_DIM), 1.0),
        "w_in": _normal(ks[7], (DEPTH, D_MODEL, IN_COLS), D_MODEL ** -0.5),
        "ret_gn_w": 1.0 + _normal(ks[8], (DEPTH, RET_V), 0.02),
        "ret_gn_b": _normal(ks[9], (DEPTH, RET_V), 0.02),
        "w_out": _normal(ks[10], (DEPTH, MIX_WIDTH, D_MODEL), DEEPNORM_BETA * MIX_WIDTH ** -0.5),
        "ln1_g": 1.0 + _normal(ks[11], (DEPTH, D_MODEL), 0.02),
        "ln1_b": _normal(ks[12], (DEPTH, D_MODEL), 0.02),
        "w_router": _normal(ks[13], (DEPTH, D_MODEL, N_EXPERTS), D_MODEL ** -0.5),
        "b_router": _normal(ks[14], (DEPTH, N_EXPERTS), 0.01),
        "w_e1": _normal(ks[15], (DEPTH, N_EXPERTS, D_MODEL, 2 * D_EXPERT), D_MODEL ** -0.5),
        "b_e1": _normal(ks[16], (DEPTH, N_EXPERTS, 2 * D_EXPERT), 0.02),
        "w_e2": _normal(ks[17], (DEPTH, N_EXPERTS, D_EXPERT, D_MODEL), DEEPNORM_BETA * D_EXPERT ** -0.5),
        "b_e2": _normal(ks[18], (DEPTH, N_EXPERTS, D_MODEL), 0.02),
        "w_pl_gate": _normal(ks[19], (DEPTH, D_MODEL, D_MODEL), D_MODEL ** -0.5),
        "b_pl_gate": _normal(ks[20], (DEPTH, D_MODEL), 0.02),
        "w_pl_proj": _normal(ks[21], (DEPTH, PLE_DIM, D_MODEL), DEEPNORM_BETA * PLE_DIM ** -0.5),
        "ln2_g": 1.0 + _normal(ks[22], (DEPTH, D_MODEL), 0.02),
        "ln2_b": _normal(ks[23], (DEPTH, D_MODEL), 0.02),
    }


def reference(x_prompt, x_sample, cache_att_k, cache_att_v, state_ret, p_prompt, p_sample,
              w_in, ret_gn_w, ret_gn_b, w_out, ln1_g, ln1_b, w_router, b_router,
              w_e1, b_e1, w_e2, b_e2, w_pl_gate, b_pl_gate, w_pl_proj, ln2_g, ln2_b):
    pos_prompt = jnp.arange(x_prompt.shape[1], dtype=jnp.float32)
    pos_sample = PAST_LEN + jnp.arange(x_sample.shape[1], dtype=jnp.float32)
    hp, hs = x_prompt, x_sample
    kp_list, vp_list, sp_list, ks_list, vs_list, ss_list = [], [], [], [], [], []
    for l in range(DEPTH):
        lw = (w_in[l], ret_gn_w[l], ret_gn_b[l], w_out[l], ln1_g[l], ln1_b[l], w_router[l], b_router[l],
              w_e1[l], b_e1[l], w_e2[l], b_e2[l], w_pl_gate[l], b_pl_gate[l], w_pl_proj[l], ln2_g[l], ln2_b[l])
        hp, (kp, vp, sp) = layer_forward(hp, p_prompt[l], pos_prompt, prompt_token_mixers, lw)
        sample_mixers = functools.partial(sample_token_mixers, cache_att_k[l], cache_att_v[l], state_ret[l])
        hs, (ks_, vs_, ss_) = layer_forward(hs, p_sample[l], pos_sample, sample_mixers, lw)
        kp_list.append(kp)
        vp_list.append(vp)
        sp_list.append(sp)
        ks_list.append(ks_)
        vs_list.append(vs_)
        ss_list.append(ss_)
    new_att_k_prompt = jnp.stack(kp_list)
    new_att_v_prompt = jnp.stack(vp_list)
    new_state_ret_prompt = jnp.stack(sp_list)
    new_att_k_sample = jnp.stack(ks_list)
    new_att_v_sample = jnp.stack(vs_list)
    new_state_ret_sample = jnp.stack(ss_list)
    return (hp, hs, new_att_k_prompt, new_att_v_prompt, new_state_ret_prompt, new_att_k_sample, new_att_v_sample, new_state_ret_sample)
```

```python
import functools
import math

import numpy as np
import jax
import jax.numpy as jnp
from jax import lax
from jax.experimental import pallas as pl
from jax.experimental.pallas import tpu as pltpu

F32 = jnp.float32
BF16 = jnp.bfloat16

D_MODEL = 1024
DEPTH = 1
PAST_LEN = 8192
RET_HEADS = 8
RET_QK_DIM = 32
RET_V_DIM = 64
RET_CHUNK = 128
RET_ROPE_BASE = 10000.0
ATT_HEADS = 8
ATT_HEAD_DIM = 64
ROT_DIMS = ATT_HEAD_DIM // 4
ROPE_THETA = 500000.0
DILATED_PATTERNS = ((128, 1), (512, 4), (2048, 16))
MAX_WINDOW = 2048
RET_QK = RET_HEADS * RET_QK_DIM
RET_V = RET_HEADS * RET_V_DIM
ATT_W = ATT_HEADS * ATT_HEAD_DIM
IN_COLS = RET_QK * 2 + RET_V * 2 + ATT_W * 3
N_EXPERTS = 32
TOP_K = 4
D_EXPERT = D_MODEL
SWIGLU_ALPHA = 1.702
SWIGLU_LIMIT = 7.0
PLE_DIM = 256
LN_EPS = 1e-5
GN_EPS = 1e-6
DEEPNORM_ALPHA = (2 * DEPTH) ** 0.25
NEG_INF = -1e30

LANES = 128
SUBLANES = 8
ROW_TILES = D_MODEL // LANES
VMEM_LIMIT = 56 * 1024 * 1024

TOKEN_TILE = 512
MOE_BLOCK = 256
COMBINE_TILE = 256
ATT_BLOCK = 128


def _dot(a, b):
    return jnp.dot(a, b, preferred_element_type=F32)


def _dot_nt(a, b):
    return lax.dot_general(a, b, (((1,), (1,)), ((), ())), preferred_element_type=F32)


def _dot_tn(a, b):
    return lax.dot_general(a, b, (((0,), (0,)), ((), ())), preferred_element_type=F32)


def _cparams(sem):
    return pltpu.CompilerParams(dimension_semantics=sem, vmem_limit_bytes=VMEM_LIMIT)


def _rotary_tables(pos, inv_freq, head_dim, n_heads):
    half = inv_freq.shape[0]
    ang = pos.astype(F32)[:, None] * inv_freq[None, :]
    cos, sin = jnp.cos(ang), jnp.sin(ang)
    rest = head_dim - 2 * half
    n = pos.shape[0]
    cos_h = jnp.concatenate([cos, cos, jnp.ones((n, rest), F32)], axis=1)
    sin_h = jnp.concatenate([-sin, sin, jnp.zeros((n, rest), F32)], axis=1)
    return jnp.tile(cos_h, (1, n_heads)), jnp.tile(sin_h, (1, n_heads))


def _ret_inv_freq():
    return 1.0 / (RET_ROPE_BASE ** jnp.linspace(0.0, 1.0, RET_QK_DIM // 2, dtype=F32))


def _att_inv_freq():
    return ROPE_THETA ** (-jnp.arange(0, ROT_DIMS, 2, dtype=F32) / ROT_DIMS)


def _ret_log_decay():
    return jnp.log(1.0 - 2.0 ** (-5.0 - jnp.arange(RET_HEADS, dtype=F32)))


def _retention_tables(c_true, c_pad):
    lg = _ret_log_decay()
    idx = jnp.arange(c_pad, dtype=F32)
    live = idx < c_true
    diff = idx[:, None] - idx[None, :]
    decay = jnp.where(diff[None] >= 0, jnp.exp(jnp.maximum(diff, 0.0)[None] * lg[:, None, None]), 0.0)
    decay = jnp.where(live[None, :, None] & live[None, None, :], decay, 0.0)
    cross = jnp.exp((idx + 1.0)[:, None] * lg[None, :])
    cross = jnp.where(live[:, None], cross, 0.0)
    kdec = jnp.exp((c_true - 1.0 - idx)[:, None] * lg[None, :])
    kdec = jnp.where(live[:, None], kdec, 0.0)
    sdec = jnp.exp(c_true * lg)
    hh = RET_HEADS // 2
    decay = decay.reshape(2, hh, c_pad, c_pad)
    cross = jnp.repeat(cross, RET_V_DIM, axis=1).reshape(c_pad, 2, hh * RET_V_DIM).transpose(1, 0, 2)
    kdec = jnp.repeat(kdec, RET_QK_DIM, axis=1).reshape(c_pad, 2, hh * RET_QK_DIM).transpose(1, 0, 2)
    sdec = jnp.broadcast_to(jnp.repeat(sdec, RET_V_DIM)[None, :], (hh * RET_QK_DIM, RET_V)) \
        .reshape(hh * RET_QK_DIM, 2, hh * RET_V_DIM).transpose(1, 0, 2)
    return decay, cross, kdec, sdec


def _rotate(h, cos, sin, half, period):
    outs = []
    for j in range(h.shape[1] // LANES):
        blk = h[:, j * LANES:(j + 1) * LANES]
        lane = lax.broadcasted_iota(jnp.int32, blk.shape, 1)
        first = (lane % period) < half
        partner = jnp.where(first, pltpu.roll(blk, LANES - half, 1), pltpu.roll(blk, half, 1))
        outs.append(blk * cos[:, j * LANES:(j + 1) * LANES] + partner * sin[:, j * LANES:(j + 1) * LANES])
    return jnp.concatenate(outs, axis=1)


def _inproj_kernel(x_ref, w_ref, cr_ref, sr_ref, ca_ref, sa_ref,
                   qr_ref, kr_ref, vr_ref, gr_ref, qa_ref, ka_ref, va_ref):
    x = x_ref[0].astype(BF16)

    def proj(c0, width):
        return _dot(x, w_ref[:, c0:c0 + width])

    cr, sr, ca, sa = cr_ref[...], sr_ref[...], ca_ref[...], sa_ref[...]
    c = 0
    qr_ref[0] = _rotate(proj(c, RET_QK), cr, sr, RET_QK_DIM // 2, RET_QK_DIM)
    c += RET_QK
    kr_ref[0] = _rotate(proj(c, RET_QK), cr, sr, RET_QK_DIM // 2, RET_QK_DIM) * (RET_QK_DIM ** -0.5)
    c += RET_QK
    vr_ref[0] = proj(c, RET_V)
    c += RET_V
    gr_ref[0] = proj(c, RET_V)
    c += RET_V
    qa_ref[0] = _rotate(proj(c, ATT_W), ca, sa, ROT_DIMS // 2, ATT_HEAD_DIM) * (ATT_HEAD_DIM ** -0.5)
    c += ATT_W
    ka_ref[0] = _rotate(proj(c, ATT_W), ca, sa, ROT_DIMS // 2, ATT_HEAD_DIM)
    c += ATT_W
    va_ref[0] = proj(c, ATT_W)


def _in_projection(x, w_in_b, pos):
    B, S, _ = x.shape
    tm = min(TOKEN_TILE, S)
    cr, sr = _rotary_tables(pos, _ret_inv_freq(), RET_QK_DIM, RET_HEADS)
    ca, sa = _rotary_tables(pos, _att_inv_freq(), ATT_HEAD_DIM, ATT_HEADS)
    widths = (RET_QK, RET_QK, RET_V, RET_V, ATT_W, ATT_W, ATT_W)
    tab = lambda w: pl.BlockSpec((tm, w), lambda s, b: (s, 0))
    return pl.pallas_call(
        _inproj_kernel,
        grid=(S // tm, B),
        in_specs=[pl.BlockSpec((1, tm, D_MODEL), lambda s, b: (b, s, 0)),
                  pl.BlockSpec((D_MODEL, IN_COLS), lambda s, b: (0, 0)),
                  tab(RET_QK), tab(RET_QK), tab(ATT_W), tab(ATT_W)],
        out_specs=[pl.BlockSpec((1, tm, w), lambda s, b: (b, s, 0)) for w in widths],
        out_shape=[jax.ShapeDtypeStruct((B, S, w), F32) for w in widths],
        compiler_params=_cparams(("arbitrary", "arbitrary")),
        name="in_projection",
    )(x, w_in_b, cr, sr, ca, sa)


def _retention_kernel(q_ref, k_ref, v_ref, s0_ref, dec_ref, cross_ref, kdec_ref, sdec_ref,
                      o_ref, so_ref, *, chunk, n_chunks):
    hh = RET_HEADS // 2
    qk_w, v_w = hh * RET_QK_DIM, hh * RET_V_DIM
    lane_q = lax.broadcasted_iota(jnp.int32, (chunk, qk_w), 1) // RET_QK_DIM
    lane_v = lax.broadcasted_iota(jnp.int32, (chunk, v_w), 1) // RET_V_DIM
    blockdiag = (lax.broadcasted_iota(jnp.int32, (qk_w, v_w), 0) // RET_QK_DIM
                 == lax.broadcasted_iota(jnp.int32, (qk_w, v_w), 1) // RET_V_DIM)
    state = s0_ref[0, 0]
    cross, kdec, sdec = cross_ref[0], kdec_ref[0], sdec_ref[0]
    for ci in range(n_chunks):
        rows = pl.ds(ci * chunk, chunk)
        q, k, v = q_ref[0, rows, :], k_ref[0, rows, :], v_ref[0, rows, :]
        kb, vb = k.astype(BF16), v.astype(BF16)
        o = _dot(q.astype(BF16), state.astype(BF16)) * cross
        for h in range(hh):
            qm = jnp.where(lane_q == h, q, 0.0).astype(BF16)
            inner = _dot_nt(qm, kb) * dec_ref[0, h]
            o = jnp.where(lane_v == h, o + _dot(inner.astype(BF16), vb), o)
        o_ref[0, rows, :] = o
        upd = _dot_tn((k * kdec).astype(BF16), vb)
        state = state * sdec + jnp.where(blockdiag, upd, 0.0)
    so_ref[0, 0] = state


def _retention(q, k, v, state_bd, c_true, c_pad):
    B, S, _ = q.shape
    hh = RET_HEADS // 2
    qk_w, v_w = hh * RET_QK_DIM, hh * RET_V_DIM
    decay, cross, kdec, sdec = _retention_tables(c_true, c_pad)
    kern = functools.partial(_retention_kernel, chunk=c_pad, n_chunks=S // c_pad)
    return pl.pallas_call(
        kern,
        grid=(B, 2),
        in_specs=[pl.BlockSpec((1, S, qk_w), lambda b, g: (b, 0, g)),
                  pl.BlockSpec((1, S, qk_w), lambda b, g: (b, 0, g)),
                  pl.BlockSpec((1, S, v_w), lambda b, g: (b, 0, g)),
                  pl.BlockSpec((1, 1, qk_w, v_w), lambda b, g: (b, g, 0, 0)),
                  pl.BlockSpec((1, hh, c_pad, c_pad), lambda b, g: (g, 0, 0, 0)),
                  pl.BlockSpec((1, c_pad, v_w), lambda b, g: (g, 0, 0)),
                  pl.BlockSpec((1, c_pad, qk_w), lambda b, g: (g, 0, 0)),
                  pl.BlockSpec((1, qk_w, v_w), lambda b, g: (g, 0, 0))],
        out_specs=[pl.BlockSpec((1, S, v_w), lambda b, g: (b, 0, g)),
                   pl.BlockSpec((1, 1, qk_w, v_w), lambda b, g: (b, g, 0, 0))],
        out_shape=[jax.ShapeDtypeStruct((B, S, RET_V), F32),
                   jax.ShapeDtypeStruct((B, 2, qk_w, v_w), F32)],
        compiler_params=_cparams(("arbitrary", "arbitrary")),
        name="retention",
    )(q, k, v, state_bd, decay, cross, kdec, sdec)


def _state_to_blockdiag(state):
    B = state.shape[0]
    hh = RET_HEADS // 2
    s = state.reshape(B, 2, hh, RET_QK_DIM, RET_V_DIM)
    eye = jnp.eye(hh, dtype=state.dtype)
    bd = s[:, :, :, :, None, :] * eye[None, None, :, None, :, None]
    return bd.reshape(B, 2, hh * RET_QK_DIM, hh * RET_V_DIM)


def _blockdiag_to_state(bd):
    B = bd.shape[0]
    hh = RET_HEADS // 2
    s = bd.reshape(B, 2, hh, RET_QK_DIM, hh, RET_V_DIM)
    s = jnp.stack([s[:, :, h, :, h, :] for h in range(hh)], axis=2)
    return s.reshape(B, RET_HEADS, RET_QK_DIM, RET_V_DIM)


def _attn_prompt_kernel(q_ref, k_ref, v_ref, o_ref, acc_ref, m_ref, l_ref, *, seq):
    n = ATT_BLOCK
    head0 = lax.broadcasted_iota(jnp.int32, (n, LANES), 1) < ATT_HEAD_DIM
    head0_kv = lax.broadcasted_iota(jnp.int32, (2 * n, LANES), 1) < ATT_HEAD_DIM
    a = lax.broadcasted_iota(jnp.int32, (n, 2 * n), 0)
    j = lax.broadcasted_iota(jnp.int32, (n, 2 * n), 1)
    bias_pc = jnp.where(((j < n) & (j >= a)) | ((j >= n) & (j - n <= a)), 0.0, NEG_INF)
    bias_c = bias_pc[:, n:]

    for p_idx, (window, d) in enumerate(DILATED_PATTERNS):
        n_blocks = (seq // d) // n
        for r in range(d):
            for c in range(n_blocks):
                def rows(block):
                    start = block * n * d + r
                    return pl.ds(start, n, stride=d) if d > 1 else pl.ds(start, n)
                q = q_ref[0, rows(c), :]
                if c > 0:
                    kk = jnp.concatenate([k_ref[0, rows(c - 1), :], k_ref[0, rows(c), :]], axis=0)
                    vv = jnp.concatenate([v_ref[0, rows(c - 1), :], v_ref[0, rows(c), :]], axis=0)
                    bias, hkv = bias_pc, head0_kv
                else:
                    kk, vv = k_ref[0, rows(c), :], v_ref[0, rows(c), :]
                    bias, hkv = bias_c, head0
                kb = kk.astype(BF16)
                pv, mx = [], []
                for h in range(2):
                    mine = head0 if h == 0 else jnp.logical_not(head0)
                    mine_kv = hkv if h == 0 else jnp.logical_not(hkv)
                    s = _dot_nt(jnp.where(mine, q, 0.0).astype(BF16), kb) + bias
                    m = jnp.max(s, axis=1, keepdims=True)
                    p = jnp.exp(s - m)
                    pv.append(_dot(p.astype(BF16), jnp.where(mine_kv, vv, 1.0).astype(BF16)))
                    mx.append(m)
                acc_u = jnp.where(head0, pv[0], pv[1])
                l_u = pltpu.roll(jnp.where(head0, pv[1], pv[0]), ATT_HEAD_DIM, 1)
                m_u = jnp.where(head0, mx[0], mx[1])
                rs = rows(c)
                if p_idx == 0:
                    acc_ref[rs, :] = acc_u
                    l_ref[rs, :] = l_u
                    m_ref[rs, :] = m_u
                else:
                    m_old = m_ref[rs, :]
                    m_new = jnp.maximum(m_old, m_u)
                    w_old, w_u = jnp.exp(m_old - m_new), jnp.exp(m_u - m_new)
                    acc_ref[rs, :] = acc_ref[rs, :] * w_old + acc_u * w_u
                    l_ref[rs, :] = l_ref[rs, :] * w_old + l_u * w_u
                    m_ref[rs, :] = m_new
    o_ref[0] = acc_ref[...] / l_ref[...]


def _attention_prompt(q, k, v):
    B, S, _ = q.shape
    spec = pl.BlockSpec((1, S, LANES), lambda b, g: (b, 0, g))
    return pl.pallas_call(
        functools.partial(_attn_prompt_kernel, seq=S),
        grid=(B, ATT_W // LANES),
        in_specs=[spec, spec, spec],
        out_specs=spec,
        out_shape=jax.ShapeDtypeStruct((B, S, ATT_W), F32),
        scratch_shapes=[pltpu.VMEM((S, LANES), F32)] * 3,
        compiler_params=_cparams(("arbitrary", "arbitrary")),
        name="attention_prompt",
    )(q, k, v)


def _attn_sample_kernel(q_ref, kn_ref, vn_ref, ka_ref, va_ref, kb_ref, vb_ref,
                        ca_ref, cb_ref, cn_ref, o_ref):
    q = q_ref[0].astype(BF16)
    groups = ((ka_ref[0], va_ref[0], ca_ref[...]),
              (kb_ref[0].reshape(-1, ATT_HEAD_DIM), vb_ref[0].reshape(-1, ATT_HEAD_DIM), cb_ref[...]),
              (kn_ref[0], vn_ref[0], cn_ref[...]))
    scores = []
    for kk, _, cnt in groups:
        s = _dot_nt(q, kk.astype(BF16))
        scores.append(jnp.where(cnt > 0.0, s, NEG_INF))
    m = scores[0].max(axis=1, keepdims=True)
    for s in scores[1:]:
        m = jnp.maximum(m, s.max(axis=1, keepdims=True))
    num = jnp.zeros(o_ref.shape[1:], F32)
    den = jnp.zeros((o_ref.shape[1], 1), F32)
    for (kk, vv, cnt), s in zip(groups, scores):
        p = cnt * jnp.exp(s - m)
        den = den + p.sum(axis=1, keepdims=True)
        num = num + _dot(p.astype(BF16), vv.astype(BF16))
    o_ref[0] = num / den


def _sample_key_counts(dec_seq, buf):
    n = ATT_BLOCK
    t_q = np.repeat(np.arange(dec_seq), ATT_HEADS)[:, None]
    h_q = np.tile(np.arange(ATT_HEADS), dec_seq)[:, None]
    fine = [(w, d) for w, d in DILATED_PATTERNS if w < MAX_WINDOW]
    coarse = [(w, d) for w, d in DILATED_PATTERNS if w >= MAX_WINDOW]
    assert len(coarse) == 1 and coarse[0][0] == buf and dec_seq <= coarse[0][1]
    d_c = coarse[0][1]
    lo = buf - max(w for w, _ in fine)
    assert lo >= 0 and lo % d_c == 0
    pos = np.repeat(np.arange(lo, buf), ATT_HEADS)[None, :]
    h_k = np.tile(np.arange(ATT_HEADS), buf - lo)[None, :]
    cnt_a = np.zeros((dec_seq * ATT_HEADS, pos.shape[1]), np.float32)
    for w, d in fine:
        back = buf + t_q - pos
        cnt_a += ((back % d == 0) & (back >= d) & (back <= w)).astype(np.float32)
    cnt_a *= (h_q == h_k)
    g = np.arange(buf // d_c)
    res = np.arange(dec_seq)
    res_k = np.tile(np.repeat(res, ATT_HEADS), g.size)[None, :]
    h_k = np.tile(np.arange(ATT_HEADS), g.size * dec_seq)[None, :]
    cnt_b = ((res_k == t_q) & (h_k == h_q)).astype(np.float32)
    t_k = np.repeat(np.arange(dec_seq), ATT_HEADS)[None, :]
    h_k = np.tile(np.arange(ATT_HEADS), dec_seq)[None, :]
    cnt_n = np.zeros((dec_seq * ATT_HEADS, dec_seq * ATT_HEADS), np.float32)
    for w, d in DILATED_PATTERNS:
        back = t_q - t_k
        cnt_n += ((back >= 0) & (back % d == 0) & (back <= w)).astype(np.float32)
    cnt_n *= (h_q == h_k)
    return lo, d_c, jnp.asarray(cnt_a), jnp.asarray(cnt_b), jnp.asarray(cnt_n)


def _attention_sample(q, k_new, v_new, cache_k, cache_v):
    B, T, H, D = q.shape
    buf = cache_k.shape[1]
    lo, d_c, cnt_a, cnt_b, cnt_n = _sample_key_counts(T, buf)
    rows = T * H
    tail_rows = (buf - lo) * H
    flat = lambda x: x.reshape(B, -1, D)
    strided = lambda x: x.reshape(B, buf // d_c, d_c * H, D)
    row_spec = pl.BlockSpec((1, rows, D), lambda b: (b, 0, 0))
    tail_spec = pl.BlockSpec((1, tail_rows, D), lambda b: (b, lo * H // tail_rows, 0))
    strided_spec = pl.BlockSpec((1, buf // d_c, T * H, D), lambda b: (b, 0, 0, 0))
    const = lambda x: pl.BlockSpec(x.shape, lambda b: (0, 0))
    assert (lo * H) % tail_rows == 0
    out = pl.pallas_call(
        _attn_sample_kernel,
        grid=(B,),
        in_specs=[row_spec, row_spec, row_spec, tail_spec, tail_spec, strided_spec, strided_spec,
                  const(cnt_a), const(cnt_b), const(cnt_n)],
        out_specs=row_spec,
        out_shape=jax.ShapeDtypeStruct((B, rows, D), F32),
        compiler_params=_cparams(("arbitrary",)),
        name="attention_sample",
    )(flat(q), flat(k_new), flat(v_new), flat(cache_k), flat(cache_v), strided(cache_k), strided(cache_v),
      cnt_a, cnt_b, cnt_n)
    return out.reshape(B, T, H, D)


def _layer_norm(y, g, b):
    mu = jnp.mean(y, axis=-1, keepdims=True)
    d = y - mu
    var = jnp.mean(d * d, axis=-1, keepdims=True)
    return d * lax.rsqrt(var + LN_EPS) * g + b


def _mixer_kernel(oret_p, g_p, oatt_p, x_p, pe_p, oret_s, g_s, oatt_s, x_s, pe_s,
                  avg_ref, gnw_ref, gnb_ref, wor_ref, woa_ref, ln1g_ref, ln1b_ref,
                  wg_ref, bg_ref, wp_ref, wr_ref, br_ref,
                  x1r_ref, resid_ref, tope_ref, topg_ref, *, n_prompt_tiles):
    is_prompt = pl.program_id(0) < n_prompt_tiles
    pick = lambda a, b: jnp.where(is_prompt, a[...], b[...])
    o_ret, g, o_att, x, pe = (pick(oret_p, oret_s), pick(g_p, g_s), pick(oatt_p, oatt_s),
                              pick(x_p, x_s), pick(pe_p, pe_s))
    avg = avg_ref[...]

    def head_mean(z):
        hi = z.astype(BF16)
        lo = (z - hi.astype(F32)).astype(BF16)
        return _dot(hi, avg) + _dot(lo, avg)

    d = o_ret - head_mean(o_ret)
    o_n = d * lax.rsqrt(head_mean(d * d) + GN_EPS) * gnw_ref[...] + gnb_ref[...]
    ret = g * jax.nn.sigmoid(g) * o_n
    mix = _dot(ret.astype(BF16), wor_ref[...]) + _dot(o_att.astype(BF16), woa_ref[...])
    x1 = _layer_norm(DEEPNORM_ALPHA * x + mix, ln1g_ref[...], ln1b_ref[...])
    x1b = x1.astype(BF16)
    ple = jax.nn.sigmoid(_dot(x1b, wg_ref[...]) + bg_ref[...]) * _dot(pe.astype(BF16), wp_ref[...])
    resid_ref[...] = DEEPNORM_ALPHA * x1 + ple
    tm = x1.shape[0]
    for s in range(ROW_TILES):
        x1r_ref[pl.ds(s, tm, stride=ROW_TILES), :] = x1[:, s * LANES:(s + 1) * LANES]

    logits = jnp.dot(x1, wr_ref[...], preferred_element_type=F32, precision=lax.Precision.HIGHEST) + br_ref[...]
    lane = lax.broadcasted_iota(jnp.int32, logits.shape, 1).astype(F32)
    work = jnp.where(lane < N_EXPERTS, logits, -jnp.inf)
    vals, idxs = [], []
    for _ in range(TOP_K):
        m = jnp.max(work, axis=1, keepdims=True)
        idx = jnp.min(jnp.where(work == m, lane, float(LANES)), axis=1, keepdims=True)
        vals.append(m)
        idxs.append(idx)
        work = jnp.where(lane == idx, -jnp.inf, work)
    exps = [jnp.exp(v - vals[0]) for v in vals]
    den = exps[0]
    for e in exps[1:]:
        den = den + e
    tope = jnp.zeros(logits.shape, F32)
    topg = jnp.zeros(logits.shape, F32)
    for kk in range(TOP_K):
        tope = jnp.where(lane == kk, idxs[kk], tope)
        topg = jnp.where(lane == kk, exps[kk] / den, topg)
    tope_ref[...] = tope.astype(jnp.int32)
    topg_ref[...] = topg


def _mixer(prompt, sample, weights):
    (gn_w, gn_b, w_out, ln1_g, ln1_b, w_router, b_router, w_pl_gate, b_pl_gate, w_pl_proj) = weights
    tp, ts = prompt[3].shape[0], sample[3].shape[0]
    tm = TOKEN_TILE
    assert tp % tm == 0 and ts % tm == 0
    npt, nst = tp // tm, ts // tm
    t_all = tp + ts
    avg = jnp.asarray(np.kron(np.eye(RET_HEADS), np.full((RET_V_DIM, RET_V_DIM), 1.0 / RET_V_DIM)), BF16)
    w_out_b = w_out.astype(BF16)
    wr = jnp.zeros((D_MODEL, LANES), F32).at[:, :N_EXPERTS].set(w_router)
    br = jnp.zeros((1, LANES), F32).at[0, :N_EXPERTS].set(b_router)
    row = lambda v: v.reshape(1, -1)
    consts = [avg, row(gn_w), row(gn_b), w_out_b[:RET_V], w_out_b[RET_V:], row(ln1_g), row(ln1_b),
              w_pl_gate.astype(BF16), row(b_pl_gate), w_pl_proj.astype(BF16), wr, br]
    p_spec = lambda a: pl.BlockSpec((tm, a.shape[1]), lambda i: (jnp.minimum(i, npt - 1), 0))
    s_spec = lambda a: pl.BlockSpec((tm, a.shape[1]), lambda i: (jnp.maximum(i - npt, 0), 0))
    c_spec = lambda a: pl.BlockSpec(a.shape, lambda i: (0, 0))
    return pl.pallas_call(
        functools.partial(_mixer_kernel, n_prompt_tiles=npt),
        grid=(npt + nst,),
        in_specs=[p_spec(a) for a in prompt] + [s_spec(a) for a in sample] + [c_spec(a) for a in consts],
        out_specs=[pl.BlockSpec((tm * ROW_TILES, LANES), lambda i: (i, 0)),
                   pl.BlockSpec((tm, D_MODEL), lambda i: (i, 0)),
                   pl.BlockSpec((tm, LANES), lambda i: (i, 0)),
                   pl.BlockSpec((tm, LANES), lambda i: (i, 0))],
        out_shape=[jax.ShapeDtypeStruct((t_all * ROW_TILES, LANES), F32),
                   jax.ShapeDtypeStruct((t_all, D_MODEL), F32),
                   jax.ShapeDtypeStruct((t_all, LANES), jnp.int32),
                   jax.ShapeDtypeStruct((t_all, LANES), F32)],
        compiler_params=_cparams(("arbitrary",)),
        name="mixer_out",
    )(*prompt, *sample, *consts)


def _routing_plan(top_e, top_g, n_tokens):
    a = n_tokens * TOP_K
    flat_e = top_e.reshape(-1)
    flat_t = jnp.arange(a, dtype=jnp.int32) // TOP_K
    flat_g = top_g.reshape(-1)
    order = jnp.argsort(flat_e)
    se = flat_e[order]
    counts = jnp.zeros((N_EXPERTS,), jnp.int32).at[flat_e].add(1)
    padded = (counts + MOE_BLOCK - 1) // MOE_BLOCK * MOE_BLOCK
    start = jnp.cumsum(counts) - counts
    pend = jnp.cumsum(padded)
    pstart = pend - padded
    dest = pstart[se] + jnp.arange(a, dtype=jnp.int32) - start[se]
    n_blocks = -(-(a + N_EXPERTS * (MOE_BLOCK - 1)) // MOE_BLOCK)
    p = n_blocks * MOE_BLOCK
    slot_tok = jnp.zeros((p,), jnp.int32).at[dest].set(flat_t[order])
    slot_gate = jnp.zeros((p,), F32).at[dest].set(flat_g[order])
    block_e = jnp.minimum(jnp.searchsorted(pend, jnp.arange(n_blocks, dtype=jnp.int32) * MOE_BLOCK, side='right'),
                          N_EXPERTS - 1).astype(jnp.int32)
    n_used = (pend[-1] // MOE_BLOCK).astype(jnp.int32).reshape(1)
    slot_of = jnp.zeros((a,), jnp.int32).at[order].set(dest)
    return slot_tok.reshape(n_blocks, MOE_BLOCK), slot_gate.reshape(p, 1), block_e, n_used, slot_of


def _expert_kernel(be_ref, nused_ref, tok_hbm, x_hbm, gate_ref, w1g_ref, w1l_ref, b1g_ref, b1l_ref, w2_ref, b2_ref,
                   y_ref, xbuf, tok_smem, gsem, tsem, *, n_blocks):
    i = pl.program_id(0)
    n_used = nused_ref[0]
    mb = MOE_BLOCK
    slot = i % 2

    def fetch_tokens(block, s):
        return pltpu.make_async_copy(tok_hbm.at[block], tok_smem.at[s], tsem.at[s])

    def gather_done(s):
        return pltpu.make_async_copy(x_hbm.at[pl.ds(0, mb * ROW_TILES), :], xbuf.at[s], gsem.at[s])

    def start_gather(s):
        def body(r, carry):
            t = tok_smem[s, r]
            pltpu.make_async_copy(x_hbm.at[pl.ds(pl.multiple_of(t * ROW_TILES, ROW_TILES), ROW_TILES), :],
                                  xbuf.at[s, pl.ds(pl.multiple_of(r * ROW_TILES, ROW_TILES), ROW_TILES), :],
                                  gsem.at[s]).start()
            return carry
        lax.fori_loop(0, mb, body, 0)

    @pl.when(i == 0)
    def _():
        fetch_tokens(0, 0).start()
        fetch_tokens(0, 0).wait()
        start_gather(0)

        @pl.when(n_used > 1)
        def _():
            fetch_tokens(1, 1).start()

    @pl.when(i + 1 < n_used)
    def _():
        fetch_tokens(i + 1, 1 - slot).wait()
        start_gather(1 - slot)

    @pl.when(i + 2 < n_used)
    def _():
        fetch_tokens(i + 2, slot).start()

    @pl.when(i < n_used)
    def _():
        gather_done(slot).wait()
        x = jnp.concatenate([xbuf[slot, pl.ds(s, mb, stride=ROW_TILES), :] for s in range(ROW_TILES)],
                            axis=1).astype(BF16)
        glu = jnp.minimum(_dot(x, w1g_ref[0]) + b1g_ref[0], SWIGLU_LIMIT)
        lin = jnp.clip(_dot(x, w1l_ref[0]) + b1l_ref[0], -SWIGLU_LIMIT, SWIGLU_LIMIT)
        act = glu * jax.nn.sigmoid(SWIGLU_ALPHA * glu) * (lin + 1.0)
        y = (_dot(act.astype(BF16), w2_ref[0]) + b2_ref[0]) * gate_ref[...]
        for s in range(ROW_TILES):
            y_ref[pl.ds(s, mb, stride=ROW_TILES), :] = y[:, s * LANES:(s + 1) * LANES]

    @pl.when(i >= n_used)
    def _():
        y_ref[...] = jnp.zeros_like(y_ref)


def _experts(x1r, slot_tok, slot_gate, block_e, n_used, w_e1, b_e1, w_e2, b_e2):
    n_blocks = slot_tok.shape[0]
    mb = MOE_BLOCK
    w1 = w_e1.astype(BF16)
    w1g, w1l = w1[:, :, 0::2], w1[:, :, 1::2]
    b1g, b1l = b_e1[:, None, 0::2], b_e1[:, None, 1::2]
    w2 = w_e2.astype(BF16)
    b2 = b_e2[:, None, :]
    wspec = pl.BlockSpec((1, D_MODEL, D_EXPERT), lambda i, be, nu: (be[i], 0, 0))
    bspec = pl.BlockSpec((1, 1, D_EXPERT), lambda i, be, nu: (be[i], 0, 0))
    return pl.pallas_call(
        functools.partial(_expert_kernel, n_blocks=n_blocks),
        grid_spec=pltpu.PrefetchScalarGridSpec(
            num_scalar_prefetch=2,
            grid=(n_blocks,),
            in_specs=[pl.BlockSpec(memory_space=pl.ANY),
                      pl.BlockSpec(memory_space=pl.ANY),
                      pl.BlockSpec((mb, 1), lambda i, be, nu: (i, 0)),
                      wspec, wspec, bspec, bspec,
                      pl.BlockSpec((1, D_EXPERT, D_MODEL), lambda i, be, nu: (be[i], 0, 0)),
                      pl.BlockSpec((1, 1, D_MODEL), lambda i, be, nu: (be[i], 0, 0))],
            out_specs=pl.BlockSpec((mb * ROW_TILES, LANES), lambda i, be, nu: (i, 0)),
            scratch_shapes=[pltpu.VMEM((2, mb * ROW_TILES, LANES), F32),
                            pltpu.SMEM((2, mb), jnp.int32),
                            pltpu.SemaphoreType.DMA((2,)),
                            pltpu.SemaphoreType.DMA((2,))]),
        out_shape=jax.ShapeDtypeStruct((n_blocks * mb * ROW_TILES, LANES), F32),
        compiler_params=_cparams(("arbitrary",)),
        name="experts",
    )(block_e, n_used, slot_tok, x1r, slot_gate, w1g, w1l, b1g, b1l, w2, b2)


def _combine_kernel(slot_hbm, y_hbm, resid_ref, g_ref, b_ref, o_ref, ybuf, slot_smem, gsem, ssem, *, n_tiles):
    i = pl.program_id(0)
    tm = COMBINE_TILE
    n_rows = tm * TOP_K
    slot = i % 2

    def fetch_slots(tile, s):
        return pltpu.make_async_copy(slot_hbm.at[tile], slot_smem.at[s], ssem.at[s])

    def gather_done(s):
        return pltpu.make_async_copy(y_hbm.at[pl.ds(0, n_rows * ROW_TILES), :], ybuf.at[s], gsem.at[s])

    def start_gather(s):
        def body(r, carry):
            src = slot_smem[s, r]
            pltpu.make_async_copy(y_hbm.at[pl.ds(pl.multiple_of(src * ROW_TILES, ROW_TILES), ROW_TILES), :],
                                  ybuf.at[s, pl.ds(pl.multiple_of(r * ROW_TILES, ROW_TILES), ROW_TILES), :],
                                  gsem.at[s]).start()
            return carry
        lax.fori_loop(0, n_rows, body, 0)

    @pl.when(i == 0)
    def _():
        fetch_slots(0, 0).start()
        fetch_slots(0, 0).wait()
        start_gather(0)
        if n_tiles > 1:
            fetch_slots(1, 1).start()

    @pl.when(i + 1 < n_tiles)
    def _():
        fetch_slots(i + 1, 1 - slot).wait()
        start_gather(1 - slot)

    @pl.when(i + 2 < n_tiles)
    def _():
        fetch_slots(i + 2, slot).start()

    gather_done(slot).wait()
    z = resid_ref[...]
    for kk in range(TOP_K):
        z = z + jnp.concatenate(
            [ybuf[slot, pl.ds(kk * tm * ROW_TILES + s, tm, stride=ROW_TILES), :] for s in range(ROW_TILES)], axis=1)
    o_ref[...] = _layer_norm(z, g_ref[...], b_ref[...])


def _combine(slot_of, ysr, resid, ln2_g, ln2_b):
    t_all = resid.shape[0]
    tm = COMBINE_TILE
    assert t_all % tm == 0
    n_tiles = t_all // tm
    slots = slot_of.reshape(n_tiles, tm, TOP_K).transpose(0, 2, 1).reshape(n_tiles, tm * TOP_K)
    return pl.pallas_call(
        functools.partial(_combine_kernel, n_tiles=n_tiles),
        grid=(n_tiles,),
        in_specs=[pl.BlockSpec(memory_space=pl.ANY),
                  pl.BlockSpec(memory_space=pl.ANY),
                  pl.BlockSpec((tm, D_MODEL), lambda i: (i, 0)),
                  pl.BlockSpec((1, D_MODEL), lambda i: (0, 0)),
                  pl.BlockSpec((1, D_MODEL), lambda i: (0, 0))],
        out_specs=pl.BlockSpec((tm, D_MODEL), lambda i: (i, 0)),
        out_shape=jax.ShapeDtypeStruct((t_all, D_MODEL), F32),
        scratch_shapes=[pltpu.VMEM((2, tm * TOP_K * ROW_TILES, LANES), F32),
                        pltpu.SMEM((2, tm * TOP_K), jnp.int32),
                        pltpu.SemaphoreType.DMA((2,)),
                        pltpu.SemaphoreType.DMA((2,))],
        compiler_params=_cparams(("arbitrary",)),
        name="combine_ln2",
    )(slots, ysr, resid, ln2_g.reshape(1, -1), ln2_b.reshape(1, -1))


def _pad_rows(x, rows):
    return jnp.pad(x, ((0, 0), (0, rows - x.shape[1]), (0, 0)))


def kernel(x_prompt, x_sample, cache_att_k, cache_att_v, state_ret, p_prompt, p_sample, w_in, ret_gn_w, ret_gn_b, w_out, ln1_g, ln1_b, w_router, b_router, w_e1, b_e1, w_e2, b_e2, w_pl_gate, b_pl_gate, w_pl_proj, ln2_g, ln2_b):
    assert w_in.shape[0] == DEPTH == 1
    B, S, _ = x_prompt.shape
    DB, T, _ = x_sample.shape
    l = 0
    w_in_b = w_in[l].astype(BF16)

    pos_p = jnp.arange(S, dtype=F32)
    qr, kr, vr, gr, qa, ka, va = _in_projection(x_prompt, w_in_b, pos_p)
    zero_state = jnp.zeros((B, 2, RET_QK // 2, RET_V // 2), F32)
    o_ret_p, st_p = _retention(qr, kr, vr, zero_state, RET_CHUNK, RET_CHUNK)
    o_att_p = _attention_prompt(qa, ka, va)
    keep = min(MAX_WINDOW, S)
    new_k_p = ka[:, S - keep:].reshape(1, B, keep, ATT_HEADS, ATT_HEAD_DIM)
    new_v_p = va[:, S - keep:].reshape(1, B, keep, ATT_HEADS, ATT_HEAD_DIM)
    new_st_p = _blockdiag_to_state(st_p)[None]

    ts = DB * T
    pos_s = jnp.tile(PAST_LEN + jnp.arange(T, dtype=F32), DB)
    sqr, skr, svr, sgr, sqa, ska, sva = _in_projection(x_sample.reshape(1, ts, D_MODEL), w_in_b, pos_s)
    c_pad = 16
    per_b = lambda a: _pad_rows(a.reshape(DB, T, -1), c_pad)
    o_ret_s, st_s = _retention(per_b(sqr), per_b(skr), per_b(svr), _state_to_blockdiag(state_ret[l]), T, c_pad)
    o_ret_s = o_ret_s[:, :T].reshape(ts, RET_V)
    heads = lambda a: a.reshape(DB, T, ATT_HEADS, ATT_HEAD_DIM)
    o_att_s = _attention_sample(heads(sqa), heads(ska), heads(sva), cache_att_k[l], cache_att_v[l])
    new_k_s = heads(ska)[None]
    new_v_s = heads(sva)[None]
    new_st_s = _blockdiag_to_state(st_s).astype(state_ret.dtype)[None]

    tp = B * S
    flat2 = lambda a: a.reshape(-1, a.shape[-1])
    prompt = (flat2(o_ret_p), flat2(gr), flat2(o_att_p), flat2(x_prompt), flat2(p_prompt[l]))
    sample = (o_ret_s, flat2(sgr), o_att_s.reshape(ts, ATT_W), flat2(x_sample), flat2(p_sample[l]))
    x1r, resid, tope, topg = _mixer(prompt, sample, (ret_gn_w[l], ret_gn_b[l], w_out[l], ln1_g[l], ln1_b[l],
                                                     w_router[l], b_router[l], w_pl_gate[l], b_pl_gate[l],
                                                     w_pl_proj[l]))

    t_all = tp + ts
    slot_tok, slot_gate, block_e, n_used, slot_of = _routing_plan(tope[:, :TOP_K], topg[:, :TOP_K], t_all)
    ysr = _experts(x1r, slot_tok, slot_gate, block_e, n_used, w_e1[l], b_e1[l], w_e2[l], b_e2[l])
    y = _combine(slot_of, ysr, resid, ln2_g[l], ln2_b[l])
    y_prompt = y[:tp].reshape(B, S, D_MODEL)
    y_sample = y[tp:].reshape(DB, T, D_MODEL)
    return (y_prompt, y_sample, new_k_p, new_v_p, new_st_p, new_k_s, new_v_s, new_st_s)
```

```python
import functools
import math

import numpy as np
import jax
import jax.numpy as jnp
from jax import lax
from jax.experimental import pallas as pl
from jax.experimental.pallas import tpu as pltpu

F32 = jnp.float32
BF16 = jnp.bfloat16

D_MODEL = 1024
DEPTH = 1
PAST_LEN = 8192
RET_HEADS = 8
RET_QK_DIM = 32
RET_V_DIM = 64
RET_CHUNK = 128
RET_ROPE_BASE = 10000.0
ATT_HEADS = 8
ATT_HEAD_DIM = 64
ROT_DIMS = ATT_HEAD_DIM // 4
ROPE_THETA = 500000.0
DILATED_PATTERNS = ((128, 1), (512, 4), (2048, 16))
MAX_WINDOW = 2048
RET_QK = RET_HEADS * RET_QK_DIM
RET_V = RET_HEADS * RET_V_DIM
ATT_W = ATT_HEADS * ATT_HEAD_DIM
IN_COLS = RET_QK * 2 + RET_V * 2 + ATT_W * 3
N_EXPERTS = 32
TOP_K = 4
D_EXPERT = D_MODEL
SWIGLU_ALPHA = 1.702
SWIGLU_LIMIT = 7.0
PLE_DIM = 256
LN_EPS = 1e-5
GN_EPS = 1e-6
DEEPNORM_ALPHA = (2 * DEPTH) ** 0.25
NEG_INF = -1e30

LANES = 128
SUBLANES = 8
ROW_TILES = D_MODEL // LANES
VMEM_LIMIT = 56 * 1024 * 1024

TOKEN_TILE = 512
MOE_BLOCK = 256
COMBINE_TILE = 256
ATT_BLOCK = 128


def _dot(a, b):
    return jnp.dot(a, b, preferred_element_type=F32)


def _dot_nt(a, b):
    return lax.dot_general(a, b, (((1,), (1,)), ((), ())), preferred_element_type=F32)


def _dot_tn(a, b):
    return lax.dot_general(a, b, (((0,), (0,)), ((), ())), preferred_element_type=F32)


def _cparams(sem):
    return pltpu.CompilerParams(dimension_semantics=sem, vmem_limit_bytes=VMEM_LIMIT)


def _rotary_tables(pos, inv_freq, head_dim, n_heads):
    half = inv_freq.shape[0]
    ang = pos.astype(F32)[:, None] * inv_freq[None, :]
    cos, sin = jnp.cos(ang), jnp.sin(ang)
    rest = head_dim - 2 * half
    n = pos.shape[0]
    cos_h = jnp.concatenate([cos, cos, jnp.ones((n, rest), F32)], axis=1)
    sin_h = jnp.concatenate([-sin, sin, jnp.zeros((n, rest), F32)], axis=1)
    return jnp.tile(cos_h, (1, n_heads)), jnp.tile(sin_h, (1, n_heads))


def _ret_inv_freq():
    return 1.0 / (RET_ROPE_BASE ** jnp.linspace(0.0, 1.0, RET_QK_DIM // 2, dtype=F32))


def _att_inv_freq():
    return ROPE_THETA ** (-jnp.arange(0, ROT_DIMS, 2, dtype=F32) / ROT_DIMS)


def _ret_log_decay():
    return jnp.log(1.0 - 2.0 ** (-5.0 - jnp.arange(RET_HEADS, dtype=F32)))


def _retention_tables(c_true, c_pad):
    lg = _ret_log_decay()
    idx = jnp.arange(c_pad, dtype=F32)
    live = idx < c_true
    diff = idx[:, None] - idx[None, :]
    decay = jnp.where(diff[None] >= 0, jnp.exp(jnp.maximum(diff, 0.0)[None] * lg[:, None, None]), 0.0)
    decay = jnp.where(live[None, :, None] & live[None, None, :], decay, 0.0)
    cross = jnp.exp((idx + 1.0)[:, None] * lg[None, :])
    cross = jnp.where(live[:, None], cross, 0.0)
    kdec = jnp.exp((c_true - 1.0 - idx)[:, None] * lg[None, :])
    kdec = jnp.where(live[:, None], kdec, 0.0)
    sdec = jnp.exp(c_true * lg)
    hh = RET_HEADS // 2
    decay = decay.reshape(2, hh, c_pad, c_pad)
    cross = jnp.repeat(cross, RET_V_DIM, axis=1).reshape(c_pad, 2, hh * RET_V_DIM).transpose(1, 0, 2)
    kdec = jnp.repeat(kdec, RET_QK_DIM, axis=1).reshape(c_pad, 2, hh * RET_QK_DIM).transpose(1, 0, 2)
    sdec = jnp.broadcast_to(jnp.repeat(sdec, RET_V_DIM)[None, :], (hh * RET_QK_DIM, RET_V)) \
        .reshape(hh * RET_QK_DIM, 2, hh * RET_V_DIM).transpose(1, 0, 2)
    return decay, cross, kdec, sdec


def _rotate(h, cos, sin, half, period):
    outs = []
    for j in range(h.shape[1] // LANES):
        blk = h[:, j * LANES:(j + 1) * LANES]
        lane = lax.broadcasted_iota(jnp.int32, blk.shape, 1)
        first = (lane % period) < half
        partner = jnp.where(first, pltpu.roll(blk, LANES - half, 1), pltpu.roll(blk, half, 1))
        outs.append(blk * cos[:, j * LANES:(j + 1) * LANES] + partner * sin[:, j * LANES:(j + 1) * LANES])
    return jnp.concatenate(outs, axis=1)


def _inproj_kernel(x_ref, w_ref, cr_ref, sr_ref, ca_ref, sa_ref,
                   qr_ref, kr_ref, vr_ref, gr_ref, qa_ref, ka_ref, va_ref, ka4_ref, va4_ref):
    x = x_ref[0].astype(BF16)

    def proj(c0, width):
        return _dot(x, w_ref[:, c0:c0 + width])

    cr, sr, ca, sa = cr_ref[...], sr_ref[...], ca_ref[...], sa_ref[...]
    c = 0
    qr_ref[0] = _rotate(proj(c, RET_QK), cr, sr, RET_QK_DIM // 2, RET_QK_DIM)
    c += RET_QK
    kr_ref[0] = _rotate(proj(c, RET_QK), cr, sr, RET_QK_DIM // 2, RET_QK_DIM) * (RET_QK_DIM ** -0.5)
    c += RET_QK
    vr_ref[0] = proj(c, RET_V)
    c += RET_V
    gr_ref[0] = proj(c, RET_V)
    c += RET_V
    qa_ref[0] = _rotate(proj(c, ATT_W), ca, sa, ROT_DIMS // 2, ATT_HEAD_DIM) * (ATT_HEAD_DIM ** -0.5)
    c += ATT_W
    ka = _rotate(proj(c, ATT_W), ca, sa, ROT_DIMS // 2, ATT_HEAD_DIM)
    c += ATT_W
    va = proj(c, ATT_W)
    ka_ref[0] = ka
    va_ref[0] = va
    for h in range(ATT_HEADS):
        ka4_ref[0, :, h, :] = ka[:, h * ATT_HEAD_DIM:(h + 1) * ATT_HEAD_DIM]
        va4_ref[0, :, h, :] = va[:, h * ATT_HEAD_DIM:(h + 1) * ATT_HEAD_DIM]


def _in_projection(x, w_in_b, pos):
    B, S, _ = x.shape
    tm = min(TOKEN_TILE, S)
    cr, sr = _rotary_tables(pos, _ret_inv_freq(), RET_QK_DIM, RET_HEADS)
    ca, sa = _rotary_tables(pos, _att_inv_freq(), ATT_HEAD_DIM, ATT_HEADS)
    widths = (RET_QK, RET_QK, RET_V, RET_V, ATT_W, ATT_W, ATT_W)
    tab = lambda w: pl.BlockSpec((tm, w), lambda s, b: (s, 0))
    return pl.pallas_call(
        _inproj_kernel,
        grid=(S // tm, B),
        in_specs=[pl.BlockSpec((1, tm, D_MODEL), lambda s, b: (b, s, 0)),
                  pl.BlockSpec((D_MODEL, IN_COLS), lambda s, b: (0, 0)),
                  tab(RET_QK), tab(RET_QK), tab(ATT_W), tab(ATT_W)],
        out_specs=[pl.BlockSpec((1, tm, w), lambda s, b: (b, s, 0)) for w in widths]
        + [pl.BlockSpec((1, tm, ATT_HEADS, ATT_HEAD_DIM), lambda s, b: (b, s, 0, 0))] * 2,
        out_shape=[jax.ShapeDtypeStruct((B, S, w), F32) for w in widths]
        + [jax.ShapeDtypeStruct((B, S, ATT_HEADS, ATT_HEAD_DIM), F32)] * 2,
        compiler_params=_cparams(("arbitrary", "arbitrary")),
        name="in_projection",
    )(x, w_in_b, cr, sr, ca, sa)


def _retention_kernel(q_ref, k_ref, v_ref, s0_ref, dec_ref, cross_ref, kdec_ref, sdec_ref,
                      o_ref, so_ref, *, chunk, n_chunks):
    hh = RET_HEADS // 2
    qk_w, v_w = hh * RET_QK_DIM, hh * RET_V_DIM
    lane_q = lax.broadcasted_iota(jnp.int32, (chunk, qk_w), 1) // RET_QK_DIM
    lane_v = lax.broadcasted_iota(jnp.int32, (chunk, v_w), 1) // RET_V_DIM
    blockdiag = (lax.broadcasted_iota(jnp.int32, (qk_w, v_w), 0) // RET_QK_DIM
                 == lax.broadcasted_iota(jnp.int32, (qk_w, v_w), 1) // RET_V_DIM)
    state = s0_ref[0, 0]
    cross, kdec, sdec = cross_ref[0], kdec_ref[0], sdec_ref[0]
    for ci in range(n_chunks):
        rows = pl.ds(ci * chunk, chunk)
        q, k, v = q_ref[0, rows, :], k_ref[0, rows, :], v_ref[0, rows, :]
        kb, vb = k.astype(BF16), v.astype(BF16)
        o = _dot(q.astype(BF16), state.astype(BF16)) * cross
        for h in range(hh):
            qm = jnp.where(lane_q == h, q, 0.0).astype(BF16)
            inner = _dot_nt(qm, kb) * dec_ref[0, h]
            o = jnp.where(lane_v == h, o + _dot(inner.astype(BF16), vb), o)
        o_ref[0, rows, :] = o
        upd = _dot_tn((k * kdec).astype(BF16), vb)
        state = state * sdec + jnp.where(blockdiag, upd, 0.0)
    so_ref[0, 0] = state


def _retention(q, k, v, state_bd, c_true, c_pad):
    B, S, _ = q.shape
    hh = RET_HEADS // 2
    qk_w, v_w = hh * RET_QK_DIM, hh * RET_V_DIM
    decay, cross, kdec, sdec = _retention_tables(c_true, c_pad)
    kern = functools.partial(_retention_kernel, chunk=c_pad, n_chunks=S // c_pad)
    return pl.pallas_call(
        kern,
        grid=(B, 2),
        in_specs=[pl.BlockSpec((1, S, qk_w), lambda b, g: (b, 0, g)),
                  pl.BlockSpec((1, S, qk_w), lambda b, g: (b, 0, g)),
                  pl.BlockSpec((1, S, v_w), lambda b, g: (b, 0, g)),
                  pl.BlockSpec((1, 1, qk_w, v_w), lambda b, g: (b, g, 0, 0)),
                  pl.BlockSpec((1, hh, c_pad, c_pad), lambda b, g: (g, 0, 0, 0)),
                  pl.BlockSpec((1, c_pad, v_w), lambda b, g: (g, 0, 0)),
                  pl.BlockSpec((1, c_pad, qk_w), lambda b, g: (g, 0, 0)),
                  pl.BlockSpec((1, qk_w, v_w), lambda b, g: (g, 0, 0))],
        out_specs=[pl.BlockSpec((1, S, v_w), lambda b, g: (b, 0, g)),
                   pl.BlockSpec((1, 1, qk_w, v_w), lambda b, g: (b, g, 0, 0))],
        out_shape=[jax.ShapeDtypeStruct((B, S, RET_V), F32),
                   jax.ShapeDtypeStruct((B, 2, qk_w, v_w), F32)],
        compiler_params=_cparams(("arbitrary", "arbitrary")),
        name="retention",
    )(q, k, v, state_bd, decay, cross, kdec, sdec)


def _state_to_blockdiag(state):
    B = state.shape[0]
    hh = RET_HEADS // 2
    s = state.reshape(B, 2, hh, RET_QK_DIM, RET_V_DIM)
    eye = jnp.eye(hh, dtype=state.dtype)
    bd = s[:, :, :, :, None, :] * eye[None, None, :, None, :, None]
    return bd.reshape(B, 2, hh * RET_QK_DIM, hh * RET_V_DIM)


def _blockdiag_to_state(bd):
    B = bd.shape[0]
    hh = RET_HEADS // 2
    s = bd.reshape(B, 2, hh, RET_QK_DIM, hh, RET_V_DIM)
    s = jnp.stack([s[:, :, h, :, h, :] for h in range(hh)], axis=2)
    return s.reshape(B, RET_HEADS, RET_QK_DIM, RET_V_DIM)


def _attn_prompt_kernel(q_ref, k_ref, v_ref, o_ref, acc_ref, m_ref, l_ref, *, seq):
    n = ATT_BLOCK
    head0 = lax.broadcasted_iota(jnp.int32, (n, LANES), 1) < ATT_HEAD_DIM
    head0_kv = lax.broadcasted_iota(jnp.int32, (2 * n, LANES), 1) < ATT_HEAD_DIM
    a = lax.broadcasted_iota(jnp.int32, (n, 2 * n), 0)
    j = lax.broadcasted_iota(jnp.int32, (n, 2 * n), 1)
    bias_pc = jnp.where(((j < n) & (j >= a)) | ((j >= n) & (j - n <= a)), 0.0, NEG_INF)
    bias_c = bias_pc[:, n:]

    for p_idx, (window, d) in enumerate(DILATED_PATTERNS):
        n_blocks = (seq // d) // n
        for r in range(d):
            for c in range(n_blocks):
                def rows(block):
                    start = block * n * d + r
                    return pl.ds(start, n, stride=d) if d > 1 else pl.ds(start, n)
                q = q_ref[0, rows(c), :]
                if c > 0:
                    kk = jnp.concatenate([k_ref[0, rows(c - 1), :], k_ref[0, rows(c), :]], axis=0)
                    vv = jnp.concatenate([v_ref[0, rows(c - 1), :], v_ref[0, rows(c), :]], axis=0)
                    bias, hkv = bias_pc, head0_kv
                else:
                    kk, vv = k_ref[0, rows(c), :], v_ref[0, rows(c), :]
                    bias, hkv = bias_c, head0
                kb = kk.astype(BF16)
                pv, mx = [], []
                for h in range(2):
                    mine = head0 if h == 0 else jnp.logical_not(head0)
                    mine_kv = hkv if h == 0 else jnp.logical_not(hkv)
                    s = _dot_nt(jnp.where(mine, q, 0.0).astype(BF16), kb) + bias
                    m = jnp.max(s, axis=1, keepdims=True)
                    p = jnp.exp(s - m)
                    pv.append(_dot(p.astype(BF16), jnp.where(mine_kv, vv, 1.0).astype(BF16)))
                    mx.append(m)
                acc_u = jnp.where(head0, pv[0], pv[1])
                l_u = pltpu.roll(jnp.where(head0, pv[1], pv[0]), ATT_HEAD_DIM, 1)
                m_u = jnp.where(head0, mx[0], mx[1])
                rs = rows(c)
                if p_idx == 0:
                    acc_ref[rs, :] = acc_u
                    l_ref[rs, :] = l_u
                    m_ref[rs, :] = m_u
                else:
                    m_old = m_ref[rs, :]
                    m_new = jnp.maximum(m_old, m_u)
                    w_old, w_u = jnp.exp(m_old - m_new), jnp.exp(m_u - m_new)
                    acc_ref[rs, :] = acc_ref[rs, :] * w_old + acc_u * w_u
                    l_ref[rs, :] = l_ref[rs, :] * w_old + l_u * w_u
                    m_ref[rs, :] = m_new
    o_ref[0] = acc_ref[...] / l_ref[...]


def _attention_prompt(q, k, v):
    B, S, _ = q.shape
    spec = pl.BlockSpec((1, S, LANES), lambda b, g: (b, 0, g))
    return pl.pallas_call(
        functools.partial(_attn_prompt_kernel, seq=S),
        grid=(B, ATT_W // LANES),
        in_specs=[spec, spec, spec],
        out_specs=spec,
        out_shape=jax.ShapeDtypeStruct((B, S, ATT_W), F32),
        scratch_shapes=[pltpu.VMEM((S, LANES), F32)] * 3,
        compiler_params=_cparams(("arbitrary", "arbitrary")),
        name="attention_prompt",
    )(q, k, v)


def _attn_sample_kernel(q_ref, kn_ref, vn_ref, ka_ref, va_ref, kb_ref, vb_ref,
                        ca_ref, cb_ref, cn_ref, o_ref):
    q = q_ref[0].astype(BF16)
    groups = ((ka_ref[0], va_ref[0], ca_ref[...]),
              (kb_ref[0].reshape(-1, ATT_HEAD_DIM), vb_ref[0].reshape(-1, ATT_HEAD_DIM), cb_ref[...]),
              (kn_ref[0], vn_ref[0], cn_ref[...]))
    scores = []
    for kk, _, cnt in groups:
        s = _dot_nt(q, kk.astype(BF16))
        scores.append(jnp.where(cnt > 0.0, s, NEG_INF))
    m = scores[0].max(axis=1, keepdims=True)
    for s in scores[1:]:
        m = jnp.maximum(m, s.max(axis=1, keepdims=True))
    num = jnp.zeros(o_ref.shape[1:], F32)
    den = jnp.zeros((o_ref.shape[1], 1), F32)
    for (kk, vv, cnt), s in zip(groups, scores):
        p = cnt * jnp.exp(s - m)
        den = den + p.sum(axis=1, keepdims=True)
        num = num + _dot(p.astype(BF16), vv.astype(BF16))
    o_ref[0] = num / den


def _sample_key_counts(dec_seq, buf):
    n = ATT_BLOCK
    t_q = np.repeat(np.arange(dec_seq), ATT_HEADS)[:, None]
    h_q = np.tile(np.arange(ATT_HEADS), dec_seq)[:, None]
    fine = [(w, d) for w, d in DILATED_PATTERNS if w < MAX_WINDOW]
    coarse = [(w, d) for w, d in DILATED_PATTERNS if w >= MAX_WINDOW]
    assert len(coarse) == 1 and coarse[0][0] == buf and dec_seq <= coarse[0][1]
    d_c = coarse[0][1]
    lo = buf - max(w for w, _ in fine)
    assert lo >= 0 and lo % d_c == 0
    pos = np.repeat(np.arange(lo, buf), ATT_HEADS)[None, :]
    h_k = np.tile(np.arange(ATT_HEADS), buf - lo)[None, :]
    cnt_a = np.zeros((dec_seq * ATT_HEADS, pos.shape[1]), np.float32)
    for w, d in fine:
        back = buf + t_q - pos
        cnt_a += ((back % d == 0) & (back >= d) & (back <= w)).astype(np.float32)
    cnt_a *= (h_q == h_k)
    g = np.arange(buf // d_c)
    res = np.arange(dec_seq)
    res_k = np.tile(np.repeat(res, ATT_HEADS), g.size)[None, :]
    h_k = np.tile(np.arange(ATT_HEADS), g.size * dec_seq)[None, :]
    cnt_b = ((res_k == t_q) & (h_k == h_q)).astype(np.float32)
    t_k = np.repeat(np.arange(dec_seq), ATT_HEADS)[None, :]
    h_k = np.tile(np.arange(ATT_HEADS), dec_seq)[None, :]
    cnt_n = np.zeros((dec_seq * ATT_HEADS, dec_seq * ATT_HEADS), np.float32)
    for w, d in DILATED_PATTERNS:
        back = t_q - t_k
        cnt_n += ((back >= 0) & (back % d == 0) & (back <= w)).astype(np.float32)
    cnt_n *= (h_q == h_k)
    return lo, d_c, jnp.asarray(cnt_a), jnp.asarray(cnt_b), jnp.asarray(cnt_n)


def _attention_sample(q, k_new, v_new, cache_k, cache_v):
    B, T, H, D = q.shape
    buf = cache_k.shape[1]
    lo, d_c, cnt_a, cnt_b, cnt_n = _sample_key_counts(T, buf)
    rows = T * H
    tail_rows = (buf - lo) * H
    flat = lambda x: x.reshape(B, -1, D)
    strided = lambda x: x.reshape(B, buf // d_c, d_c * H, D)
    row_spec = pl.BlockSpec((1, rows, D), lambda b: (b, 0, 0))
    tail_spec = pl.BlockSpec((1, tail_rows, D), lambda b: (b, lo * H // tail_rows, 0))
    strided_spec = pl.BlockSpec((1, buf // d_c, T * H, D), lambda b: (b, 0, 0, 0))
    const = lambda x: pl.BlockSpec(x.shape, lambda b: (0, 0))
    assert (lo * H) % tail_rows == 0
    out = pl.pallas_call(
        _attn_sample_kernel,
        grid=(B,),
        in_specs=[row_spec, row_spec, row_spec, tail_spec, tail_spec, strided_spec, strided_spec,
                  const(cnt_a), const(cnt_b), const(cnt_n)],
        out_specs=row_spec,
        out_shape=jax.ShapeDtypeStruct((B, rows, D), F32),
        compiler_params=_cparams(("arbitrary",)),
        name="attention_sample",
    )(flat(q), flat(k_new), flat(v_new), flat(cache_k), flat(cache_v), strided(cache_k), strided(cache_v),
      cnt_a, cnt_b, cnt_n)
    return out.reshape(B, T, H, D)


def _layer_norm(y, g, b):
    mu = jnp.mean(y, axis=-1, keepdims=True)
    d = y - mu
    var = jnp.mean(d * d, axis=-1, keepdims=True)
    return d * lax.rsqrt(var + LN_EPS) * g + b


def _mixer_kernel(oret_p, g_p, oatt_p, x_p, pe_p, oret_s, g_s, oatt_s, x_s, pe_s,
                  avg_ref, gnw_ref, gnb_ref, wor_ref, woa_ref, ln1g_ref, ln1b_ref,
                  wg_ref, bg_ref, wp_ref, wr_ref, br_ref,
                  x1r_ref, resid_ref, tope_ref, topg_ref, *, n_prompt_tiles):
    is_prompt = pl.program_id(0) < n_prompt_tiles
    pick = lambda a, b: jnp.where(is_prompt, a[...], b[...])
    o_ret, g, o_att, x, pe = (pick(oret_p, oret_s), pick(g_p, g_s), pick(oatt_p, oatt_s),
                              pick(x_p, x_s), pick(pe_p, pe_s))
    avg = avg_ref[...]

    def head_mean(z):
        hi = z.astype(BF16)
        lo = (z - hi.astype(F32)).astype(BF16)
        return _dot(hi, avg) + _dot(lo, avg)

    d = o_ret - head_mean(o_ret)
    o_n = d * lax.rsqrt(head_mean(d * d) + GN_EPS) * gnw_ref[...] + gnb_ref[...]
    ret = g * jax.nn.sigmoid(g) * o_n
    mix = _dot(ret.astype(BF16), wor_ref[...]) + _dot(o_att.astype(BF16), woa_ref[...])
    x1 = _layer_norm(DEEPNORM_ALPHA * x + mix, ln1g_ref[...], ln1b_ref[...])
    x1b = x1.astype(BF16)
    ple = jax.nn.sigmoid(_dot(x1b, wg_ref[...]) + bg_ref[...]) * _dot(pe.astype(BF16), wp_ref[...])
    resid_ref[...] = DEEPNORM_ALPHA * x1 + ple
    tm = x1.shape[0]
    for s in range(ROW_TILES):
        x1r_ref[pl.ds(s, tm, stride=ROW_TILES), :] = x1[:, s * LANES:(s + 1) * LANES]

    logits = jnp.dot(x1, wr_ref[...], preferred_element_type=F32, precision=lax.Precision.HIGHEST) + br_ref[...]
    lane = lax.broadcasted_iota(jnp.int32, logits.shape, 1).astype(F32)
    work = jnp.where(lane < N_EXPERTS, logits, -jnp.inf)
    vals, idxs = [], []
    for _ in range(TOP_K):
        m = jnp.max(work, axis=1, keepdims=True)
        idx = jnp.min(jnp.where(work == m, lane, float(LANES)), axis=1, keepdims=True)
        vals.append(m)
        idxs.append(idx)
        work = jnp.where(lane == idx, -jnp.inf, work)
    exps = [jnp.exp(v - vals[0]) for v in vals]
    den = exps[0]
    for e in exps[1:]:
        den = den + e
    tope = jnp.zeros(logits.shape, F32)
    topg = jnp.zeros(logits.shape, F32)
    for kk in range(TOP_K):
        tope = jnp.where(lane == kk, idxs[kk], tope)
        topg = jnp.where(lane == kk, exps[kk] / den, topg)
    tope_ref[...] = tope.astype(jnp.int32)
    topg_ref[...] = topg


def _mixer(prompt, sample, weights):
    (gn_w, gn_b, w_out, ln1_g, ln1_b, w_router, b_router, w_pl_gate, b_pl_gate, w_pl_proj) = weights
    tp, ts = prompt[3].shape[0], sample[3].shape[0]
    tm = TOKEN_TILE
    assert tp % tm == 0 and ts % tm == 0
    npt, nst = tp // tm, ts // tm
    t_all = tp + ts
    avg = jnp.asarray(np.kron(np.eye(RET_HEADS), np.full((RET_V_DIM, RET_V_DIM), 1.0 / RET_V_DIM)), BF16)
    w_out_b = w_out.astype(BF16)
    wr = jnp.zeros((D_MODEL, LANES), F32).at[:, :N_EXPERTS].set(w_router)
    br = jnp.zeros((1, LANES), F32).at[0, :N_EXPERTS].set(b_router)
    row = lambda v: v.reshape(1, -1)
    consts = [avg, row(gn_w), row(gn_b), w_out_b[:RET_V], w_out_b[RET_V:], row(ln1_g), row(ln1_b),
              w_pl_gate.astype(BF16), row(b_pl_gate), w_pl_proj.astype(BF16), wr, br]
    p_spec = lambda a: pl.BlockSpec((tm, a.shape[1]), lambda i: (jnp.minimum(i, npt - 1), 0))
    s_spec = lambda a: pl.BlockSpec((tm, a.shape[1]), lambda i: (jnp.maximum(i - npt, 0), 0))
    c_spec = lambda a: pl.BlockSpec(a.shape, lambda i: (0, 0))
    return pl.pallas_call(
        functools.partial(_mixer_kernel, n_prompt_tiles=npt),
        grid=(npt + nst,),
        in_specs=[p_spec(a) for a in prompt] + [s_spec(a) for a in sample] + [c_spec(a) for a in consts],
        out_specs=[pl.BlockSpec((tm * ROW_TILES, LANES), lambda i: (i, 0)),
                   pl.BlockSpec((tm, D_MODEL), lambda i: (i, 0)),
                   pl.BlockSpec((tm, LANES), lambda i: (i, 0)),
                   pl.BlockSpec((tm, LANES), lambda i: (i, 0))],
        out_shape=[jax.ShapeDtypeStruct((t_all * ROW_TILES, LANES), F32),
                   jax.ShapeDtypeStruct((t_all, D_MODEL), F32),
                   jax.ShapeDtypeStruct((t_all, LANES), jnp.int32),
                   jax.ShapeDtypeStruct((t_all, LANES), F32)],
        compiler_params=_cparams(("arbitrary",)),
        name="mixer_out",
    )(*prompt, *sample, *consts)


def _deinterleave_kernel(w_ref, g_ref, l_ref, t_ref):
    t_ref[...] = w_ref[0].T
    half = t_ref.shape[0] // 2
    g_ref[0] = t_ref[pl.ds(0, half, stride=2), :].T.astype(BF16)
    l_ref[0] = t_ref[pl.ds(1, half, stride=2), :].T.astype(BF16)


def _deinterleave_w1(w_e1):
    e, d, f2 = w_e1.shape
    spec = pl.BlockSpec((1, LANES, f2 // 2), lambda i, c: (i, c, 0))
    return pl.pallas_call(
        _deinterleave_kernel,
        grid=(e, d // LANES),
        in_specs=[pl.BlockSpec((1, LANES, f2), lambda i, c: (i, c, 0))],
        out_specs=[spec, spec],
        out_shape=[jax.ShapeDtypeStruct((e, d, f2 // 2), BF16)] * 2,
        scratch_shapes=[pltpu.VMEM((f2, LANES), F32)],
        compiler_params=_cparams(("arbitrary", "arbitrary")),
        name="deinterleave_w1",
    )(w_e1)


def _routing_plan(top_e, n_tokens):
    a = n_tokens * TOP_K
    mb, tc = MOE_BLOCK, COMBINE_TILE
    flat_e = top_e.reshape(-1)
    order = jnp.argsort(flat_e).astype(jnp.int32)
    experts = jnp.arange(N_EXPERTS, dtype=jnp.int32)
    counts = jnp.sum((flat_e[:, None] == experts[None, :]).astype(jnp.int32), axis=0)
    padded = (counts + mb - 1) // mb * mb
    start = jnp.cumsum(counts) - counts
    pend = jnp.cumsum(padded)
    pstart = pend - padded
    n_blocks = -(-(a + N_EXPERTS * (mb - 1)) // mb)
    block_first = jnp.arange(n_blocks, dtype=jnp.int32) * mb
    block_e = jnp.minimum(jnp.sum((pend[None, :] <= block_first[:, None]).astype(jnp.int32), axis=1),
                          N_EXPERTS - 1).astype(jnp.int32)
    n_used = (pend[-1] // mb).astype(jnp.int32).reshape(1)
    e_p = jnp.repeat(block_e, mb)
    off = jnp.arange(n_blocks * mb, dtype=jnp.int32) - pstart[e_p]
    valid = off < counts[e_p]
    asg = order[jnp.clip(start[e_p] + off, 0, a - 1)]
    t, k = asg // TOP_K, asg % TOP_K
    row_real = ((t // tc) * TOP_K + k) * tc + t % tc
    row_pad = a + (pstart[e_p] - start[e_p]) + (off - counts[e_p])
    tok = jnp.where(valid, t, 0)
    row = jnp.where(valid, row_real, row_pad)
    idx = jnp.stack([tok.reshape(n_blocks, mb), row.reshape(n_blocks, mb)], axis=1)
    extra = jnp.stack([jnp.zeros((mb,), jnp.int32), n_blocks * mb + jnp.arange(mb, dtype=jnp.int32)])[None]
    return jnp.concatenate([idx, extra], axis=0), block_e, n_used


IDX_RING = 4


def _row_tile(r):
    start = r * ROW_TILES
    return pl.ds(start if isinstance(r, int) else pl.multiple_of(start, ROW_TILES), ROW_TILES)


def _expert_kernel(be_ref, nused_ref, idx_hbm, x_hbm, w1g_ref, w1l_ref, b1g_ref, b1l_ref, w2_ref, b2_ref,
                   y_hbm, xbuf, ybuf, idx_smem, gsem, ssem, isem, *, n_blocks):
    i = pl.program_id(0)
    n_used = nused_ref[0]
    mb = MOE_BLOCK
    rows = mb * ROW_TILES
    cur, other = i % 2, (i + 1) % 2

    def ring(block):
        return (block + IDX_RING) % IDX_RING

    def idx_copy(block):
        src = jnp.where(block < 0, n_blocks, jnp.minimum(block, n_blocks))
        return pltpu.make_async_copy(idx_hbm.at[src], idx_smem.at[ring(block)], isem.at[ring(block)])

    def gather_copy(r, tok, s):
        return pltpu.make_async_copy(x_hbm.at[_row_tile(tok), :], xbuf.at[s, _row_tile(r), :], gsem.at[s])

    def scatter_copy(r, row, s):
        return pltpu.make_async_copy(ybuf.at[s, _row_tile(r), :], y_hbm.at[_row_tile(row), :], ssem.at[s])

    def gathered(s):
        return pltpu.make_async_copy(x_hbm.at[pl.ds(0, rows), :], xbuf.at[s], gsem.at[s])

    def scattered(s):
        return pltpu.make_async_copy(ybuf.at[s], y_hbm.at[pl.ds(0, rows), :], ssem.at[s])

    def issue_gather(block, s):
        for r in range(mb):
            gather_copy(r, idx_smem[ring(block), 0, r], s).start(priority=r % 2)

    def issue_scatter(block, s):
        for r in range(mb):
            scatter_copy(r, idx_smem[ring(block), 1, r], s).start(priority=r % 2)

    def issue_rolled(copy, block, column, s):
        def body(r, carry):
            copy(r, idx_smem[ring(block), column, r], s).start()
            return carry
        lax.fori_loop(0, mb, body, 0)

    @pl.when(i == 0)
    def _():
        idx_copy(i).start()
        idx_copy(i - 1).start()
        idx_copy(i + 1).start()
        idx_copy(i).wait()
        idx_copy(i - 1).wait()
        issue_rolled(gather_copy, i, 0, cur)
        ybuf[other] = jnp.zeros((rows, LANES), F32)

    @pl.when(i < n_used)
    def _():
        idx_copy(i + 1).wait()
        idx_copy(i + 2).start()
        gathered(cur).wait()

        @pl.when(i >= 1)
        def _():
            scattered(cur).wait()

        issue_gather(i + 1, other)
        issue_scatter(i - 1, other)
        x = jnp.concatenate([xbuf[cur, pl.ds(s, mb, stride=ROW_TILES), :] for s in range(ROW_TILES)],
                            axis=1).astype(BF16)
        glu = jnp.minimum(_dot(x, w1g_ref[0]) + b1g_ref[0], SWIGLU_LIMIT)
        lin = jnp.clip(_dot(x, w1l_ref[0]) + b1l_ref[0], -SWIGLU_LIMIT, SWIGLU_LIMIT)
        act = glu * jax.nn.sigmoid(SWIGLU_ALPHA * glu) * (lin + 1.0)
        y = _dot(act.astype(BF16), w2_ref[0]) + b2_ref[0]
        for s in range(ROW_TILES):
            ybuf[cur, pl.ds(s, mb, stride=ROW_TILES), :] = y[:, s * LANES:(s + 1) * LANES]

    @pl.when(i == n_used - 1)
    def _():
        issue_rolled(scatter_copy, i, 1, cur)
        scattered(other).wait()
        scattered(cur).wait()
        gathered(other).wait()
        idx_copy(i + 2).wait()

    @pl.when(i >= n_used)
    def _():
        ybuf[cur] = jnp.zeros((rows, LANES), F32)
        fill = pltpu.make_async_copy(ybuf.at[cur], y_hbm.at[pl.ds(pl.multiple_of(i * rows, rows), rows), :],
                                     ssem.at[cur])
        fill.start()
        fill.wait()


def _experts(x1r, idx, block_e, n_used, w1g, w1l, b_e1, w_e2, b_e2):
    n_blocks = idx.shape[0] - 1
    mb = MOE_BLOCK
    rows = mb * ROW_TILES
    b1g, b1l = b_e1[:, None, 0::2], b_e1[:, None, 1::2]
    w2 = w_e2.astype(BF16)
    b2 = b_e2[:, None, :]
    wspec = pl.BlockSpec((1, D_MODEL, D_EXPERT), lambda i, be, nu: (be[i], 0, 0))
    bspec = pl.BlockSpec((1, 1, D_EXPERT), lambda i, be, nu: (be[i], 0, 0))
    return pl.pallas_call(
        functools.partial(_expert_kernel, n_blocks=n_blocks),
        grid_spec=pltpu.PrefetchScalarGridSpec(
            num_scalar_prefetch=2,
            grid=(n_blocks,),
            in_specs=[pl.BlockSpec(memory_space=pl.ANY),
                      pl.BlockSpec(memory_space=pl.ANY),
                      wspec, wspec, bspec, bspec,
                      pl.BlockSpec((1, D_EXPERT, D_MODEL), lambda i, be, nu: (be[i], 0, 0)),
                      pl.BlockSpec((1, 1, D_MODEL), lambda i, be, nu: (be[i], 0, 0))],
            out_specs=pl.BlockSpec(memory_space=pl.ANY),
            scratch_shapes=[pltpu.VMEM((2, rows, LANES), F32),
                            pltpu.VMEM((2, rows, LANES), F32),
                            pltpu.SMEM((IDX_RING, 2, mb), jnp.int32),
                            pltpu.SemaphoreType.DMA((2,)),
                            pltpu.SemaphoreType.DMA((2,)),
                            pltpu.SemaphoreType.DMA((IDX_RING,))]),
        out_shape=jax.ShapeDtypeStruct(((n_blocks + 1) * rows, LANES), F32),
        compiler_params=_cparams(("arbitrary",)),
        name="experts",
    )(block_e, n_used, idx, x1r, w1g, w1l, b1g, b1l, w2, b2)


def _combine_kernel(y_ref, gate_ref, resid_ref, g_ref, b_ref, op_ref, os_ref, *, n_prompt_tiles):
    tm = COMBINE_TILE
    z = resid_ref[...]
    gate = gate_ref[...]
    for kk in range(TOP_K):
        y_k = jnp.concatenate(
            [y_ref[pl.ds(kk * tm * ROW_TILES + s, tm, stride=ROW_TILES), :] for s in range(ROW_TILES)], axis=1)
        z = z + y_k * gate[:, kk:kk + 1]
    out = _layer_norm(z, g_ref[...], b_ref[...])

    @pl.when(pl.program_id(0) < n_prompt_tiles)
    def _():
        op_ref[...] = out

    @pl.when(pl.program_id(0) >= n_prompt_tiles)
    def _():
        os_ref[...] = out


def _combine(ys, top_g, resid, ln2_g, ln2_b, n_prompt):
    t_all = resid.shape[0]
    tm = COMBINE_TILE
    assert n_prompt % tm == 0 and (t_all - n_prompt) % tm == 0
    npt, n_tiles = n_prompt // tm, t_all // tm
    return pl.pallas_call(
        functools.partial(_combine_kernel, n_prompt_tiles=npt),
        grid=(n_tiles,),
        in_specs=[pl.BlockSpec((tm * TOP_K * ROW_TILES, LANES), lambda i: (i, 0)),
                  pl.BlockSpec((tm, LANES), lambda i: (i, 0)),
                  pl.BlockSpec((tm, D_MODEL), lambda i: (i, 0)),
                  pl.BlockSpec((1, D_MODEL), lambda i: (0, 0)),
                  pl.BlockSpec((1, D_MODEL), lambda i: (0, 0))],
        out_specs=[pl.BlockSpec((tm, D_MODEL), lambda i: (jnp.minimum(i, npt - 1), 0)),
                   pl.BlockSpec((tm, D_MODEL), lambda i: (jnp.maximum(i - npt, 0), 0))],
        out_shape=[jax.ShapeDtypeStruct((n_prompt, D_MODEL), F32),
                   jax.ShapeDtypeStruct((t_all - n_prompt, D_MODEL), F32)],
        compiler_params=_cparams(("arbitrary",)),
        name="combine_ln2",
    )(ys, top_g, resid, ln2_g.reshape(1, -1), ln2_b.reshape(1, -1))


def _pad_rows(x, rows):
    return jnp.pad(x, ((0, 0), (0, rows - x.shape[1]), (0, 0)))


def kernel(x_prompt, x_sample, cache_att_k, cache_att_v, state_ret, p_prompt, p_sample, w_in, ret_gn_w, ret_gn_b, w_out, ln1_g, ln1_b, w_router, b_router, w_e1, b_e1, w_e2, b_e2, w_pl_gate, b_pl_gate, w_pl_proj, ln2_g, ln2_b):
    assert w_in.shape[0] == DEPTH == 1
    B, S, _ = x_prompt.shape
    DB, T, _ = x_sample.shape
    l = 0
    w_in_b = w_in[l].astype(BF16)

    pos_p = jnp.arange(S, dtype=F32)
    qr, kr, vr, gr, qa, ka, va, ka4, va4 = _in_projection(x_prompt, w_in_b, pos_p)
    zero_state = jnp.zeros((B, 2, RET_QK // 2, RET_V // 2), F32)
    o_ret_p, st_p = _retention(qr, kr, vr, zero_state, RET_CHUNK, RET_CHUNK)
    o_att_p = _attention_prompt(qa, ka, va)
    keep = min(MAX_WINDOW, S)
    new_k_p = ka4[None, :, S - keep:]
    new_v_p = va4[None, :, S - keep:]
    new_st_p = _blockdiag_to_state(st_p)[None]

    ts = DB * T
    pos_s = jnp.tile(PAST_LEN + jnp.arange(T, dtype=F32), DB)
    sqr, skr, svr, sgr, sqa, _, _, ska4, sva4 = _in_projection(x_sample.reshape(1, ts, D_MODEL), w_in_b, pos_s)
    c_pad = 16
    per_b = lambda a: _pad_rows(a.reshape(DB, T, -1), c_pad)
    o_ret_s, st_s = _retention(per_b(sqr), per_b(skr), per_b(svr), _state_to_blockdiag(state_ret[l]), T, c_pad)
    o_ret_s = o_ret_s[:, :T].reshape(ts, RET_V)
    heads = lambda a: a.reshape(DB, T, ATT_HEADS, ATT_HEAD_DIM)
    o_att_s = _attention_sample(heads(sqa), heads(ska4), heads(sva4), cache_att_k[l], cache_att_v[l])
    new_k_s = heads(ska4)[None]
    new_v_s = heads(sva4)[None]
    new_st_s = _blockdiag_to_state(st_s).astype(state_ret.dtype)[None]

    tp = B * S
    flat2 = lambda a: a.reshape(-1, a.shape[-1])
    prompt = (flat2(o_ret_p), flat2(gr), flat2(o_att_p), flat2(x_prompt), flat2(p_prompt[l]))
    sample = (o_ret_s, flat2(sgr), o_att_s.reshape(ts, ATT_W), flat2(x_sample), flat2(p_sample[l]))
    x1r, resid, tope, topg = _mixer(prompt, sample, (ret_gn_w[l], ret_gn_b[l], w_out[l], ln1_g[l], ln1_b[l],
                                                     w_router[l], b_router[l], w_pl_gate[l], b_pl_gate[l],
                                                     w_pl_proj[l]))

    t_all = tp + ts
    idx, block_e, n_used = _routing_plan(tope[:, :TOP_K], t_all)
    w1g, w1l = _deinterleave_w1(w_e1[l])
    ys = _experts(x1r, idx, block_e, n_used, w1g, w1l, b_e1[l], w_e2[l], b_e2[l])
    y_p, y_s = _combine(ys, topg, resid, ln2_g[l], ln2_b[l], tp)
    return (y_p.reshape(B, S, D_MODEL), y_s.reshape(DB, T, D_MODEL),
            new_k_p, new_v_p, new_st_p, new_k_s, new_v_s, new_st_s)
```

```python
import functools
import math

import numpy as np
import jax
import jax.numpy as jnp
from jax import lax
from jax.experimental import pallas as pl
from jax.experimental.pallas import tpu as pltpu

F32 = jnp.float32
BF16 = jnp.bfloat16

D_MODEL = 1024
DEPTH = 1
PAST_LEN = 8192
RET_HEADS = 8
RET_QK_DIM = 32
RET_V_DIM = 64
RET_CHUNK = 128
RET_ROPE_BASE = 10000.0
ATT_HEADS = 8
ATT_HEAD_DIM = 64
ROT_DIMS = ATT_HEAD_DIM // 4
ROPE_THETA = 500000.0
DILATED_PATTERNS = ((128, 1), (512, 4), (2048, 16))
MAX_WINDOW = 2048
RET_QK = RET_HEADS * RET_QK_DIM
RET_V = RET_HEADS * RET_V_DIM
ATT_W = ATT_HEADS * ATT_HEAD_DIM
IN_COLS = RET_QK * 2 + RET_V * 2 + ATT_W * 3
N_EXPERTS = 32
TOP_K = 4
D_EXPERT = D_MODEL
SWIGLU_ALPHA = 1.702
SWIGLU_LIMIT = 7.0
PLE_DIM = 256
LN_EPS = 1e-5
GN_EPS = 1e-6
DEEPNORM_ALPHA = (2 * DEPTH) ** 0.25
NEG_INF = -1e30

LANES = 128
SUBLANES = 8
ROW_TILES = D_MODEL // LANES
VMEM_LIMIT = 56 * 1024 * 1024

TOKEN_TILE = 512
MOE_BLOCK = 256
COMBINE_TILE = 256
ATT_BLOCK = 128


def _dot(a, b):
    return jnp.dot(a, b, preferred_element_type=F32)


def _dot_nt(a, b):
    return lax.dot_general(a, b, (((1,), (1,)), ((), ())), preferred_element_type=F32)


def _dot_tn(a, b):
    return lax.dot_general(a, b, (((0,), (0,)), ((), ())), preferred_element_type=F32)


def _cparams(sem):
    return pltpu.CompilerParams(dimension_semantics=sem, vmem_limit_bytes=VMEM_LIMIT)


def _rotary_tables(pos, inv_freq, head_dim, n_heads):
    half = inv_freq.shape[0]
    ang = pos.astype(F32)[:, None] * inv_freq[None, :]
    cos, sin = jnp.cos(ang), jnp.sin(ang)
    rest = head_dim - 2 * half
    n = pos.shape[0]
    cos_h = jnp.concatenate([cos, cos, jnp.ones((n, rest), F32)], axis=1)
    sin_h = jnp.concatenate([-sin, sin, jnp.zeros((n, rest), F32)], axis=1)
    return jnp.tile(cos_h, (1, n_heads)), jnp.tile(sin_h, (1, n_heads))


def _ret_inv_freq():
    return 1.0 / (RET_ROPE_BASE ** jnp.linspace(0.0, 1.0, RET_QK_DIM // 2, dtype=F32))


def _att_inv_freq():
    return ROPE_THETA ** (-jnp.arange(0, ROT_DIMS, 2, dtype=F32) / ROT_DIMS)


def _ret_log_decay():
    return jnp.log(1.0 - 2.0 ** (-5.0 - jnp.arange(RET_HEADS, dtype=F32)))


def _retention_tables(c_true, c_pad):
    lg = _ret_log_decay()
    idx = jnp.arange(c_pad, dtype=F32)
    live = idx < c_true
    diff = idx[:, None] - idx[None, :]
    decay = jnp.where(diff[None] >= 0, jnp.exp(jnp.maximum(diff, 0.0)[None] * lg[:, None, None]), 0.0)
    decay = jnp.where(live[None, :, None] & live[None, None, :], decay, 0.0)
    cross = jnp.exp((idx + 1.0)[:, None] * lg[None, :])
    cross = jnp.where(live[:, None], cross, 0.0)
    kdec = jnp.exp((c_true - 1.0 - idx)[:, None] * lg[None, :])
    kdec = jnp.where(live[:, None], kdec, 0.0)
    sdec = jnp.exp(c_true * lg)
    hh = RET_HEADS // 2
    decay = decay.reshape(2, hh, c_pad, c_pad)
    cross = jnp.repeat(cross, RET_V_DIM, axis=1).reshape(c_pad, 2, hh * RET_V_DIM).transpose(1, 0, 2)
    kdec = jnp.repeat(kdec, RET_QK_DIM, axis=1).reshape(c_pad, 2, hh * RET_QK_DIM).transpose(1, 0, 2)
    sdec = jnp.broadcast_to(jnp.repeat(sdec, RET_V_DIM)[None, :], (hh * RET_QK_DIM, RET_V)) \
        .reshape(hh * RET_QK_DIM, 2, hh * RET_V_DIM).transpose(1, 0, 2)
    return decay, cross, kdec, sdec


def _rotate(h, cos, sin, half, period):
    outs = []
    for j in range(h.shape[1] // LANES):
        blk = h[:, j * LANES:(j + 1) * LANES]
        lane = lax.broadcasted_iota(jnp.int32, blk.shape, 1)
        first = (lane % period) < half
        partner = jnp.where(first, pltpu.roll(blk, LANES - half, 1), pltpu.roll(blk, half, 1))
        outs.append(blk * cos[:, j * LANES:(j + 1) * LANES] + partner * sin[:, j * LANES:(j + 1) * LANES])
    return jnp.concatenate(outs, axis=1)


PROJ_NAMES = ("q_r", "k_r", "v_r", "g_r", "q_a", "k_a", "v_a")
PROJ_WIDTHS = dict(zip(PROJ_NAMES, (RET_QK, RET_QK, RET_V, RET_V, ATT_W, ATT_W, ATT_W)))


def _inproj_kernel(x_ref, w_ref, cr_ref, sr_ref, ca_ref, sa_ref, *out_refs, direct, transposed):
    x = x_ref[0].astype(BF16)
    cr, sr, ca, sa = cr_ref[...], sr_ref[...], ca_ref[...], sa_ref[...]
    finish = {
        "q_r": lambda h: _rotate(h, cr, sr, RET_QK_DIM // 2, RET_QK_DIM),
        "k_r": lambda h: _rotate(h, cr, sr, RET_QK_DIM // 2, RET_QK_DIM) * (RET_QK_DIM ** -0.5),
        "q_a": lambda h: _rotate(h, ca, sa, ROT_DIMS // 2, ATT_HEAD_DIM) * (ATT_HEAD_DIM ** -0.5),
        "k_a": lambda h: _rotate(h, ca, sa, ROT_DIMS // 2, ATT_HEAD_DIM),
    }
    refs = dict(zip([("d", n) for n in direct] + [("t", n) for n in transposed], out_refs))
    c = 0
    for name in PROJ_NAMES:
        width = PROJ_WIDTHS[name]
        if name in direct or name in transposed:
            h = _dot(x, w_ref[:, c:c + width])
            h = finish.get(name, lambda v: v)(h)
            if name in direct:
                refs[("d", name)][0] = h
            if name in transposed:
                refs[("t", name)][0] = h.T
        c += width


def _in_projection(x, w_in_b, pos, direct, transposed, tm, tile_major):
    B, S, _ = x.shape
    assert S % tm == 0 and (B == 1 or not tile_major)
    cr, sr = _rotary_tables(pos, _ret_inv_freq(), RET_QK_DIM, RET_HEADS)
    ca, sa = _rotary_tables(pos, _att_inv_freq(), ATT_HEAD_DIM, ATT_HEADS)
    tab = lambda w: pl.BlockSpec((tm, w), lambda s, b: (s, 0))
    specs = [pl.BlockSpec((1, tm, PROJ_WIDTHS[n]), lambda s, b: (b, s, 0)) for n in direct]
    shapes = [jax.ShapeDtypeStruct((B, S, PROJ_WIDTHS[n]), F32) for n in direct]
    for n in transposed:
        w = PROJ_WIDTHS[n]
        if tile_major:
            specs.append(pl.BlockSpec((1, w, tm), lambda s, b: (s, 0, 0)))
            shapes.append(jax.ShapeDtypeStruct((S // tm, w, tm), F32))
        else:
            specs.append(pl.BlockSpec((1, w, tm), lambda s, b: (b, 0, s)))
            shapes.append(jax.ShapeDtypeStruct((B, w, S), F32))
    outs = pl.pallas_call(
        functools.partial(_inproj_kernel, direct=direct, transposed=transposed),
        grid=(S // tm, B),
        in_specs=[pl.BlockSpec((1, tm, D_MODEL), lambda s, b: (b, s, 0)),
                  pl.BlockSpec((D_MODEL, IN_COLS), lambda s, b: (0, 0)),
                  tab(RET_QK), tab(RET_QK), tab(ATT_W), tab(ATT_W)],
        out_specs=specs,
        out_shape=shapes,
        compiler_params=_cparams(("arbitrary", "arbitrary")),
        name="in_projection",
    )(x, w_in_b, cr, sr, ca, sa)
    return dict(zip(direct, outs[:len(direct)])), dict(zip(transposed, outs[len(direct):]))


def _retention_kernel(q_ref, k_ref, v_ref, s0_ref, dec_ref, cross_ref, kdec_ref, sdec_ref,
                      o_ref, so_ref, *, chunk, n_chunks):
    hh = RET_HEADS // 2
    qk_w, v_w = hh * RET_QK_DIM, hh * RET_V_DIM
    lane_q = lax.broadcasted_iota(jnp.int32, (chunk, qk_w), 1) // RET_QK_DIM
    lane_v = lax.broadcasted_iota(jnp.int32, (chunk, v_w), 1) // RET_V_DIM
    blockdiag = (lax.broadcasted_iota(jnp.int32, (qk_w, v_w), 0) // RET_QK_DIM
                 == lax.broadcasted_iota(jnp.int32, (qk_w, v_w), 1) // RET_V_DIM)
    state = s0_ref[0, 0]
    cross, kdec, sdec = cross_ref[0], kdec_ref[0], sdec_ref[0]
    for ci in range(n_chunks):
        rows = pl.ds(ci * chunk, chunk)
        q, k, v = q_ref[0, rows, :], k_ref[0, rows, :], v_ref[0, rows, :]
        kb, vb = k.astype(BF16), v.astype(BF16)
        o = _dot(q.astype(BF16), state.astype(BF16)) * cross
        for h in range(hh):
            qm = jnp.where(lane_q == h, q, 0.0).astype(BF16)
            inner = _dot_nt(qm, kb) * dec_ref[0, h]
            o = jnp.where(lane_v == h, o + _dot(inner.astype(BF16), vb), o)
        o_ref[0, rows, :] = o
        upd = _dot_tn((k * kdec).astype(BF16), vb)
        state = state * sdec + jnp.where(blockdiag, upd, 0.0)
    so_ref[0, 0] = state


def _retention(q, k, v, state_bd, c_true, c_pad):
    B, S, _ = q.shape
    hh = RET_HEADS // 2
    qk_w, v_w = hh * RET_QK_DIM, hh * RET_V_DIM
    decay, cross, kdec, sdec = _retention_tables(c_true, c_pad)
    kern = functools.partial(_retention_kernel, chunk=c_pad, n_chunks=S // c_pad)
    return pl.pallas_call(
        kern,
        grid=(B, 2),
        in_specs=[pl.BlockSpec((1, S, qk_w), lambda b, g: (b, 0, g)),
                  pl.BlockSpec((1, S, qk_w), lambda b, g: (b, 0, g)),
                  pl.BlockSpec((1, S, v_w), lambda b, g: (b, 0, g)),
                  pl.BlockSpec((1, 1, qk_w, v_w), lambda b, g: (b, g, 0, 0)),
                  pl.BlockSpec((1, hh, c_pad, c_pad), lambda b, g: (g, 0, 0, 0)),
                  pl.BlockSpec((1, c_pad, v_w), lambda b, g: (g, 0, 0)),
                  pl.BlockSpec((1, c_pad, qk_w), lambda b, g: (g, 0, 0)),
                  pl.BlockSpec((1, qk_w, v_w), lambda b, g: (g, 0, 0))],
        out_specs=[pl.BlockSpec((1, S, v_w), lambda b, g: (b, 0, g)),
                   pl.BlockSpec((1, 1, qk_w, v_w), lambda b, g: (b, g, 0, 0))],
        out_shape=[jax.ShapeDtypeStruct((B, S, RET_V), F32),
                   jax.ShapeDtypeStruct((B, 2, qk_w, v_w), F32)],
        compiler_params=_cparams(("arbitrary", "arbitrary")),
        name="retention",
    )(q, k, v, state_bd, decay, cross, kdec, sdec)


def _blockdiag_to_state(bd):
    B = bd.shape[0]
    hh = RET_HEADS // 2
    s = bd.reshape(B, 2, hh, RET_QK_DIM, hh, RET_V_DIM)
    s = jnp.stack([s[:, :, h, :, h, :] for h in range(hh)], axis=2)
    return s.reshape(B, RET_HEADS, RET_QK_DIM, RET_V_DIM)


def _retention_sample_kernel(gam_ref, q_ref, k_ref, v_ref, s_ref, o_ref, so_ref, *, steps):
    h = pl.program_id(0)
    gam = [gam_ref[h, p] for p in range(steps + 1)]
    q = [q_ref[t] for t in range(steps)]
    k = [k_ref[t] for t in range(steps)]
    v = [v_ref[t] for t in range(steps)]
    o = [jnp.zeros(v[0].shape, F32) for _ in range(steps)]
    for d in range(RET_QK_DIM):
        s_d = s_ref[0, d]
        new = s_d * gam[steps]
        for t in range(steps):
            o[t] = o[t] + q[t][d:d + 1, :] * s_d
            new = new + (k[t][d:d + 1, :] * gam[steps - 1 - t]) * v[t]
        so_ref[0, d] = new
    for i in range(steps):
        o[i] = o[i] * gam[i + 1]
        for j in range(i + 1):
            qk = jnp.sum(q[i] * k[j], axis=0, keepdims=True) * gam[i - j]
            o[i] = o[i] + qk * v[j]
        o_ref[i] = o[i]


def _retention_sample(q_t, k_t, v_t, state):
    steps, _, nb = q_t.shape
    gam = jnp.exp(_ret_log_decay()[:, None] * jnp.arange(steps + 1, dtype=F32)[None, :])
    qk_spec = pl.BlockSpec((steps, RET_QK_DIM, nb), lambda h: (0, h, 0))
    v_spec = pl.BlockSpec((steps, RET_V_DIM, nb), lambda h: (0, h, 0))
    s_spec = pl.BlockSpec((1, RET_QK_DIM, RET_V_DIM, nb), lambda h: (h, 0, 0, 0))
    return pl.pallas_call(
        functools.partial(_retention_sample_kernel, steps=steps),
        grid=(RET_HEADS,),
        in_specs=[pl.BlockSpec(memory_space=pltpu.SMEM), qk_spec, qk_spec, v_spec, s_spec],
        out_specs=[v_spec, s_spec],
        out_shape=[jax.ShapeDtypeStruct(v_t.shape, F32), jax.ShapeDtypeStruct(state.shape, F32)],
        compiler_params=_cparams(("arbitrary",)),
        name="retention_sample",
    )(gam, q_t, k_t, v_t, state)


def _attn_prompt_kernel(q_ref, k_ref, v_ref, o_ref, acc_ref, m_ref, l_ref, *, seq):
    n = ATT_BLOCK
    head0 = lax.broadcasted_iota(jnp.int32, (n, LANES), 1) < ATT_HEAD_DIM
    head0_kv = lax.broadcasted_iota(jnp.int32, (2 * n, LANES), 1) < ATT_HEAD_DIM
    a = lax.broadcasted_iota(jnp.int32, (n, 2 * n), 0)
    j = lax.broadcasted_iota(jnp.int32, (n, 2 * n), 1)
    bias_pc = jnp.where(((j < n) & (j >= a)) | ((j >= n) & (j - n <= a)), 0.0, NEG_INF)
    bias_c = bias_pc[:, n:]

    for p_idx, (window, d) in enumerate(DILATED_PATTERNS):
        n_blocks = (seq // d) // n
        for r in range(d):
            for c in range(n_blocks):
                def rows(block):
                    start = block * n * d + r
                    return pl.ds(start, n, stride=d) if d > 1 else pl.ds(start, n)
                q = q_ref[0, rows(c), :]
                if c > 0:
                    kk = jnp.concatenate([k_ref[0, rows(c - 1), :], k_ref[0, rows(c), :]], axis=0)
                    vv = jnp.concatenate([v_ref[0, rows(c - 1), :], v_ref[0, rows(c), :]], axis=0)
                    bias, hkv = bias_pc, head0_kv
                else:
                    kk, vv = k_ref[0, rows(c), :], v_ref[0, rows(c), :]
                    bias, hkv = bias_c, head0
                kb = kk.astype(BF16)
                pv, mx = [], []
                for h in range(2):
                    mine = head0 if h == 0 else jnp.logical_not(head0)
                    mine_kv = hkv if h == 0 else jnp.logical_not(hkv)
                    s = _dot_nt(jnp.where(mine, q, 0.0).astype(BF16), kb) + bias
                    m = jnp.max(s, axis=1, keepdims=True)
                    p = jnp.exp(s - m)
                    pv.append(_dot(p.astype(BF16), jnp.where(mine_kv, vv, 1.0).astype(BF16)))
                    mx.append(m)
                acc_u = jnp.where(head0, pv[0], pv[1])
                l_u = pltpu.roll(jnp.where(head0, pv[1], pv[0]), ATT_HEAD_DIM, 1)
                m_u = jnp.where(head0, mx[0], mx[1])
                rs = rows(c)
                if p_idx == 0:
                    acc_ref[rs, :] = acc_u
                    l_ref[rs, :] = l_u
                    m_ref[rs, :] = m_u
                else:
                    m_old = m_ref[rs, :]
                    m_new = jnp.maximum(m_old, m_u)
                    w_old, w_u = jnp.exp(m_old - m_new), jnp.exp(m_u - m_new)
                    acc_ref[rs, :] = acc_ref[rs, :] * w_old + acc_u * w_u
                    l_ref[rs, :] = l_ref[rs, :] * w_old + l_u * w_u
                    m_ref[rs, :] = m_new
    o_ref[0] = acc_ref[...] / l_ref[...]


def _attention_prompt(q, k, v):
    B, S, _ = q.shape
    spec = pl.BlockSpec((1, S, LANES), lambda b, g: (b, 0, g))
    return pl.pallas_call(
        functools.partial(_attn_prompt_kernel, seq=S),
        grid=(B, ATT_W // LANES),
        in_specs=[spec, spec, spec],
        out_specs=spec,
        out_shape=jax.ShapeDtypeStruct((B, S, ATT_W), F32),
        scratch_shapes=[pltpu.VMEM((S, LANES), F32)] * 3,
        compiler_params=_cparams(("arbitrary", "arbitrary")),
        name="attention_prompt",
    )(q, k, v)


def _attn_sample_kernel(q_ref, kn_ref, vn_ref, kc_ref, vc_ref, cc_ref, cn_ref, o_ref, *, steps):
    q = q_ref[0].astype(BF16)
    width = q.shape[1]
    kc = kc_ref[0].reshape(width, -1).astype(BF16)
    vc = vc_ref[0].reshape(width, -1).astype(BF16)
    cnt_c, cnt_n = cc_ref[...], cn_ref[...]
    s_c = jnp.where(cnt_c > 0.0, _dot(q, kc), NEG_INF)
    s_n = jnp.where(cnt_n > 0.0, _dot_nt(q, kn_ref[0].astype(BF16)), NEG_INF)
    m = jnp.maximum(s_c.max(axis=1, keepdims=True), s_n.max(axis=1, keepdims=True))
    p_c = cnt_c * jnp.exp(s_c - m)
    p_n = cnt_n * jnp.exp(s_n - m)
    den = p_c.sum(axis=1, keepdims=True) + p_n.sum(axis=1, keepdims=True)
    num = _dot_nt(p_c.astype(BF16), vc) + _dot(p_n.astype(BF16), vn_ref[0].astype(BF16))
    rows = lax.broadcasted_iota(jnp.int32, num.shape, 0)
    lanes = lax.broadcasted_iota(jnp.int32, num.shape, 1)
    out = jnp.where(rows % ATT_HEADS == lanes // ATT_HEAD_DIM, num / den, 0.0)
    for t in range(steps):
        o_ref[0, t:t + 1, :] = jnp.sum(out[t * ATT_HEADS:(t + 1) * ATT_HEADS], axis=0, keepdims=True)


def _sample_key_counts(dec_seq, buf, new_pad):
    t_q = np.repeat(np.arange(dec_seq), ATT_HEADS)[:, None]
    back = buf + t_q - np.arange(buf)[None, :]
    cnt_c = np.zeros(back.shape, np.float32)
    for w, d in DILATED_PATTERNS:
        cnt_c += ((back % d == 0) & (back >= d) & (back <= w)).astype(np.float32)
    back = t_q - np.arange(new_pad)[None, :]
    cnt_n = np.zeros(back.shape, np.float32)
    for w, d in DILATED_PATTERNS:
        cnt_n += ((back >= 0) & (back % d == 0) & (back <= w)).astype(np.float32)
    cnt_n *= (np.arange(new_pad) < dec_seq)[None, :]
    return jnp.asarray(cnt_c), jnp.asarray(cnt_n)


def _attention_sample(q, k_new, v_new, cache_k_t, cache_v_t):
    B, T, width = q.shape
    _, H, D, buf = cache_k_t.shape
    new_pad = SUBLANES
    cnt_c, cnt_n = _sample_key_counts(T, buf, new_pad)
    q_rows = (q.reshape(B, T, 1, H, D) * jnp.eye(H, dtype=F32)[None, None, :, :, None]).reshape(B, T * H, width)
    pad = lambda x: jnp.pad(x, ((0, 0), (0, new_pad - T), (0, 0)))
    new_spec = pl.BlockSpec((1, new_pad, width), lambda b: (b, 0, 0))
    cache_spec = pl.BlockSpec((1, H, D, buf), lambda b: (b, 0, 0, 0))
    const = lambda x: pl.BlockSpec(x.shape, lambda b: (0, 0))
    return pl.pallas_call(
        functools.partial(_attn_sample_kernel, steps=T),
        grid=(B,),
        in_specs=[pl.BlockSpec((1, T * H, width), lambda b: (b, 0, 0)), new_spec, new_spec,
                  cache_spec, cache_spec, const(cnt_c), const(cnt_n)],
        out_specs=pl.BlockSpec((1, T, width), lambda b: (b, 0, 0)),
        out_shape=jax.ShapeDtypeStruct((B, T, width), F32),
        compiler_params=_cparams(("arbitrary",)),
        name="attention_sample",
    )(q_rows, pad(k_new), pad(v_new), cache_k_t, cache_v_t, cnt_c, cnt_n)


def _layer_norm(y, g, b):
    mu = jnp.mean(y, axis=-1, keepdims=True)
    d = y - mu
    var = jnp.mean(d * d, axis=-1, keepdims=True)
    return d * lax.rsqrt(var + LN_EPS) * g + b


def _mixer_kernel(oret_p, g_p, oatt_p, x_p, pe_p, oret_s, g_s, oatt_s, x_s, pe_s,
                  avg_ref, gnw_ref, gnb_ref, wor_ref, woa_ref, ln1g_ref, ln1b_ref,
                  wg_ref, bg_ref, wp_ref, wr_ref, br_ref,
                  x1r_ref, resid_ref, tope_ref, topg_ref, *, n_prompt_tiles):
    is_prompt = pl.program_id(0) < n_prompt_tiles
    pick = lambda a, b: jnp.where(is_prompt, a[...], b[...])
    o_ret, g, o_att, x, pe = (pick(oret_p, oret_s), pick(g_p, g_s), pick(oatt_p, oatt_s),
                              pick(x_p, x_s), pick(pe_p, pe_s))
    avg = avg_ref[...]

    def head_mean(z):
        hi = z.astype(BF16)
        lo = (z - hi.astype(F32)).astype(BF16)
        return _dot(hi, avg) + _dot(lo, avg)

    d = o_ret - head_mean(o_ret)
    o_n = d * lax.rsqrt(head_mean(d * d) + GN_EPS) * gnw_ref[...] + gnb_ref[...]
    ret = g * jax.nn.sigmoid(g) * o_n
    mix = _dot(ret.astype(BF16), wor_ref[...]) + _dot(o_att.astype(BF16), woa_ref[...])
    x1 = _layer_norm(DEEPNORM_ALPHA * x + mix, ln1g_ref[...], ln1b_ref[...])
    x1b = x1.astype(BF16)
    ple = jax.nn.sigmoid(_dot(x1b, wg_ref[...]) + bg_ref[...]) * _dot(pe.astype(BF16), wp_ref[...])
    resid_ref[...] = DEEPNORM_ALPHA * x1 + ple
    tm = x1.shape[0]
    for s in range(ROW_TILES):
        x1r_ref[pl.ds(s, tm, stride=ROW_TILES), :] = x1[:, s * LANES:(s + 1) * LANES]

    logits = jnp.dot(x1, wr_ref[...], preferred_element_type=F32, precision=lax.Precision.HIGHEST) + br_ref[...]
    lane = lax.broadcasted_iota(jnp.int32, logits.shape, 1).astype(F32)
    work = jnp.where(lane < N_EXPERTS, logits, -jnp.inf)
    vals, idxs = [], []
    for _ in range(TOP_K):
        m = jnp.max(work, axis=1, keepdims=True)
        idx = jnp.min(jnp.where(work == m, lane, float(LANES)), axis=1, keepdims=True)
        vals.append(m)
        idxs.append(idx)
        work = jnp.where(lane == idx, -jnp.inf, work)
    exps = [jnp.exp(v - vals[0]) for v in vals]
    den = exps[0]
    for e in exps[1:]:
        den = den + e
    tope = jnp.zeros(logits.shape, F32)
    topg = jnp.zeros(logits.shape, F32)
    for kk in range(TOP_K):
        tope = jnp.where(lane == kk, idxs[kk], tope)
        topg = jnp.where(lane == kk, exps[kk] / den, topg)
    tope_ref[...] = tope.astype(jnp.int32)
    topg_ref[...] = topg


def _mixer(prompt, sample, weights):
    (gn_w, gn_b, w_out, ln1_g, ln1_b, w_router, b_router, w_pl_gate, b_pl_gate, w_pl_proj) = weights
    tp, ts = prompt[3].shape[0], sample[3].shape[0]
    tm = TOKEN_TILE
    assert tp % tm == 0 and ts % tm == 0
    npt, nst = tp // tm, ts // tm
    t_all = tp + ts
    avg = jnp.asarray(np.kron(np.eye(RET_HEADS), np.full((RET_V_DIM, RET_V_DIM), 1.0 / RET_V_DIM)), BF16)
    w_out_b = w_out.astype(BF16)
    wr = jnp.zeros((D_MODEL, LANES), F32).at[:, :N_EXPERTS].set(w_router)
    br = jnp.zeros((1, LANES), F32).at[0, :N_EXPERTS].set(b_router)
    row = lambda v: v.reshape(1, -1)
    consts = [avg, row(gn_w), row(gn_b), w_out_b[:RET_V], w_out_b[RET_V:], row(ln1_g), row(ln1_b),
              w_pl_gate.astype(BF16), row(b_pl_gate), w_pl_proj.astype(BF16), wr, br]
    p_spec = lambda a: pl.BlockSpec((tm, a.shape[1]), lambda i: (jnp.minimum(i, npt - 1), 0))
    s_spec = lambda a: pl.BlockSpec((tm, a.shape[1]), lambda i: (jnp.maximum(i - npt, 0), 0))
    c_spec = lambda a: pl.BlockSpec(a.shape, lambda i: (0, 0))
    return pl.pallas_call(
        functools.partial(_mixer_kernel, n_prompt_tiles=npt),
        grid=(npt + nst,),
        in_specs=[p_spec(a) for a in prompt] + [s_spec(a) for a in sample] + [c_spec(a) for a in consts],
        out_specs=[pl.BlockSpec((tm * ROW_TILES, LANES), lambda i: (i, 0)),
                   pl.BlockSpec((tm, D_MODEL), lambda i: (i, 0)),
                   pl.BlockSpec((tm, LANES), lambda i: (i, 0)),
                   pl.BlockSpec((tm, LANES), lambda i: (i, 0))],
        out_shape=[jax.ShapeDtypeStruct((t_all * ROW_TILES, LANES), F32),
                   jax.ShapeDtypeStruct((t_all, D_MODEL), F32),
                   jax.ShapeDtypeStruct((t_all, LANES), jnp.int32),
                   jax.ShapeDtypeStruct((t_all, LANES), F32)],
        compiler_params=_cparams(("arbitrary",)),
        name="mixer_out",
    )(*prompt, *sample, *consts)


def _deinterleave_kernel(w_ref, g_ref, l_ref, t_ref):
    t_ref[...] = w_ref[0].T
    half = t_ref.shape[0] // 2
    g_ref[0] = t_ref[pl.ds(0, half, stride=2), :].T.astype(BF16)
    l_ref[0] = t_ref[pl.ds(1, half, stride=2), :].T.astype(BF16)


def _deinterleave_w1(w_e1):
    e, d, f2 = w_e1.shape
    spec = pl.BlockSpec((1, LANES, f2 // 2), lambda i, c: (i, c, 0))
    return pl.pallas_call(
        _deinterleave_kernel,
        grid=(e, d // LANES),
        in_specs=[pl.BlockSpec((1, LANES, f2), lambda i, c: (i, c, 0))],
        out_specs=[spec, spec],
        out_shape=[jax.ShapeDtypeStruct((e, d, f2 // 2), BF16)] * 2,
        scratch_shapes=[pltpu.VMEM((f2, LANES), F32)],
        compiler_params=_cparams(("arbitrary", "arbitrary")),
        name="deinterleave_w1",
    )(w_e1)


def _routing_plan(top_e, n_tokens):
    a = n_tokens * TOP_K
    mb, tc = MOE_BLOCK, COMBINE_TILE
    flat_e = top_e.reshape(-1)
    order = jnp.argsort(flat_e).astype(jnp.int32)
    experts = jnp.arange(N_EXPERTS, dtype=jnp.int32)
    counts = jnp.sum((flat_e[:, None] == experts[None, :]).astype(jnp.int32), axis=0)
    padded = (counts + mb - 1) // mb * mb
    start = jnp.cumsum(counts) - counts
    pend = jnp.cumsum(padded)
    pstart = pend - padded
    n_blocks = -(-(a + N_EXPERTS * (mb - 1)) // mb)
    block_first = jnp.arange(n_blocks, dtype=jnp.int32) * mb
    block_e = jnp.minimum(jnp.sum((pend[None, :] <= block_first[:, None]).astype(jnp.int32), axis=1),
                          N_EXPERTS - 1).astype(jnp.int32)
    n_used = (pend[-1] // mb).astype(jnp.int32).reshape(1)
    e_p = jnp.repeat(block_e, mb)
    off = jnp.arange(n_blocks * mb, dtype=jnp.int32) - pstart[e_p]
    valid = off < counts[e_p]
    asg = order[jnp.clip(start[e_p] + off, 0, a - 1)]
    t, k = asg // TOP_K, asg % TOP_K
    row_real = ((t // tc) * TOP_K + k) * tc + t % tc
    row_pad = a + (pstart[e_p] - start[e_p]) + (off - counts[e_p])
    tok = jnp.where(valid, t, 0)
    row = jnp.where(valid, row_real, row_pad)
    idx = jnp.stack([tok.reshape(n_blocks, mb), row.reshape(n_blocks, mb)], axis=1)
    extra = jnp.stack([jnp.zeros((mb,), jnp.int32), n_blocks * mb + jnp.arange(mb, dtype=jnp.int32)])[None]
    return jnp.concatenate([idx, extra], axis=0), block_e, n_used


IDX_RING = 4


def _row_tile(r):
    start = r * ROW_TILES
    return pl.ds(start if isinstance(r, int) else pl.multiple_of(start, ROW_TILES), ROW_TILES)


def _expert_kernel(be_ref, nused_ref, idx_hbm, x_hbm, w1g_ref, w1l_ref, b1g_ref, b1l_ref, w2_ref, b2_ref,
                   y_hbm, xbuf, ybuf, idx_smem, gsem, ssem, isem, *, n_blocks):
    i = pl.program_id(0)
    n_used = nused_ref[0]
    mb = MOE_BLOCK
    rows = mb * ROW_TILES
    cur, other = i % 2, (i + 1) % 2

    def ring(block):
        return (block + IDX_RING) % IDX_RING

    def idx_copy(block):
        src = jnp.where(block < 0, n_blocks, jnp.minimum(block, n_blocks))
        return pltpu.make_async_copy(idx_hbm.at[src], idx_smem.at[ring(block)], isem.at[ring(block)])

    def gather_copy(r, tok, s):
        return pltpu.make_async_copy(x_hbm.at[_row_tile(tok), :], xbuf.at[s, _row_tile(r), :], gsem.at[s])

    def scatter_copy(r, row, s):
        return pltpu.make_async_copy(ybuf.at[s, _row_tile(r), :], y_hbm.at[_row_tile(row), :], ssem.at[s])

    def gathered(s):
        return pltpu.make_async_copy(x_hbm.at[pl.ds(0, rows), :], xbuf.at[s], gsem.at[s])

    def scattered(s):
        return pltpu.make_async_copy(ybuf.at[s], y_hbm.at[pl.ds(0, rows), :], ssem.at[s])

    def issue_gather(block, s):
        for r in range(mb):
            gather_copy(r, idx_smem[ring(block), 0, r], s).start(priority=r % 2)

    def issue_scatter(block, s):
        for r in range(mb):
            scatter_copy(r, idx_smem[ring(block), 1, r], s).start(priority=r % 2)

    def issue_rolled(copy, block, column, s):
        def body(r, carry):
            copy(r, idx_smem[ring(block), column, r], s).start()
            return carry
        lax.fori_loop(0, mb, body, 0)

    @pl.when(i == 0)
    def _():
        idx_copy(i).start()
        idx_copy(i - 1).start()
        idx_copy(i + 1).start()
        idx_copy(i).wait()
        idx_copy(i - 1).wait()
        issue_rolled(gather_copy, i, 0, cur)
        ybuf[other] = jnp.zeros((rows, LANES), F32)

    @pl.when(i < n_used)
    def _():
        idx_copy(i + 1).wait()
        idx_copy(i + 2).start()
        gathered(cur).wait()

        @pl.when(i >= 1)
        def _():
            scattered(cur).wait()

        issue_gather(i + 1, other)
        issue_scatter(i - 1, other)
        x = jnp.concatenate([xbuf[cur, pl.ds(s, mb, stride=ROW_TILES), :] for s in range(ROW_TILES)],
                            axis=1).astype(BF16)
        glu = jnp.minimum(_dot(x, w1g_ref[0]) + b1g_ref[0], SWIGLU_LIMIT)
        lin = jnp.clip(_dot(x, w1l_ref[0]) + b1l_ref[0], -SWIGLU_LIMIT, SWIGLU_LIMIT)
        act = glu * jax.nn.sigmoid(SWIGLU_ALPHA * glu) * (lin + 1.0)
        y = _dot(act.astype(BF16), w2_ref[0]) + b2_ref[0]
        for s in range(ROW_TILES):
            ybuf[cur, pl.ds(s, mb, stride=ROW_TILES), :] = y[:, s * LANES:(s + 1) * LANES]

    @pl.when(i == n_used - 1)
    def _():
        issue_rolled(scatter_copy, i, 1, cur)
        scattered(other).wait()
        scattered(cur).wait()
        gathered(other).wait()
        idx_copy(i + 2).wait()

    @pl.when(i >= n_used)
    def _():
        ybuf[cur] = jnp.zeros((rows, LANES), F32)
        fill = pltpu.make_async_copy(ybuf.at[cur], y_hbm.at[pl.ds(pl.multiple_of(i * rows, rows), rows), :],
                                     ssem.at[cur])
        fill.start()
        fill.wait()


def _experts(x1r, idx, block_e, n_used, w1g, w1l, b_e1, w_e2, b_e2):
    n_blocks = idx.shape[0] - 1
    mb = MOE_BLOCK
    rows = mb * ROW_TILES
    b1g, b1l = b_e1[:, None, 0::2], b_e1[:, None, 1::2]
    w2 = w_e2.astype(BF16)
    b2 = b_e2[:, None, :]
    wspec = pl.BlockSpec((1, D_MODEL, D_EXPERT), lambda i, be, nu: (be[i], 0, 0))
    bspec = pl.BlockSpec((1, 1, D_EXPERT), lambda i, be, nu: (be[i], 0, 0))
    return pl.pallas_call(
        functools.partial(_expert_kernel, n_blocks=n_blocks),
        grid_spec=pltpu.PrefetchScalarGridSpec(
            num_scalar_prefetch=2,
            grid=(n_blocks,),
            in_specs=[pl.BlockSpec(memory_space=pl.ANY),
                      pl.BlockSpec(memory_space=pl.ANY),
                      wspec, wspec, bspec, bspec,
                      pl.BlockSpec((1, D_EXPERT, D_MODEL), lambda i, be, nu: (be[i], 0, 0)),
                      pl.BlockSpec((1, 1, D_MODEL), lambda i, be, nu: (be[i], 0, 0))],
            out_specs=pl.BlockSpec(memory_space=pl.ANY),
            scratch_shapes=[pltpu.VMEM((2, rows, LANES), F32),
                            pltpu.VMEM((2, rows, LANES), F32),
                            pltpu.SMEM((IDX_RING, 2, mb), jnp.int32),
                            pltpu.SemaphoreType.DMA((2,)),
                            pltpu.SemaphoreType.DMA((2,)),
                            pltpu.SemaphoreType.DMA((IDX_RING,))]),
        out_shape=jax.ShapeDtypeStruct(((n_blocks + 1) * rows, LANES), F32),
        compiler_params=_cparams(("arbitrary",)),
        name="experts",
    )(block_e, n_used, idx, x1r, w1g, w1l, b1g, b1l, w2, b2)


def _combine_kernel(y_ref, gate_ref, resid_ref, g_ref, b_ref, op_ref, os_ref, *, n_prompt_tiles):
    tm = COMBINE_TILE
    z = resid_ref[...]
    gate = gate_ref[...]
    for kk in range(TOP_K):
        y_k = jnp.concatenate(
            [y_ref[pl.ds(kk * tm * ROW_TILES + s, tm, stride=ROW_TILES), :] for s in range(ROW_TILES)], axis=1)
        z = z + y_k * gate[:, kk:kk + 1]
    out = _layer_norm(z, g_ref[...], b_ref[...])

    @pl.when(pl.program_id(0) < n_prompt_tiles)
    def _():
        op_ref[...] = out

    @pl.when(pl.program_id(0) >= n_prompt_tiles)
    def _():
        os_ref[...] = out


def _combine(ys, top_g, resid, ln2_g, ln2_b, n_prompt):
    t_all = resid.shape[0]
    tm = COMBINE_TILE
    assert n_prompt % tm == 0 and (t_all - n_prompt) % tm == 0
    npt, n_tiles = n_prompt // tm, t_all // tm
    return pl.pallas_call(
        functools.partial(_combine_kernel, n_prompt_tiles=npt),
        grid=(n_tiles,),
        in_specs=[pl.BlockSpec((tm * TOP_K * ROW_TILES, LANES), lambda i: (i, 0)),
                  pl.BlockSpec((tm, LANES), lambda i: (i, 0)),
                  pl.BlockSpec((tm, D_MODEL), lambda i: (i, 0)),
                  pl.BlockSpec((1, D_MODEL), lambda i: (0, 0)),
                  pl.BlockSpec((1, D_MODEL), lambda i: (0, 0))],
        out_specs=[pl.BlockSpec((tm, D_MODEL), lambda i: (jnp.minimum(i, npt - 1), 0)),
                   pl.BlockSpec((tm, D_MODEL), lambda i: (jnp.maximum(i - npt, 0), 0))],
        out_shape=[jax.ShapeDtypeStruct((n_prompt, D_MODEL), F32),
                   jax.ShapeDtypeStruct((t_all - n_prompt, D_MODEL), F32)],
        compiler_params=_cparams(("arbitrary",)),
        name="combine_ln2",
    )(ys, top_g, resid, ln2_g.reshape(1, -1), ln2_b.reshape(1, -1))


def kernel(x_prompt, x_sample, cache_att_k, cache_att_v, state_ret, p_prompt, p_sample, w_in, ret_gn_w, ret_gn_b, w_out, ln1_g, ln1_b, w_router, b_router, w_e1, b_e1, w_e2, b_e2, w_pl_gate, b_pl_gate, w_pl_proj, ln2_g, ln2_b):
    assert w_in.shape[0] == DEPTH == 1
    B, S, _ = x_prompt.shape
    DB, T, _ = x_sample.shape
    H, D = ATT_HEADS, ATT_HEAD_DIM
    l = 0
    w_in_b = w_in[l].astype(BF16)

    pos_p = jnp.arange(S, dtype=F32)
    pd, pt = _in_projection(x_prompt, w_in_b, pos_p, PROJ_NAMES, ("k_a", "v_a"), TOKEN_TILE, False)
    zero_state = jnp.zeros((B, 2, RET_QK // 2, RET_V // 2), F32)
    o_ret_p, st_p = _retention(pd["q_r"], pd["k_r"], pd["v_r"], zero_state, RET_CHUNK, RET_CHUNK)
    o_att_p = _attention_prompt(pd["q_a"], pd["k_a"], pd["v_a"])
    keep = min(MAX_WINDOW, S)
    as_cache = lambda a: a.reshape(B, H, D, S).transpose(0, 3, 1, 2)[None, :, S - keep:]
    new_k_p, new_v_p = as_cache(pt["k_a"]), as_cache(pt["v_a"])
    new_st_p = _blockdiag_to_state(st_p)[None]

    ts = DB * T
    by_step = lambda a: a.transpose(1, 0, 2).reshape(ts, a.shape[-1])
    xs = by_step(x_sample)
    pos_s = jnp.repeat(PAST_LEN + jnp.arange(T, dtype=F32), DB)
    sd, st = _in_projection(xs[None], w_in_b, pos_s, ("g_r", "q_a", "k_a", "v_a"),
                            ("q_r", "k_r", "v_r", "k_a", "v_a"), DB, True)
    state_t = jnp.transpose(state_ret[l], (1, 2, 3, 0))
    o_ret_t, state_new = _retention_sample(st["q_r"], st["k_r"], st["v_r"], state_t)
    o_ret_s = o_ret_t.transpose(0, 2, 1).reshape(ts, RET_V)
    by_batch = lambda a: a[0].reshape(T, DB, -1).transpose(1, 0, 2)
    cache_t = lambda c: jnp.transpose(c[l], (0, 2, 3, 1))
    o_att_s = _attention_sample(by_batch(sd["q_a"]), by_batch(sd["k_a"]), by_batch(sd["v_a"]),
                                cache_t(cache_att_k), cache_t(cache_att_v))
    as_new = lambda a: a.reshape(T, H, D, DB).transpose(3, 0, 1, 2)[None]
    new_k_s, new_v_s = as_new(st["k_a"]), as_new(st["v_a"])
    new_st_s = jnp.transpose(state_new, (3, 0, 1, 2)).astype(state_ret.dtype)[None]

    tp = B * S
    flat2 = lambda a: a.reshape(-1, a.shape[-1])
    prompt = (flat2(o_ret_p), flat2(pd["g_r"]), flat2(o_att_p), flat2(x_prompt), flat2(p_prompt[l]))
    sample = (o_ret_s, sd["g_r"][0], by_step(o_att_s), xs, by_step(p_sample[l]))
    x1r, resid, tope, topg = _mixer(prompt, sample, (ret_gn_w[l], ret_gn_b[l], w_out[l], ln1_g[l], ln1_b[l],
                                                     w_router[l], b_router[l], w_pl_gate[l], b_pl_gate[l],
                                                     w_pl_proj[l]))

    t_all = tp + ts
    idx, block_e, n_used = _routing_plan(tope[:, :TOP_K], t_all)
    w1g, w1l = _deinterleave_w1(w_e1[l])
    ys = _experts(x1r, idx, block_e, n_used, w1g, w1l, b_e1[l], w_e2[l], b_e2[l])
    y_p, y_s = _combine(ys, topg, resid, ln2_g[l], ln2_b[l], tp)
    return (y_p.reshape(B, S, D_MODEL), y_s.reshape(T, DB, D_MODEL).transpose(1, 0, 2),
            new_k_p, new_v_p, new_st_p, new_k_s, new_v_s, new_st_s)
```

```python
import functools
import math

import numpy as np
import jax
import jax.numpy as jnp
from jax import lax
from jax.experimental import pallas as pl
from jax.experimental.pallas import tpu as pltpu

F32 = jnp.float32
BF16 = jnp.bfloat16

D_MODEL = 1024
DEPTH = 1
PAST_LEN = 8192
RET_HEADS = 8
RET_QK_DIM = 32
RET_V_DIM = 64
RET_CHUNK = 128
RET_ROPE_BASE = 10000.0
ATT_HEADS = 8
ATT_HEAD_DIM = 64
ROT_DIMS = ATT_HEAD_DIM // 4
ROPE_THETA = 500000.0
DILATED_PATTERNS = ((128, 1), (512, 4), (2048, 16))
MAX_WINDOW = 2048
RET_QK = RET_HEADS * RET_QK_DIM
RET_V = RET_HEADS * RET_V_DIM
ATT_W = ATT_HEADS * ATT_HEAD_DIM
IN_COLS = RET_QK * 2 + RET_V * 2 + ATT_W * 3
N_EXPERTS = 32
TOP_K = 4
D_EXPERT = D_MODEL
SWIGLU_ALPHA = 1.702
SWIGLU_LIMIT = 7.0
PLE_DIM = 256
LN_EPS = 1e-5
GN_EPS = 1e-6
DEEPNORM_ALPHA = (2 * DEPTH) ** 0.25
NEG_INF = -1e30

LANES = 128
SUBLANES = 8
ROW_TILES = D_MODEL // LANES
VMEM_LIMIT = 56 * 1024 * 1024

TOKEN_TILE = 512
MOE_BLOCK = 256
COMBINE_TILE = 256
ATT_BLOCK = 128
ATT_GROUP = 16


def _dot(a, b):
    return jnp.dot(a, b, preferred_element_type=F32)


def _dot_nt(a, b):
    return lax.dot_general(a, b, (((1,), (1,)), ((), ())), preferred_element_type=F32)


def _dot_tn(a, b):
    return lax.dot_general(a, b, (((0,), (0,)), ((), ())), preferred_element_type=F32)


def _cparams(sem):
    return pltpu.CompilerParams(dimension_semantics=sem, vmem_limit_bytes=VMEM_LIMIT)


def _rotary_tables(pos, inv_freq, head_dim, n_heads):
    half = inv_freq.shape[0]
    ang = pos.astype(F32)[:, None] * inv_freq[None, :]
    cos, sin = jnp.cos(ang), jnp.sin(ang)
    rest = head_dim - 2 * half
    n = pos.shape[0]
    cos_h = jnp.concatenate([cos, cos, jnp.ones((n, rest), F32)], axis=1)
    sin_h = jnp.concatenate([-sin, sin, jnp.zeros((n, rest), F32)], axis=1)
    return jnp.tile(cos_h, (1, n_heads)), jnp.tile(sin_h, (1, n_heads))


def _ret_inv_freq():
    return 1.0 / (RET_ROPE_BASE ** jnp.linspace(0.0, 1.0, RET_QK_DIM // 2, dtype=F32))


def _att_inv_freq():
    return ROPE_THETA ** (-jnp.arange(0, ROT_DIMS, 2, dtype=F32) / ROT_DIMS)


def _ret_log_decay():
    return jnp.log(1.0 - 2.0 ** (-5.0 - jnp.arange(RET_HEADS, dtype=F32)))


def _retention_tables(c_true, c_pad):
    lg = _ret_log_decay()
    idx = jnp.arange(c_pad, dtype=F32)
    live = idx < c_true
    diff = idx[:, None] - idx[None, :]
    decay = jnp.where(diff[None] >= 0, jnp.exp(jnp.maximum(diff, 0.0)[None] * lg[:, None, None]), 0.0)
    decay = jnp.where(live[None, :, None] & live[None, None, :], decay, 0.0)
    cross = jnp.exp((idx + 1.0)[:, None] * lg[None, :])
    cross = jnp.where(live[:, None], cross, 0.0)
    kdec = jnp.exp((c_true - 1.0 - idx)[:, None] * lg[None, :])
    kdec = jnp.where(live[:, None], kdec, 0.0)
    sdec = jnp.exp(c_true * lg)
    hh = RET_HEADS // 2
    decay = decay.reshape(2, hh, c_pad, c_pad)
    cross = jnp.repeat(cross, RET_V_DIM, axis=1).reshape(c_pad, 2, hh * RET_V_DIM).transpose(1, 0, 2)
    kdec = jnp.repeat(kdec, RET_QK_DIM, axis=1).reshape(c_pad, 2, hh * RET_QK_DIM).transpose(1, 0, 2)
    sdec = jnp.broadcast_to(jnp.repeat(sdec, RET_V_DIM)[None, :], (hh * RET_QK_DIM, RET_V)) \
        .reshape(hh * RET_QK_DIM, 2, hh * RET_V_DIM).transpose(1, 0, 2)
    return decay, cross, kdec, sdec


def _rotate(h, cos, sin, half, period):
    outs = []
    for j in range(h.shape[1] // LANES):
        blk = h[:, j * LANES:(j + 1) * LANES]
        lane = lax.broadcasted_iota(jnp.int32, blk.shape, 1)
        first = (lane % period) < half
        partner = jnp.where(first, pltpu.roll(blk, LANES - half, 1), pltpu.roll(blk, half, 1))
        outs.append(blk * cos[:, j * LANES:(j + 1) * LANES] + partner * sin[:, j * LANES:(j + 1) * LANES])
    return jnp.concatenate(outs, axis=1)


PROJ_NAMES = ("q_r", "k_r", "v_r", "g_r", "q_a", "k_a", "v_a")
PROJ_WIDTHS = dict(zip(PROJ_NAMES, (RET_QK, RET_QK, RET_V, RET_V, ATT_W, ATT_W, ATT_W)))


def _inproj_kernel(x_ref, w_ref, cr_ref, sr_ref, ca_ref, sa_ref, *out_refs, direct, transposed):
    x = x_ref[0].astype(BF16)
    cr, sr, ca, sa = cr_ref[...], sr_ref[...], ca_ref[...], sa_ref[...]
    finish = {
        "q_r": lambda h: _rotate(h, cr, sr, RET_QK_DIM // 2, RET_QK_DIM),
        "k_r": lambda h: _rotate(h, cr, sr, RET_QK_DIM // 2, RET_QK_DIM) * (RET_QK_DIM ** -0.5),
        "q_a": lambda h: _rotate(h, ca, sa, ROT_DIMS // 2, ATT_HEAD_DIM) * (ATT_HEAD_DIM ** -0.5),
        "k_a": lambda h: _rotate(h, ca, sa, ROT_DIMS // 2, ATT_HEAD_DIM),
    }
    refs = dict(zip([("d", n) for n in direct] + [("t", n) for n in transposed], out_refs))
    c = 0
    for name in PROJ_NAMES:
        width = PROJ_WIDTHS[name]
        if name in direct or name in transposed:
            h = _dot(x, w_ref[:, c:c + width])
            h = finish.get(name, lambda v: v)(h)
            if name in direct:
                refs[("d", name)][0] = h
            if name in transposed:
                refs[("t", name)][0] = h.T
        c += width


def _in_projection(x, w_in_b, pos, direct, transposed, tm, tile_major):
    B, S, _ = x.shape
    assert S % tm == 0 and (B == 1 or not tile_major)
    cr, sr = _rotary_tables(pos, _ret_inv_freq(), RET_QK_DIM, RET_HEADS)
    ca, sa = _rotary_tables(pos, _att_inv_freq(), ATT_HEAD_DIM, ATT_HEADS)
    tab = lambda w: pl.BlockSpec((tm, w), lambda s, b: (s, 0))
    specs = [pl.BlockSpec((1, tm, PROJ_WIDTHS[n]), lambda s, b: (b, s, 0)) for n in direct]
    shapes = [jax.ShapeDtypeStruct((B, S, PROJ_WIDTHS[n]), F32) for n in direct]
    for n in transposed:
        w = PROJ_WIDTHS[n]
        if tile_major:
            specs.append(pl.BlockSpec((1, w, tm), lambda s, b: (s, 0, 0)))
            shapes.append(jax.ShapeDtypeStruct((S // tm, w, tm), F32))
        else:
            specs.append(pl.BlockSpec((1, w, tm), lambda s, b: (b, 0, s)))
            shapes.append(jax.ShapeDtypeStruct((B, w, S), F32))
    outs = pl.pallas_call(
        functools.partial(_inproj_kernel, direct=direct, transposed=transposed),
        grid=(S // tm, B),
        in_specs=[pl.BlockSpec((1, tm, D_MODEL), lambda s, b: (b, s, 0)),
                  pl.BlockSpec((D_MODEL, IN_COLS), lambda s, b: (0, 0)),
                  tab(RET_QK), tab(RET_QK), tab(ATT_W), tab(ATT_W)],
        out_specs=specs,
        out_shape=shapes,
        compiler_params=_cparams(("arbitrary", "arbitrary")),
        name="in_projection",
    )(x, w_in_b, cr, sr, ca, sa)
    return dict(zip(direct, outs[:len(direct)])), dict(zip(transposed, outs[len(direct):]))


def _retention_kernel(q_ref, k_ref, v_ref, s0_ref, dec_ref, cross_ref, kdec_ref, sdec_ref,
                      o_ref, so_ref, *, chunk, n_chunks):
    hh = RET_HEADS // 2
    qk_w, v_w = hh * RET_QK_DIM, hh * RET_V_DIM
    lane_q = lax.broadcasted_iota(jnp.int32, (chunk, qk_w), 1) // RET_QK_DIM
    lane_v = lax.broadcasted_iota(jnp.int32, (chunk, v_w), 1) // RET_V_DIM
    blockdiag = (lax.broadcasted_iota(jnp.int32, (qk_w, v_w), 0) // RET_QK_DIM
                 == lax.broadcasted_iota(jnp.int32, (qk_w, v_w), 1) // RET_V_DIM)
    state = s0_ref[0, 0]
    cross, kdec, sdec = cross_ref[0], kdec_ref[0], sdec_ref[0]
    for ci in range(n_chunks):
        rows = pl.ds(ci * chunk, chunk)
        q, k, v = q_ref[0, rows, :], k_ref[0, rows, :], v_ref[0, rows, :]
        kb, vb = k.astype(BF16), v.astype(BF16)
        o = _dot(q.astype(BF16), state.astype(BF16)) * cross
        for h in range(hh):
            qm = jnp.where(lane_q == h, q, 0.0).astype(BF16)
            inner = _dot_nt(qm, kb) * dec_ref[0, h]
            o = jnp.where(lane_v == h, o + _dot(inner.astype(BF16), vb), o)
        o_ref[0, rows, :] = o
        upd = _dot_tn((k * kdec).astype(BF16), vb)
        state = state * sdec + jnp.where(blockdiag, upd, 0.0)
    so_ref[0, 0] = state


def _retention(q, k, v, state_bd, c_true, c_pad):
    B, S, _ = q.shape
    hh = RET_HEADS // 2
    qk_w, v_w = hh * RET_QK_DIM, hh * RET_V_DIM
    decay, cross, kdec, sdec = _retention_tables(c_true, c_pad)
    kern = functools.partial(_retention_kernel, chunk=c_pad, n_chunks=S // c_pad)
    return pl.pallas_call(
        kern,
        grid=(B, 2),
        in_specs=[pl.BlockSpec((1, S, qk_w), lambda b, g: (b, 0, g)),
                  pl.BlockSpec((1, S, qk_w), lambda b, g: (b, 0, g)),
                  pl.BlockSpec((1, S, v_w), lambda b, g: (b, 0, g)),
                  pl.BlockSpec((1, 1, qk_w, v_w), lambda b, g: (b, g, 0, 0)),
                  pl.BlockSpec((1, hh, c_pad, c_pad), lambda b, g: (g, 0, 0, 0)),
                  pl.BlockSpec((1, c_pad, v_w), lambda b, g: (g, 0, 0)),
                  pl.BlockSpec((1, c_pad, qk_w), lambda b, g: (g, 0, 0)),
                  pl.BlockSpec((1, qk_w, v_w), lambda b, g: (g, 0, 0))],
        out_specs=[pl.BlockSpec((1, S, v_w), lambda b, g: (b, 0, g)),
                   pl.BlockSpec((1, 1, qk_w, v_w), lambda b, g: (b, g, 0, 0))],
        out_shape=[jax.ShapeDtypeStruct((B, S, RET_V), F32),
                   jax.ShapeDtypeStruct((B, 2, qk_w, v_w), F32)],
        compiler_params=_cparams(("arbitrary", "arbitrary")),
        name="retention",
    )(q, k, v, state_bd, decay, cross, kdec, sdec)


def _blockdiag_to_state(bd):
    B = bd.shape[0]
    hh = RET_HEADS // 2
    s = bd.reshape(B, 2, hh, RET_QK_DIM, hh, RET_V_DIM)
    s = jnp.stack([s[:, :, h, :, h, :] for h in range(hh)], axis=2)
    return s.reshape(B, RET_HEADS, RET_QK_DIM, RET_V_DIM)


def _retention_sample_kernel(gam_ref, q_ref, k_ref, v_ref, s_ref, o_ref, so_ref, *, steps):
    h = pl.program_id(0)
    gam = [gam_ref[h, p] for p in range(steps + 1)]
    q = [q_ref[t] for t in range(steps)]
    k = [k_ref[t] for t in range(steps)]
    v = [v_ref[t] for t in range(steps)]
    o = [jnp.zeros(v[0].shape, F32) for _ in range(steps)]
    for d in range(RET_QK_DIM):
        s_d = s_ref[0, d]
        new = s_d * gam[steps]
        for t in range(steps):
            o[t] = o[t] + q[t][d:d + 1, :] * s_d
            new = new + (k[t][d:d + 1, :] * gam[steps - 1 - t]) * v[t]
        so_ref[0, d] = new
    for i in range(steps):
        o[i] = o[i] * gam[i + 1]
        for j in range(i + 1):
            qk = jnp.sum(q[i] * k[j], axis=0, keepdims=True) * gam[i - j]
            o[i] = o[i] + qk * v[j]
        o_ref[i] = o[i]


def _retention_sample(q_t, k_t, v_t, state):
    steps, _, nb = q_t.shape
    gam = jnp.exp(_ret_log_decay()[:, None] * jnp.arange(steps + 1, dtype=F32)[None, :])
    qk_spec = pl.BlockSpec((steps, RET_QK_DIM, nb), lambda h: (0, h, 0))
    v_spec = pl.BlockSpec((steps, RET_V_DIM, nb), lambda h: (0, h, 0))
    s_spec = pl.BlockSpec((1, RET_QK_DIM, RET_V_DIM, nb), lambda h: (h, 0, 0, 0))
    return pl.pallas_call(
        functools.partial(_retention_sample_kernel, steps=steps),
        grid=(RET_HEADS,),
        in_specs=[pl.BlockSpec(memory_space=pltpu.SMEM), qk_spec, qk_spec, v_spec, s_spec],
        out_specs=[v_spec, s_spec],
        out_shape=[jax.ShapeDtypeStruct(v_t.shape, F32), jax.ShapeDtypeStruct(state.shape, F32)],
        compiler_params=_cparams(("arbitrary",)),
        name="retention_sample",
    )(gam, q_t, k_t, v_t, state)


def _attn_prompt_kernel(q_ref, k_ref, v_ref, o_ref, o_ref_p, lse_ref, *, seq):
    n = ATT_BLOCK
    head0 = lax.broadcasted_iota(jnp.int32, (n, LANES), 1) < ATT_HEAD_DIM
    head0_kv = lax.broadcasted_iota(jnp.int32, (2 * n, LANES), 1) < ATT_HEAD_DIM
    a = lax.broadcasted_iota(jnp.int32, (n, 2 * n), 0)
    j = lax.broadcasted_iota(jnp.int32, (n, 2 * n), 1)
    bias_pc = jnp.where(((j < n) & (j >= a)) | ((j >= n) & (j - n <= a)), 0.0, NEG_INF)
    bias_c = bias_pc[:, n:]
    bias_first = jnp.where(j < n, NEG_INF, bias_pc)

    def rows(start, d):
        if d > 1:
            return pl.ds(start, n, stride=d)
        return pl.ds(start if isinstance(start, int) else pl.multiple_of(start, n), n)

    def blocks(p_idx, d, specs):
        loaded = []
        for cur, prev, has_prev in specs:
            q = q_ref[0, rows(cur, d), :]
            if has_prev is False:
                kk, vv = k_ref[0, rows(cur, d), :], v_ref[0, rows(cur, d), :]
                bias, hkv = bias_c, head0
            else:
                kk = jnp.concatenate([k_ref[0, rows(prev, d), :], k_ref[0, rows(cur, d), :]], axis=0)
                vv = jnp.concatenate([v_ref[0, rows(prev, d), :], v_ref[0, rows(cur, d), :]], axis=0)
                bias = bias_pc if has_prev is True else jnp.where(has_prev, bias_pc, bias_first)
                hkv = head0_kv
            loaded.append((q, kk.astype(BF16), vv, bias, hkv))
        scores = []
        for q, kb, vv, bias, hkv in loaded:
            for h in range(2):
                mine = head0 if h == 0 else jnp.logical_not(head0)
                scores.append(_dot_nt(jnp.where(mine, q, 0.0).astype(BF16), kb) + bias)
        probs, maxes = [], []
        for s in scores:
            m = jnp.max(s, axis=1, keepdims=True)
            probs.append(jnp.exp(s - m).astype(BF16))
            maxes.append(m)
        pvs = []
        for u, (q, kb, vv, bias, hkv) in enumerate(loaded):
            for h in range(2):
                mine_kv = hkv if h == 0 else jnp.logical_not(hkv)
                pvs.append(_dot(probs[2 * u + h], jnp.where(mine_kv, vv, 1.0).astype(BF16)))
        for u, (cur, prev, has_prev) in enumerate(specs):
            pv0, pv1 = pvs[2 * u], pvs[2 * u + 1]
            acc_u = jnp.where(head0, pv0, pv1)
            l_u = pltpu.roll(jnp.where(head0, pv1, pv0), ATT_HEAD_DIM, 1)
            m_u = jnp.where(head0, maxes[2 * u], maxes[2 * u + 1])
            o_ref_p[p_idx, rows(cur, d), :] = acc_u / l_u
            lse_ref[p_idx, rows(cur, d), :] = m_u + jnp.log(l_u)

    G = ATT_GROUP
    for p_idx, (window, d) in enumerate(DILATED_PATTERNS):
        nb = (seq // d) // n
        assert (d * nb) % G == 0 and (nb % G == 0 or G % nb == 0)

        def group(g, carry, p_idx=p_idx, d=d, nb=nb):
            specs = []
            for jj in range(G):
                if nb > G:
                    first = g * G
                    r = 0 if d == 1 else first // nb
                    c = first % nb + jj
                    specs.append((c * (n * d) + r, jnp.maximum(c - 1, 0) * (n * d) + r,
                                  True if jj > 0 else c > 0))
                else:
                    r = g * (G // nb) + jj // nb
                    c = jj % nb
                    specs.append((c * (n * d) + r, (c - 1) * (n * d) + r, c > 0))
            blocks(p_idx, d, specs)
            return carry

        lax.fori_loop(0, d * nb // G, group, 0)

    def merge(c, carry):
        rs = pl.ds(pl.multiple_of(c * n, n), n)
        lse = [lse_ref[p, rs, :] for p in range(len(DILATED_PATTERNS))]
        top = functools.reduce(jnp.maximum, lse)
        w = [jnp.exp(x - top) for x in lse]
        num = functools.reduce(lambda x, y: x + y, [w[p] * o_ref_p[p, rs, :] for p in range(len(w))])
        o_ref[0, rs, :] = num / functools.reduce(lambda x, y: x + y, w)
        return carry

    lax.fori_loop(0, seq // n, merge, 0)


def _attention_prompt(q, k, v):
    B, S, _ = q.shape
    spec = pl.BlockSpec((1, S, LANES), lambda b, g: (b, 0, g))
    return pl.pallas_call(
        functools.partial(_attn_prompt_kernel, seq=S),
        grid=(B, ATT_W // LANES),
        in_specs=[spec, spec, spec],
        out_specs=spec,
        out_shape=jax.ShapeDtypeStruct((B, S, ATT_W), F32),
        scratch_shapes=[pltpu.VMEM((len(DILATED_PATTERNS), S, LANES), F32)] * 2,
        compiler_params=_cparams(("arbitrary", "arbitrary")),
        name="attention_prompt",
    )(q, k, v)


def _attn_sample_kernel(q_ref, kn_ref, vn_ref, kc_ref, vc_ref, cc_ref, cn_ref, o_ref, *, steps):
    q = q_ref[0].astype(BF16)
    width = q.shape[1]
    kc = kc_ref[0].reshape(width, -1).astype(BF16)
    vc = vc_ref[0].reshape(width, -1).astype(BF16)
    cnt_c, cnt_n = cc_ref[...], cn_ref[...]
    s_c = jnp.where(cnt_c > 0.0, _dot(q, kc), NEG_INF)
    s_n = jnp.where(cnt_n > 0.0, _dot_nt(q, kn_ref[0].astype(BF16)), NEG_INF)
    m = jnp.maximum(s_c.max(axis=1, keepdims=True), s_n.max(axis=1, keepdims=True))
    p_c = cnt_c * jnp.exp(s_c - m)
    p_n = cnt_n * jnp.exp(s_n - m)
    den = p_c.sum(axis=1, keepdims=True) + p_n.sum(axis=1, keepdims=True)
    num = _dot_nt(p_c.astype(BF16), vc) + _dot(p_n.astype(BF16), vn_ref[0].astype(BF16))
    rows = lax.broadcasted_iota(jnp.int32, num.shape, 0)
    lanes = lax.broadcasted_iota(jnp.int32, num.shape, 1)
    out = jnp.where(rows % ATT_HEADS == lanes // ATT_HEAD_DIM, num / den, 0.0)
    for t in range(steps):
        o_ref[0, t:t + 1, :] = jnp.sum(out[t * ATT_HEADS:(t + 1) * ATT_HEADS], axis=0, keepdims=True)


def _sample_key_counts(dec_seq, buf, new_pad):
    t_q = np.repeat(np.arange(dec_seq), ATT_HEADS)[:, None]
    back = buf + t_q - np.arange(buf)[None, :]
    cnt_c = np.zeros(back.shape, np.float32)
    for w, d in DILATED_PATTERNS:
        cnt_c += ((back % d == 0) & (back >= d) & (back <= w)).astype(np.float32)
    back = t_q - np.arange(new_pad)[None, :]
    cnt_n = np.zeros(back.shape, np.float32)
    for w, d in DILATED_PATTERNS:
        cnt_n += ((back >= 0) & (back % d == 0) & (back <= w)).astype(np.float32)
    cnt_n *= (np.arange(new_pad) < dec_seq)[None, :]
    return jnp.asarray(cnt_c), jnp.asarray(cnt_n)


def _attention_sample(q, k_new, v_new, cache_k_t, cache_v_t):
    B, T, width = q.shape
    _, H, D, buf = cache_k_t.shape
    new_pad = SUBLANES
    cnt_c, cnt_n = _sample_key_counts(T, buf, new_pad)
    q_rows = (q.reshape(B, T, 1, H, D) * jnp.eye(H, dtype=F32)[None, None, :, :, None]).reshape(B, T * H, width)
    pad = lambda x: jnp.pad(x, ((0, 0), (0, new_pad - T), (0, 0)))
    new_spec = pl.BlockSpec((1, new_pad, width), lambda b: (b, 0, 0))
    cache_spec = pl.BlockSpec((1, H, D, buf), lambda b: (b, 0, 0, 0))
    const = lambda x: pl.BlockSpec(x.shape, lambda b: (0, 0))
    return pl.pallas_call(
        functools.partial(_attn_sample_kernel, steps=T),
        grid=(B,),
        in_specs=[pl.BlockSpec((1, T * H, width), lambda b: (b, 0, 0)), new_spec, new_spec,
                  cache_spec, cache_spec, const(cnt_c), const(cnt_n)],
        out_specs=pl.BlockSpec((1, T, width), lambda b: (b, 0, 0)),
        out_shape=jax.ShapeDtypeStruct((B, T, width), F32),
        compiler_params=_cparams(("arbitrary",)),
        name="attention_sample",
    )(q_rows, pad(k_new), pad(v_new), cache_k_t, cache_v_t, cnt_c, cnt_n)


def _layer_norm(y, g, b):
    mu = jnp.mean(y, axis=-1, keepdims=True)
    d = y - mu
    var = jnp.mean(d * d, axis=-1, keepdims=True)
    return d * lax.rsqrt(var + LN_EPS) * g + b


def _mixer_kernel(oret_p, g_p, oatt_p, x_p, pe_p, oret_s, g_s, oatt_s, x_s, pe_s,
                  avg_ref, gnw_ref, gnb_ref, wor_ref, woa_ref, ln1g_ref, ln1b_ref,
                  wg_ref, bg_ref, wp_ref, wr_ref, br_ref,
                  x1r_ref, resid_ref, tope_ref, topg_ref, *, n_prompt_tiles):
    def tile(oret_ref, g_ref, oatt_ref, x_ref, pe_ref):
        o_ret, g, o_att, x, pe = oret_ref[...], g_ref[...], oatt_ref[...], x_ref[...], pe_ref[...]
        avg = avg_ref[...]

        def head_mean(z):
            hi = z.astype(BF16)
            lo = (z - hi.astype(F32)).astype(BF16)
            return _dot(hi, avg) + _dot(lo, avg)

        d = o_ret - head_mean(o_ret)
        o_n = d * lax.rsqrt(head_mean(d * d) + GN_EPS) * gnw_ref[...] + gnb_ref[...]
        ret = g * jax.nn.sigmoid(g) * o_n
        mix = _dot(ret.astype(BF16), wor_ref[...]) + _dot(o_att.astype(BF16), woa_ref[...])
        x1 = _layer_norm(DEEPNORM_ALPHA * x + mix, ln1g_ref[...], ln1b_ref[...])
        x1b = x1.astype(BF16)
        ple = jax.nn.sigmoid(_dot(x1b, wg_ref[...]) + bg_ref[...]) * _dot(pe.astype(BF16), wp_ref[...])
        resid_ref[...] = DEEPNORM_ALPHA * x1 + ple
        tm = x1.shape[0]
        for s in range(ROW_TILES):
            x1r_ref[pl.ds(s, tm, stride=ROW_TILES), :] = x1[:, s * LANES:(s + 1) * LANES]

        logits = lax.dot_general(wr_ref[...], x1, (((1,), (1,)), ((), ())), preferred_element_type=F32,
                                 precision=lax.Precision.HIGHEST) + br_ref[...]
        row = lax.broadcasted_iota(jnp.int32, logits.shape, 0).astype(F32)
        work = logits
        vals, idxs = [], []
        for _ in range(TOP_K):
            m = jnp.max(work, axis=0, keepdims=True)
            idx = jnp.min(jnp.where(work == m, row, float(N_EXPERTS)), axis=0, keepdims=True)
            vals.append(m)
            idxs.append(idx)
            work = jnp.where(row == idx, -jnp.inf, work)
        exps = [jnp.exp(v - vals[0]) for v in vals]
        den = exps[0]
        for e in exps[1:]:
            den = den + e
        fill = [jnp.zeros_like(den)] * (SUBLANES - TOP_K)
        tope_ref[...] = jnp.concatenate(idxs + fill, axis=0).astype(jnp.int32)
        topg_ref[...] = jnp.concatenate([e / den for e in exps] + fill, axis=0)

    @pl.when(pl.program_id(0) < n_prompt_tiles)
    def _():
        tile(oret_p, g_p, oatt_p, x_p, pe_p)

    @pl.when(pl.program_id(0) >= n_prompt_tiles)
    def _():
        tile(oret_s, g_s, oatt_s, x_s, pe_s)


def _mixer(prompt, sample, weights):
    (gn_w, gn_b, w_out, ln1_g, ln1_b, w_router, b_router, w_pl_gate, b_pl_gate, w_pl_proj) = weights
    tp, ts = prompt[3].shape[0], sample[3].shape[0]
    tm = TOKEN_TILE
    assert tp % tm == 0 and ts % tm == 0
    npt, nst = tp // tm, ts // tm
    t_all = tp + ts
    avg = jnp.asarray(np.kron(np.eye(RET_HEADS), np.full((RET_V_DIM, RET_V_DIM), 1.0 / RET_V_DIM)), BF16)
    w_out_b = w_out.astype(BF16)
    row = lambda v: v.reshape(1, -1)
    consts = [avg, row(gn_w), row(gn_b), w_out_b[:RET_V], w_out_b[RET_V:], row(ln1_g), row(ln1_b),
              w_pl_gate.astype(BF16), row(b_pl_gate), w_pl_proj.astype(BF16), w_router.T, b_router.reshape(-1, 1)]
    p_spec = lambda a: pl.BlockSpec((tm, a.shape[1]), lambda i: (jnp.minimum(i, npt - 1), 0))
    s_spec = lambda a: pl.BlockSpec((tm, a.shape[1]), lambda i: (jnp.maximum(i - npt, 0), 0))
    c_spec = lambda a: pl.BlockSpec(a.shape, lambda i: (0, 0))
    return pl.pallas_call(
        functools.partial(_mixer_kernel, n_prompt_tiles=npt),
        grid=(npt + nst,),
        in_specs=[p_spec(a) for a in prompt] + [s_spec(a) for a in sample] + [c_spec(a) for a in consts],
        out_specs=[pl.BlockSpec((tm * ROW_TILES, LANES), lambda i: (i, 0)),
                   pl.BlockSpec((tm, D_MODEL), lambda i: (i, 0)),
                   pl.BlockSpec((SUBLANES, tm), lambda i: (0, i)),
                   pl.BlockSpec((SUBLANES, tm), lambda i: (0, i))],
        out_shape=[jax.ShapeDtypeStruct((t_all * ROW_TILES, LANES), F32),
                   jax.ShapeDtypeStruct((t_all, D_MODEL), F32),
                   jax.ShapeDtypeStruct((SUBLANES, t_all), jnp.int32),
                   jax.ShapeDtypeStruct((SUBLANES, t_all), F32)],
        compiler_params=_cparams(("arbitrary",)),
        name="mixer_out",
    )(*prompt, *sample, *consts)


def _deinterleave_kernel(w_ref, g_ref, l_ref, t_ref):
    t_ref[...] = w_ref[0].T
    half = t_ref.shape[0] // 2
    g_ref[0] = t_ref[pl.ds(0, half, stride=2), :].T.astype(BF16)
    l_ref[0] = t_ref[pl.ds(1, half, stride=2), :].T.astype(BF16)


def _deinterleave_w1(w_e1):
    e, d, f2 = w_e1.shape
    spec = pl.BlockSpec((1, LANES, f2 // 2), lambda i, c: (i, c, 0))
    return pl.pallas_call(
        _deinterleave_kernel,
        grid=(e, d // LANES),
        in_specs=[pl.BlockSpec((1, LANES, f2), lambda i, c: (i, c, 0))],
        out_specs=[spec, spec],
        out_shape=[jax.ShapeDtypeStruct((e, d, f2 // 2), BF16)] * 2,
        scratch_shapes=[pltpu.VMEM((f2, LANES), F32)],
        compiler_params=_cparams(("arbitrary", "arbitrary")),
        name="deinterleave_w1",
    )(w_e1)


def _routing_plan(top_e, n_tokens):
    a = n_tokens * TOP_K
    mb, tc = MOE_BLOCK, COMBINE_TILE
    flat_e = top_e.reshape(-1)
    order = jnp.argsort(flat_e).astype(jnp.int32)
    experts = jnp.arange(N_EXPERTS, dtype=jnp.int32)
    counts = jnp.sum((flat_e[:, None] == experts[None, :]).astype(jnp.int32), axis=0)
    padded = (counts + mb - 1) // mb * mb
    start = jnp.cumsum(counts) - counts
    pend = jnp.cumsum(padded)
    pstart = pend - padded
    n_blocks = -(-(a + N_EXPERTS * (mb - 1)) // mb)
    block_first = jnp.arange(n_blocks, dtype=jnp.int32) * mb
    block_e = jnp.minimum(jnp.sum((pend[None, :] <= block_first[:, None]).astype(jnp.int32), axis=1),
                          N_EXPERTS - 1).astype(jnp.int32)
    n_used = (pend[-1] // mb).astype(jnp.int32).reshape(1)
    e_p = jnp.repeat(block_e, mb)
    off = jnp.arange(n_blocks * mb, dtype=jnp.int32) - pstart[e_p]
    valid = off < counts[e_p]
    asg = order[jnp.clip(start[e_p] + off, 0, a - 1)]
    t, k = asg // TOP_K, asg % TOP_K
    row_real = ((t // tc) * TOP_K + k) * tc + t % tc
    row_pad = a + (pstart[e_p] - start[e_p]) + (off - counts[e_p])
    tok = jnp.where(valid, t, 0)
    row = jnp.where(valid, row_real, row_pad)
    idx = jnp.stack([tok.reshape(n_blocks, mb), row.reshape(n_blocks, mb)], axis=1)
    extra = jnp.stack([jnp.zeros((mb,), jnp.int32), n_blocks * mb + jnp.arange(mb, dtype=jnp.int32)])[None]
    return jnp.concatenate([idx, extra], axis=0), block_e, n_used


IDX_RING = 4
EXPERT_CHUNKS = 4


def _row_tile(r):
    start = r * ROW_TILES
    return pl.ds(start if isinstance(r, int) else pl.multiple_of(start, ROW_TILES), ROW_TILES)


def _expert_kernel(be_ref, nused_ref, idx_hbm, x_hbm, w1g_ref, w1l_ref, b1g_ref, b1l_ref, w2_ref, b2_ref,
                   y_hbm, xbuf, ybuf, idx_smem, gsem, ssem, isem, *, n_blocks):
    i = pl.program_id(0)
    n_used = nused_ref[0]
    mb = MOE_BLOCK
    rows = mb * ROW_TILES
    cur, other = i % 2, (i + 1) % 2

    def ring(block):
        return (block + IDX_RING) % IDX_RING

    def idx_copy(block):
        src = jnp.where(block < 0, n_blocks, jnp.minimum(block, n_blocks))
        return pltpu.make_async_copy(idx_hbm.at[src], idx_smem.at[ring(block)], isem.at[ring(block)])

    def gather_copy(r, tok, s):
        return pltpu.make_async_copy(x_hbm.at[_row_tile(tok), :], xbuf.at[s, _row_tile(r), :], gsem.at[s])

    def scatter_copy(r, row, s):
        return pltpu.make_async_copy(ybuf.at[s, _row_tile(r), :], y_hbm.at[_row_tile(row), :], ssem.at[s])

    def gathered(s):
        return pltpu.make_async_copy(x_hbm.at[pl.ds(0, rows), :], xbuf.at[s], gsem.at[s])

    def scattered(s):
        return pltpu.make_async_copy(ybuf.at[s], y_hbm.at[pl.ds(0, rows), :], ssem.at[s])

    def issue_rolled(copy, block, column, s):
        def body(r, carry):
            copy(r, idx_smem[ring(block), column, r], s).start()
            return carry
        lax.fori_loop(0, mb, body, 0)

    @pl.when(i == 0)
    def _():
        idx_copy(i).start()
        idx_copy(i - 1).start()
        idx_copy(i + 1).start()
        idx_copy(i).wait()
        idx_copy(i - 1).wait()
        issue_rolled(gather_copy, i, 0, cur)
        ybuf[other] = jnp.zeros((rows, LANES), F32)

    @pl.when(i < n_used)
    def _():
        idx_copy(i + 1).wait()
        idx_copy(i + 2).start()
        gathered(cur).wait()

        @pl.when(i >= 1)
        def _():
            scattered(cur).wait()

        def x_rows(first, count):
            return jnp.concatenate([xbuf[cur, pl.ds(first * ROW_TILES + s, count, stride=ROW_TILES), :]
                                    for s in range(ROW_TILES)], axis=1).astype(BF16)

        head = 16
        x_rest = x_rows(head, mb - head)
        y = None
        per = mb // EXPERT_CHUNKS
        cw = D_EXPERT // EXPERT_CHUNKS
        for c in range(EXPERT_CHUNKS):
            for r in range(c * per, (c + 1) * per):
                gather_copy(r, idx_smem[ring(i + 1), 0, r], other).start(priority=r % 2)
                scatter_copy(r, idx_smem[ring(i - 1), 1, r], other).start(priority=r % 2)
            x = jnp.concatenate([x_rows(0, head), x_rest], axis=0)
            cols = slice(c * cw, (c + 1) * cw)
            glu = jnp.minimum(_dot(x, w1g_ref[0, :, cols]) + b1g_ref[0, :, cols], SWIGLU_LIMIT)
            lin = jnp.clip(_dot(x, w1l_ref[0, :, cols]) + b1l_ref[0, :, cols], -SWIGLU_LIMIT, SWIGLU_LIMIT)
            act = (glu * jax.nn.sigmoid(SWIGLU_ALPHA * glu) * (lin + 1.0)).astype(BF16)
            part = _dot(act, w2_ref[0, cols, :])
            y = part if y is None else y + part
        y = y + b2_ref[0]
        for s in range(ROW_TILES):
            ybuf[cur, pl.ds(s, mb, stride=ROW_TILES), :] = y[:, s * LANES:(s + 1) * LANES]

    @pl.when(i == n_used - 1)
    def _():
        issue_rolled(scatter_copy, i, 1, cur)
        scattered(other).wait()
        scattered(cur).wait()
        gathered(other).wait()
        idx_copy(i + 2).wait()

    @pl.when(i >= n_used)
    def _():
        ybuf[cur] = jnp.zeros((rows, LANES), F32)
        fill = pltpu.make_async_copy(ybuf.at[cur], y_hbm.at[pl.ds(pl.multiple_of(i * rows, rows), rows), :],
                                     ssem.at[cur])
        fill.start()
        fill.wait()


def _experts(x1r, idx, block_e, n_used, w1g, w1l, b_e1, w_e2, b_e2):
    n_blocks = idx.shape[0] - 1
    mb = MOE_BLOCK
    rows = mb * ROW_TILES
    b1g, b1l = b_e1[:, None, 0::2], b_e1[:, None, 1::2]
    w2 = w_e2.astype(BF16)
    b2 = b_e2[:, None, :]
    wspec = pl.BlockSpec((1, D_MODEL, D_EXPERT), lambda i, be, nu: (be[i], 0, 0))
    bspec = pl.BlockSpec((1, 1, D_EXPERT), lambda i, be, nu: (be[i], 0, 0))
    return pl.pallas_call(
        functools.partial(_expert_kernel, n_blocks=n_blocks),
        grid_spec=pltpu.PrefetchScalarGridSpec(
            num_scalar_prefetch=2,
            grid=(n_blocks,),
            in_specs=[pl.BlockSpec(memory_space=pl.ANY),
                      pl.BlockSpec(memory_space=pl.ANY),
                      wspec, wspec, bspec, bspec,
                      pl.BlockSpec((1, D_EXPERT, D_MODEL), lambda i, be, nu: (be[i], 0, 0)),
                      pl.BlockSpec((1, 1, D_MODEL), lambda i, be, nu: (be[i], 0, 0))],
            out_specs=pl.BlockSpec(memory_space=pl.ANY),
            scratch_shapes=[pltpu.VMEM((2, rows, LANES), F32),
                            pltpu.VMEM((2, rows, LANES), F32),
                            pltpu.SMEM((IDX_RING, 2, mb), jnp.int32),
                            pltpu.SemaphoreType.DMA((2,)),
                            pltpu.SemaphoreType.DMA((2,)),
                            pltpu.SemaphoreType.DMA((IDX_RING,))]),
        out_shape=jax.ShapeDtypeStruct(((n_blocks + 1) * rows, LANES), F32),
        compiler_params=_cparams(("arbitrary",)),
        name="experts",
    )(block_e, n_used, idx, x1r, w1g, w1l, b1g, b1l, w2, b2)


def _combine_kernel(y_ref, gate_ref, resid_ref, g_ref, b_ref, op_ref, os_ref, *, n_prompt_tiles):
    tm = COMBINE_TILE
    z = resid_ref[...]
    gate = gate_ref[...]
    for kk in range(TOP_K):
        y_k = jnp.concatenate(
            [y_ref[pl.ds(kk * tm * ROW_TILES + s, tm, stride=ROW_TILES), :] for s in range(ROW_TILES)], axis=1)
        z = z + y_k * gate[:, kk:kk + 1]
    out = _layer_norm(z, g_ref[...], b_ref[...])

    @pl.when(pl.program_id(0) < n_prompt_tiles)
    def _():
        op_ref[...] = out

    @pl.when(pl.program_id(0) >= n_prompt_tiles)
    def _():
        os_ref[...] = out


def _combine(ys, top_g, resid, ln2_g, ln2_b, n_prompt):
    t_all = resid.shape[0]
    tm = COMBINE_TILE
    assert n_prompt % tm == 0 and (t_all - n_prompt) % tm == 0
    npt, n_tiles = n_prompt // tm, t_all // tm
    return pl.pallas_call(
        functools.partial(_combine_kernel, n_prompt_tiles=npt),
        grid=(n_tiles,),
        in_specs=[pl.BlockSpec((tm * TOP_K * ROW_TILES, LANES), lambda i: (i, 0)),
                  pl.BlockSpec((tm, top_g.shape[1]), lambda i: (i, 0)),
                  pl.BlockSpec((tm, D_MODEL), lambda i: (i, 0)),
                  pl.BlockSpec((1, D_MODEL), lambda i: (0, 0)),
                  pl.BlockSpec((1, D_MODEL), lambda i: (0, 0))],
        out_specs=[pl.BlockSpec((tm, D_MODEL), lambda i: (jnp.minimum(i, npt - 1), 0)),
                   pl.BlockSpec((tm, D_MODEL), lambda i: (jnp.maximum(i - npt, 0), 0))],
        out_shape=[jax.ShapeDtypeStruct((n_prompt, D_MODEL), F32),
                   jax.ShapeDtypeStruct((t_all - n_prompt, D_MODEL), F32)],
        compiler_params=_cparams(("arbitrary",)),
        name="combine_ln2",
    )(ys, top_g, resid, ln2_g.reshape(1, -1), ln2_b.reshape(1, -1))


def kernel(x_prompt, x_sample, cache_att_k, cache_att_v, state_ret, p_prompt, p_sample, w_in, ret_gn_w, ret_gn_b, w_out, ln1_g, ln1_b, w_router, b_router, w_e1, b_e1, w_e2, b_e2, w_pl_gate, b_pl_gate, w_pl_proj, ln2_g, ln2_b):
    assert w_in.shape[0] == DEPTH == 1
    B, S, _ = x_prompt.shape
    DB, T, _ = x_sample.shape
    H, D = ATT_HEADS, ATT_HEAD_DIM
    l = 0
    w_in_b = w_in[l].astype(BF16)

    pos_p = jnp.arange(S, dtype=F32)
    pd, pt = _in_projection(x_prompt, w_in_b, pos_p, PROJ_NAMES, ("k_a", "v_a"), TOKEN_TILE, False)
    zero_state = jnp.zeros((B, 2, RET_QK // 2, RET_V // 2), F32)
    o_ret_p, st_p = _retention(pd["q_r"], pd["k_r"], pd["v_r"], zero_state, RET_CHUNK, RET_CHUNK)
    o_att_p = _attention_prompt(pd["q_a"], pd["k_a"], pd["v_a"])
    keep = min(MAX_WINDOW, S)
    as_cache = lambda a: a.reshape(B, H, D, S).transpose(0, 3, 1, 2)[None, :, S - keep:]
    new_k_p, new_v_p = as_cache(pt["k_a"]), as_cache(pt["v_a"])
    new_st_p = _blockdiag_to_state(st_p)[None]

    ts = DB * T
    by_step = lambda a: a.transpose(1, 0, 2).reshape(ts, a.shape[-1])
    xs = by_step(x_sample)
    pos_s = jnp.repeat(PAST_LEN + jnp.arange(T, dtype=F32), DB)
    sd, st = _in_projection(xs[None], w_in_b, pos_s, ("g_r", "q_a", "k_a", "v_a"),
                            ("q_r", "k_r", "v_r", "k_a", "v_a"), DB, True)
    state_t = jnp.transpose(state_ret[l], (1, 2, 3, 0))
    o_ret_t, state_new = _retention_sample(st["q_r"], st["k_r"], st["v_r"], state_t)
    o_ret_s = o_ret_t.transpose(0, 2, 1).reshape(ts, RET_V)
    by_batch = lambda a: a[0].reshape(T, DB, -1).transpose(1, 0, 2)
    cache_t = lambda c: jnp.transpose(c[l], (0, 2, 3, 1))
    o_att_s = _attention_sample(by_batch(sd["q_a"]), by_batch(sd["k_a"]), by_batch(sd["v_a"]),
                                cache_t(cache_att_k), cache_t(cache_att_v))
    as_new = lambda a: a.reshape(T, H, D, DB).transpose(3, 0, 1, 2)[None]
    new_k_s, new_v_s = as_new(st["k_a"]), as_new(st["v_a"])
    new_st_s = jnp.transpose(state_new, (3, 0, 1, 2)).astype(state_ret.dtype)[None]

    tp = B * S
    flat2 = lambda a: a.reshape(-1, a.shape[-1])
    prompt = (flat2(o_ret_p), flat2(pd["g_r"]), flat2(o_att_p), flat2(x_prompt), flat2(p_prompt[l]))
    sample = (o_ret_s, sd["g_r"][0], by_step(o_att_s), xs, by_step(p_sample[l]))
    x1r, resid, tope, topg = _mixer(prompt, sample, (ret_gn_w[l], ret_gn_b[l], w_out[l], ln1_g[l], ln1_b[l],
                                                     w_router[l], b_router[l], w_pl_gate[l], b_pl_gate[l],
                                                     w_pl_proj[l]))

    t_all = tp + ts
    idx, block_e, n_used = _routing_plan(tope[:TOP_K].T, t_all)
    w1g, w1l = _deinterleave_w1(w_e1[l])
    ys = _experts(x1r, idx, block_e, n_used, w1g, w1l, b_e1[l], w_e2[l], b_e2[l])
    y_p, y_s = _combine(ys, topg.T, resid, ln2_g[l], ln2_b[l], tp)
    return (y_p.reshape(B, S, D_MODEL), y_s.reshape(T, DB, D_MODEL).transpose(1, 0, 2),
            new_k_p, new_v_p, new_st_p, new_k_s, new_v_s, new_st_s)
```

```python
import functools
import math

import numpy as np
import jax
import jax.numpy as jnp
from jax import lax
from jax.experimental import pallas as pl
from jax.experimental.pallas import tpu as pltpu

F32 = jnp.float32
BF16 = jnp.bfloat16

D_MODEL = 1024
DEPTH = 1
PAST_LEN = 8192
RET_HEADS = 8
RET_QK_DIM = 32
RET_V_DIM = 64
RET_CHUNK = 128
RET_ROPE_BASE = 10000.0
ATT_HEADS = 8
ATT_HEAD_DIM = 64
ROT_DIMS = ATT_HEAD_DIM // 4
ROPE_THETA = 500000.0
DILATED_PATTERNS = ((128, 1), (512, 4), (2048, 16))
MAX_WINDOW = 2048
RET_QK = RET_HEADS * RET_QK_DIM
RET_V = RET_HEADS * RET_V_DIM
ATT_W = ATT_HEADS * ATT_HEAD_DIM
IN_COLS = RET_QK * 2 + RET_V * 2 + ATT_W * 3
N_EXPERTS = 32
TOP_K = 4
D_EXPERT = D_MODEL
SWIGLU_ALPHA = 1.702
SWIGLU_LIMIT = 7.0
PLE_DIM = 256
LN_EPS = 1e-5
GN_EPS = 1e-6
DEEPNORM_ALPHA = (2 * DEPTH) ** 0.25
NEG_INF = -1e30

LANES = 128
SUBLANES = 8
ROW_TILES = D_MODEL // LANES
VMEM_LIMIT = 56 * 1024 * 1024

TOKEN_TILE = 512
MOE_BLOCK = 256
COMBINE_TILE = 256
ATT_BLOCK = 128
ATT_GROUP = 16


def _dot(a, b):
    return jnp.dot(a, b, preferred_element_type=F32)


def _dot_nt(a, b):
    return lax.dot_general(a, b, (((1,), (1,)), ((), ())), preferred_element_type=F32)


def _dot_tn(a, b):
    return lax.dot_general(a, b, (((0,), (0,)), ((), ())), preferred_element_type=F32)


def _cparams(sem):
    return pltpu.CompilerParams(dimension_semantics=sem, vmem_limit_bytes=VMEM_LIMIT)


def _rotary_tables(pos, inv_freq, head_dim, n_heads):
    half = inv_freq.shape[0]
    ang = pos.astype(F32)[:, None] * inv_freq[None, :]
    cos, sin = jnp.cos(ang), jnp.sin(ang)
    rest = head_dim - 2 * half
    n = pos.shape[0]
    cos_h = jnp.concatenate([cos, cos, jnp.ones((n, rest), F32)], axis=1)
    sin_h = jnp.concatenate([-sin, sin, jnp.zeros((n, rest), F32)], axis=1)
    return jnp.tile(cos_h, (1, n_heads)), jnp.tile(sin_h, (1, n_heads))


def _ret_inv_freq():
    return 1.0 / (RET_ROPE_BASE ** jnp.linspace(0.0, 1.0, RET_QK_DIM // 2, dtype=F32))


def _att_inv_freq():
    return ROPE_THETA ** (-jnp.arange(0, ROT_DIMS, 2, dtype=F32) / ROT_DIMS)


def _ret_log_decay():
    return jnp.log(1.0 - 2.0 ** (-5.0 - jnp.arange(RET_HEADS, dtype=F32)))


def _retention_tables(c_true, c_pad):
    lg = _ret_log_decay()
    idx = jnp.arange(c_pad, dtype=F32)
    live = idx < c_true
    diff = idx[:, None] - idx[None, :]
    decay = jnp.where(diff[None] >= 0, jnp.exp(jnp.maximum(diff, 0.0)[None] * lg[:, None, None]), 0.0)
    decay = jnp.where(live[None, :, None] & live[None, None, :], decay, 0.0)
    cross = jnp.exp((idx + 1.0)[:, None] * lg[None, :])
    cross = jnp.where(live[:, None], cross, 0.0)
    kdec = jnp.exp((c_true - 1.0 - idx)[:, None] * lg[None, :])
    kdec = jnp.where(live[:, None], kdec, 0.0)
    sdec = jnp.exp(c_true * lg)
    hh = RET_HEADS // 2
    decay = decay.reshape(2, hh, c_pad, c_pad)
    cross = jnp.repeat(cross, RET_V_DIM, axis=1).reshape(c_pad, 2, hh * RET_V_DIM).transpose(1, 0, 2)
    kdec = jnp.repeat(kdec, RET_QK_DIM, axis=1).reshape(c_pad, 2, hh * RET_QK_DIM).transpose(1, 0, 2)
    sdec = jnp.broadcast_to(jnp.repeat(sdec, RET_V_DIM)[None, :], (hh * RET_QK_DIM, RET_V)) \
        .reshape(hh * RET_QK_DIM, 2, hh * RET_V_DIM).transpose(1, 0, 2)
    return decay, cross, kdec, sdec


def _rotate(h, cos, sin, half, period):
    outs = []
    for j in range(h.shape[1] // LANES):
        blk = h[:, j * LANES:(j + 1) * LANES]
        lane = lax.broadcasted_iota(jnp.int32, blk.shape, 1)
        first = (lane % period) < half
        partner = jnp.where(first, pltpu.roll(blk, LANES - half, 1), pltpu.roll(blk, half, 1))
        outs.append(blk * cos[:, j * LANES:(j + 1) * LANES] + partner * sin[:, j * LANES:(j + 1) * LANES])
    return jnp.concatenate(outs, axis=1)


PROJ_NAMES = ("q_r", "k_r", "v_r", "g_r", "q_a", "k_a", "v_a")
PROJ_WIDTHS = dict(zip(PROJ_NAMES, (RET_QK, RET_QK, RET_V, RET_V, ATT_W, ATT_W, ATT_W)))


def _inproj_kernel(x_ref, w_ref, cr_ref, sr_ref, ca_ref, sa_ref, *out_refs, direct, transposed):
    x = x_ref[0].astype(BF16)
    cr, sr, ca, sa = cr_ref[...], sr_ref[...], ca_ref[...], sa_ref[...]
    finish = {
        "q_r": lambda h: _rotate(h, cr, sr, RET_QK_DIM // 2, RET_QK_DIM),
        "k_r": lambda h: _rotate(h, cr, sr, RET_QK_DIM // 2, RET_QK_DIM) * (RET_QK_DIM ** -0.5),
        "q_a": lambda h: _rotate(h, ca, sa, ROT_DIMS // 2, ATT_HEAD_DIM) * (ATT_HEAD_DIM ** -0.5),
        "k_a": lambda h: _rotate(h, ca, sa, ROT_DIMS // 2, ATT_HEAD_DIM),
    }
    refs = dict(zip([("d", n) for n in direct] + [("t", n) for n in transposed], out_refs))
    c = 0
    for name in PROJ_NAMES:
        width = PROJ_WIDTHS[name]
        if name in direct or name in transposed:
            h = _dot(x, w_ref[:, c:c + width])
            h = finish.get(name, lambda v: v)(h)
            if name in direct:
                refs[("d", name)][0] = h
            if name in transposed:
                refs[("t", name)][0] = h.T
        c += width


def _in_projection(x, w_in_b, pos, direct, transposed, tm, tile_major):
    B, S, _ = x.shape
    assert S % tm == 0 and (B == 1 or not tile_major)
    cr, sr = _rotary_tables(pos, _ret_inv_freq(), RET_QK_DIM, RET_HEADS)
    ca, sa = _rotary_tables(pos, _att_inv_freq(), ATT_HEAD_DIM, ATT_HEADS)
    tab = lambda w: pl.BlockSpec((tm, w), lambda s, b: (s, 0))
    specs = [pl.BlockSpec((1, tm, PROJ_WIDTHS[n]), lambda s, b: (b, s, 0)) for n in direct]
    shapes = [jax.ShapeDtypeStruct((B, S, PROJ_WIDTHS[n]), F32) for n in direct]
    for n in transposed:
        w = PROJ_WIDTHS[n]
        if tile_major:
            specs.append(pl.BlockSpec((1, w, tm), lambda s, b: (s, 0, 0)))
            shapes.append(jax.ShapeDtypeStruct((S // tm, w, tm), F32))
        else:
            specs.append(pl.BlockSpec((1, w, tm), lambda s, b: (b, 0, s)))
            shapes.append(jax.ShapeDtypeStruct((B, w, S), F32))
    outs = pl.pallas_call(
        functools.partial(_inproj_kernel, direct=direct, transposed=transposed),
        grid=(S // tm, B),
        in_specs=[pl.BlockSpec((1, tm, D_MODEL), lambda s, b: (b, s, 0)),
                  pl.BlockSpec((D_MODEL, IN_COLS), lambda s, b: (0, 0)),
                  tab(RET_QK), tab(RET_QK), tab(ATT_W), tab(ATT_W)],
        out_specs=specs,
        out_shape=shapes,
        compiler_params=_cparams(("arbitrary", "arbitrary")),
        name="in_projection",
    )(x, w_in_b, cr, sr, ca, sa)
    return dict(zip(direct, outs[:len(direct)])), dict(zip(transposed, outs[len(direct):]))


def _retention_kernel(q_ref, k_ref, v_ref, s0_ref, dec_ref, cross_ref, kdec_ref, sdec_ref,
                      o_ref, so_ref, *, chunk, n_chunks):
    hh = RET_HEADS // 2
    qk_w, v_w = hh * RET_QK_DIM, hh * RET_V_DIM
    lane_q = lax.broadcasted_iota(jnp.int32, (chunk, qk_w), 1) // RET_QK_DIM
    lane_v = lax.broadcasted_iota(jnp.int32, (chunk, v_w), 1) // RET_V_DIM
    blockdiag = (lax.broadcasted_iota(jnp.int32, (qk_w, v_w), 0) // RET_QK_DIM
                 == lax.broadcasted_iota(jnp.int32, (qk_w, v_w), 1) // RET_V_DIM)
    state = s0_ref[0, 0]
    cross, kdec, sdec = cross_ref[0], kdec_ref[0], sdec_ref[0]
    for ci in range(n_chunks):
        rows = pl.ds(ci * chunk, chunk)
        q, k, v = q_ref[0, rows, :], k_ref[0, rows, :], v_ref[0, rows, :]
        kb, vb = k.astype(BF16), v.astype(BF16)
        o = _dot(q.astype(BF16), state.astype(BF16)) * cross
        for h in range(hh):
            qm = jnp.where(lane_q == h, q, 0.0).astype(BF16)
            inner = _dot_nt(qm, kb) * dec_ref[0, h]
            o = jnp.where(lane_v == h, o + _dot(inner.astype(BF16), vb), o)
        o_ref[0, rows, :] = o
        upd = _dot_tn((k * kdec).astype(BF16), vb)
        state = state * sdec + jnp.where(blockdiag, upd, 0.0)
    so_ref[0, 0] = state


def _retention(q, k, v, state_bd, c_true, c_pad):
    B, S, _ = q.shape
    hh = RET_HEADS // 2
    qk_w, v_w = hh * RET_QK_DIM, hh * RET_V_DIM
    decay, cross, kdec, sdec = _retention_tables(c_true, c_pad)
    kern = functools.partial(_retention_kernel, chunk=c_pad, n_chunks=S // c_pad)
    return pl.pallas_call(
        kern,
        grid=(B, 2),
        in_specs=[pl.BlockSpec((1, S, qk_w), lambda b, g: (b, 0, g)),
                  pl.BlockSpec((1, S, qk_w), lambda b, g: (b, 0, g)),
                  pl.BlockSpec((1, S, v_w), lambda b, g: (b, 0, g)),
                  pl.BlockSpec((1, 1, qk_w, v_w), lambda b, g: (b, g, 0, 0)),
                  pl.BlockSpec((1, hh, c_pad, c_pad), lambda b, g: (g, 0, 0, 0)),
                  pl.BlockSpec((1, c_pad, v_w), lambda b, g: (g, 0, 0)),
                  pl.BlockSpec((1, c_pad, qk_w), lambda b, g: (g, 0, 0)),
                  pl.BlockSpec((1, qk_w, v_w), lambda b, g: (g, 0, 0))],
        out_specs=[pl.BlockSpec((1, S, v_w), lambda b, g: (b, 0, g)),
                   pl.BlockSpec((1, 1, qk_w, v_w), lambda b, g: (b, g, 0, 0))],
        out_shape=[jax.ShapeDtypeStruct((B, S, RET_V), F32),
                   jax.ShapeDtypeStruct((B, 2, qk_w, v_w), F32)],
        compiler_params=_cparams(("arbitrary", "arbitrary")),
        name="retention",
    )(q, k, v, state_bd, decay, cross, kdec, sdec)


def _blockdiag_to_state(bd):
    B = bd.shape[0]
    hh = RET_HEADS // 2
    s = bd.reshape(B, 2, hh, RET_QK_DIM, hh, RET_V_DIM)
    s = jnp.stack([s[:, :, h, :, h, :] for h in range(hh)], axis=2)
    return s.reshape(B, RET_HEADS, RET_QK_DIM, RET_V_DIM)


def _retention_sample_kernel(gam_ref, q_ref, k_ref, v_ref, s_ref, o_ref, so_ref, *, steps):
    h = pl.program_id(0)
    gam = [gam_ref[h, p] for p in range(steps + 1)]
    q = [q_ref[t] for t in range(steps)]
    k = [k_ref[t] for t in range(steps)]
    v = [v_ref[t] for t in range(steps)]
    o = [jnp.zeros(v[0].shape, F32) for _ in range(steps)]
    for d in range(RET_QK_DIM):
        s_d = s_ref[0, d]
        new = s_d * gam[steps]
        for t in range(steps):
            o[t] = o[t] + q[t][d:d + 1, :] * s_d
            new = new + (k[t][d:d + 1, :] * gam[steps - 1 - t]) * v[t]
        so_ref[0, d] = new
    for i in range(steps):
        o[i] = o[i] * gam[i + 1]
        for j in range(i + 1):
            qk = jnp.sum(q[i] * k[j], axis=0, keepdims=True) * gam[i - j]
            o[i] = o[i] + qk * v[j]
        o_ref[i] = o[i]


def _retention_sample(q_t, k_t, v_t, state):
    steps, _, nb = q_t.shape
    gam = jnp.exp(_ret_log_decay()[:, None] * jnp.arange(steps + 1, dtype=F32)[None, :])
    qk_spec = pl.BlockSpec((steps, RET_QK_DIM, nb), lambda h: (0, h, 0))
    v_spec = pl.BlockSpec((steps, RET_V_DIM, nb), lambda h: (0, h, 0))
    s_spec = pl.BlockSpec((1, RET_QK_DIM, RET_V_DIM, nb), lambda h: (h, 0, 0, 0))
    return pl.pallas_call(
        functools.partial(_retention_sample_kernel, steps=steps),
        grid=(RET_HEADS,),
        in_specs=[pl.BlockSpec(memory_space=pltpu.SMEM), qk_spec, qk_spec, v_spec, s_spec],
        out_specs=[v_spec, s_spec],
        out_shape=[jax.ShapeDtypeStruct(v_t.shape, F32), jax.ShapeDtypeStruct(state.shape, F32)],
        compiler_params=_cparams(("arbitrary",)),
        name="retention_sample",
    )(gam, q_t, k_t, v_t, state)


def _attn_prompt_kernel(q_ref, k_ref, v_ref, o_ref, o_ref_p, lse_ref, *, seq):
    n = ATT_BLOCK
    head0 = lax.broadcasted_iota(jnp.int32, (n, LANES), 1) < ATT_HEAD_DIM
    head0_kv = lax.broadcasted_iota(jnp.int32, (2 * n, LANES), 1) < ATT_HEAD_DIM
    a = lax.broadcasted_iota(jnp.int32, (n, 2 * n), 0)
    j = lax.broadcasted_iota(jnp.int32, (n, 2 * n), 1)
    bias_pc = jnp.where(((j < n) & (j >= a)) | ((j >= n) & (j - n <= a)), 0.0, NEG_INF)
    bias_c = bias_pc[:, n:]
    bias_first = jnp.where(j < n, NEG_INF, bias_pc)

    def rows(start, d):
        if d > 1:
            return pl.ds(start, n, stride=d)
        return pl.ds(start if isinstance(start, int) else pl.multiple_of(start, n), n)

    def blocks(p_idx, d, specs):
        loaded = []
        for cur, prev, has_prev in specs:
            q = q_ref[0, rows(cur, d), :]
            if has_prev is False:
                kk, vv = k_ref[0, rows(cur, d), :], v_ref[0, rows(cur, d), :]
                bias, hkv = bias_c, head0
            else:
                kk = jnp.concatenate([k_ref[0, rows(prev, d), :], k_ref[0, rows(cur, d), :]], axis=0)
                vv = jnp.concatenate([v_ref[0, rows(prev, d), :], v_ref[0, rows(cur, d), :]], axis=0)
                bias = bias_pc if has_prev is True else jnp.where(has_prev, bias_pc, bias_first)
                hkv = head0_kv
            loaded.append((q, kk.astype(BF16), vv, bias, hkv))
        scores = []
        for q, kb, vv, bias, hkv in loaded:
            for h in range(2):
                mine = head0 if h == 0 else jnp.logical_not(head0)
                scores.append(_dot_nt(jnp.where(mine, q, 0.0).astype(BF16), kb) + bias)
        probs, maxes = [], []
        for s in scores:
            m = jnp.max(s, axis=1, keepdims=True)
            probs.append(jnp.exp(s - m).astype(BF16))
            maxes.append(m)
        pvs = []
        for u, (q, kb, vv, bias, hkv) in enumerate(loaded):
            for h in range(2):
                mine_kv = hkv if h == 0 else jnp.logical_not(hkv)
                pvs.append(_dot(probs[2 * u + h], jnp.where(mine_kv, vv, 1.0).astype(BF16)))
        for u, (cur, prev, has_prev) in enumerate(specs):
            pv0, pv1 = pvs[2 * u], pvs[2 * u + 1]
            acc_u = jnp.where(head0, pv0, pv1)
            l_u = pltpu.roll(jnp.where(head0, pv1, pv0), ATT_HEAD_DIM, 1)
            m_u = jnp.where(head0, maxes[2 * u], maxes[2 * u + 1])
            o_ref_p[p_idx, rows(cur, d), :] = acc_u / l_u
            lse_ref[p_idx, rows(cur, d), :] = m_u + jnp.log(l_u)

    G = ATT_GROUP
    for p_idx, (window, d) in enumerate(DILATED_PATTERNS):
        nb = (seq // d) // n
        assert (d * nb) % G == 0 and (nb % G == 0 or G % nb == 0)

        def group(g, carry, p_idx=p_idx, d=d, nb=nb):
            specs = []
            for jj in range(G):
                if nb > G:
                    first = g * G
                    r = 0 if d == 1 else first // nb
                    c = first % nb + jj
                    specs.append((c * (n * d) + r, jnp.maximum(c - 1, 0) * (n * d) + r,
                                  True if jj > 0 else c > 0))
                else:
                    r = g * (G // nb) + jj // nb
                    c = jj % nb
                    specs.append((c * (n * d) + r, (c - 1) * (n * d) + r, c > 0))
            blocks(p_idx, d, specs)
            return carry

        lax.fori_loop(0, d * nb // G, group, 0)

    def merge(c, carry):
        rs = pl.ds(pl.multiple_of(c * n, n), n)
        lse = [lse_ref[p, rs, :] for p in range(len(DILATED_PATTERNS))]
        top = functools.reduce(jnp.maximum, lse)
        w = [jnp.exp(x - top) for x in lse]
        num = functools.reduce(lambda x, y: x + y, [w[p] * o_ref_p[p, rs, :] for p in range(len(w))])
        o_ref[0, rs, :] = num / functools.reduce(lambda x, y: x + y, w)
        return carry

    lax.fori_loop(0, seq // n, merge, 0)


def _attention_prompt(q, k, v):
    B, S, _ = q.shape
    spec = pl.BlockSpec((1, S, LANES), lambda b, g: (b, 0, g))
    return pl.pallas_call(
        functools.partial(_attn_prompt_kernel, seq=S),
        grid=(B, ATT_W // LANES),
        in_specs=[spec, spec, spec],
        out_specs=spec,
        out_shape=jax.ShapeDtypeStruct((B, S, ATT_W), F32),
        scratch_shapes=[pltpu.VMEM((len(DILATED_PATTERNS), S, LANES), F32)] * 2,
        compiler_params=_cparams(("arbitrary", "arbitrary")),
        name="attention_prompt",
    )(q, k, v)


def _attn_sample_kernel(q_ref, kn_ref, vn_ref, kc_ref, vc_ref, cc_ref, cn_ref, o_ref, *, steps):
    q = q_ref[0].astype(BF16)
    width = q.shape[1]
    kc = kc_ref[0].reshape(width, -1).astype(BF16)
    vc = vc_ref[0].reshape(width, -1).astype(BF16)
    cnt_c, cnt_n = cc_ref[...], cn_ref[...]
    s_c = jnp.where(cnt_c > 0.0, _dot(q, kc), NEG_INF)
    s_n = jnp.where(cnt_n > 0.0, _dot_nt(q, kn_ref[0].astype(BF16)), NEG_INF)
    m = jnp.maximum(s_c.max(axis=1, keepdims=True), s_n.max(axis=1, keepdims=True))
    p_c = cnt_c * jnp.exp(s_c - m)
    p_n = cnt_n * jnp.exp(s_n - m)
    den = p_c.sum(axis=1, keepdims=True) + p_n.sum(axis=1, keepdims=True)
    num = _dot_nt(p_c.astype(BF16), vc) + _dot(p_n.astype(BF16), vn_ref[0].astype(BF16))
    rows = lax.broadcasted_iota(jnp.int32, num.shape, 0)
    lanes = lax.broadcasted_iota(jnp.int32, num.shape, 1)
    out = jnp.where(rows % ATT_HEADS == lanes // ATT_HEAD_DIM, num / den, 0.0)
    for t in range(steps):
        o_ref[0, t:t + 1, :] = jnp.sum(out[t * ATT_HEADS:(t + 1) * ATT_HEADS], axis=0, keepdims=True)


def _sample_key_counts(dec_seq, buf, new_pad):
    t_q = np.repeat(np.arange(dec_seq), ATT_HEADS)[:, None]
    back = buf + t_q - np.arange(buf)[None, :]
    cnt_c = np.zeros(back.shape, np.float32)
    for w, d in DILATED_PATTERNS:
        cnt_c += ((back % d == 0) & (back >= d) & (back <= w)).astype(np.float32)
    back = t_q - np.arange(new_pad)[None, :]
    cnt_n = np.zeros(back.shape, np.float32)
    for w, d in DILATED_PATTERNS:
        cnt_n += ((back >= 0) & (back % d == 0) & (back <= w)).astype(np.float32)
    cnt_n *= (np.arange(new_pad) < dec_seq)[None, :]
    return jnp.asarray(cnt_c), jnp.asarray(cnt_n)


def _attention_sample(q, k_new, v_new, cache_k_t, cache_v_t):
    B, T, width = q.shape
    _, H, D, buf = cache_k_t.shape
    new_pad = SUBLANES
    cnt_c, cnt_n = _sample_key_counts(T, buf, new_pad)
    q_rows = (q.reshape(B, T, 1, H, D) * jnp.eye(H, dtype=F32)[None, None, :, :, None]).reshape(B, T * H, width)
    pad = lambda x: jnp.pad(x, ((0, 0), (0, new_pad - T), (0, 0)))
    new_spec = pl.BlockSpec((1, new_pad, width), lambda b: (b, 0, 0))
    cache_spec = pl.BlockSpec((1, H, D, buf), lambda b: (b, 0, 0, 0))
    const = lambda x: pl.BlockSpec(x.shape, lambda b: (0, 0))
    return pl.pallas_call(
        functools.partial(_attn_sample_kernel, steps=T),
        grid=(B,),
        in_specs=[pl.BlockSpec((1, T * H, width), lambda b: (b, 0, 0)), new_spec, new_spec,
                  cache_spec, cache_spec, const(cnt_c), const(cnt_n)],
        out_specs=pl.BlockSpec((1, T, width), lambda b: (b, 0, 0)),
        out_shape=jax.ShapeDtypeStruct((B, T, width), F32),
        compiler_params=_cparams(("arbitrary",)),
        name="attention_sample",
    )(q_rows, pad(k_new), pad(v_new), cache_k_t, cache_v_t, cnt_c, cnt_n)


def _layer_norm(y, g, b):
    mu = jnp.mean(y, axis=-1, keepdims=True)
    d = y - mu
    var = jnp.mean(d * d, axis=-1, keepdims=True)
    return d * lax.rsqrt(var + LN_EPS) * g + b


def _mixer_kernel(oret_p, g_p, oatt_p, x_p, pe_p, oret_s, g_s, oatt_s, x_s, pe_s,
                  avg_ref, gnw_ref, gnb_ref, wor_ref, woa_ref, ln1g_ref, ln1b_ref,
                  wg_ref, bg_ref, wp_ref, wr_ref, br_ref, tri_ref,
                  x1r_ref, resid_ref, tope_ref, topg_ref, rank_ref, count_ref, run_ref, *, n_prompt_tiles):
    @pl.when(pl.program_id(0) == 0)
    def _():
        run_ref[...] = jnp.zeros_like(run_ref)

    def tile(oret_ref, g_ref, oatt_ref, x_ref, pe_ref):
        o_ret, g, o_att, x, pe = oret_ref[...], g_ref[...], oatt_ref[...], x_ref[...], pe_ref[...]
        avg = avg_ref[...]

        def head_mean(z):
            hi = z.astype(BF16)
            lo = (z - hi.astype(F32)).astype(BF16)
            return _dot(hi, avg) + _dot(lo, avg)

        d = o_ret - head_mean(o_ret)
        o_n = d * lax.rsqrt(head_mean(d * d) + GN_EPS) * gnw_ref[...] + gnb_ref[...]
        ret = g * jax.nn.sigmoid(g) * o_n
        mix = _dot(ret.astype(BF16), wor_ref[...]) + _dot(o_att.astype(BF16), woa_ref[...])
        x1 = _layer_norm(DEEPNORM_ALPHA * x + mix, ln1g_ref[...], ln1b_ref[...])
        x1b = x1.astype(BF16)
        ple = jax.nn.sigmoid(_dot(x1b, wg_ref[...]) + bg_ref[...]) * _dot(pe.astype(BF16), wp_ref[...])
        resid_ref[...] = DEEPNORM_ALPHA * x1 + ple
        tm = x1.shape[0]
        for s in range(ROW_TILES):
            x1r_ref[pl.ds(s, tm, stride=ROW_TILES), :] = x1[:, s * LANES:(s + 1) * LANES]

        logits = lax.dot_general(wr_ref[...], x1, (((1,), (1,)), ((), ())), preferred_element_type=F32,
                                 precision=lax.Precision.HIGHEST) + br_ref[...]
        row = lax.broadcasted_iota(jnp.int32, logits.shape, 0).astype(F32)
        work = logits
        vals, idxs = [], []
        for _ in range(TOP_K):
            m = jnp.max(work, axis=0, keepdims=True)
            idx = jnp.min(jnp.where(work == m, row, float(N_EXPERTS)), axis=0, keepdims=True)
            vals.append(m)
            idxs.append(idx)
            work = jnp.where(row == idx, -jnp.inf, work)
        exps = [jnp.exp(v - vals[0]) for v in vals]
        den = exps[0]
        for e in exps[1:]:
            den = den + e
        fill = [jnp.zeros_like(den)] * (SUBLANES - TOP_K)
        tope_ref[...] = jnp.concatenate(idxs + fill, axis=0).astype(jnp.int32)
        topg_ref[...] = jnp.concatenate([e / den for e in exps] + fill, axis=0)

        chosen = [row == idx for idx in idxs]
        multi = functools.reduce(lambda u, v: u + v, [jnp.where(c, 1.0, 0.0) for c in chosen])
        before = run_ref[...] + _dot(multi.astype(BF16), tri_ref[...]) - multi
        ranks = [jnp.sum(jnp.where(c, before, 0.0), axis=0, keepdims=True) for c in chosen]
        rank_ref[...] = jnp.concatenate(ranks + fill, axis=0).astype(jnp.int32)
        run_ref[...] = run_ref[...] + jnp.sum(multi, axis=1, keepdims=True)
        count_ref[...] = run_ref[...]

    @pl.when(pl.program_id(0) < n_prompt_tiles)
    def _():
        tile(oret_p, g_p, oatt_p, x_p, pe_p)

    @pl.when(pl.program_id(0) >= n_prompt_tiles)
    def _():
        tile(oret_s, g_s, oatt_s, x_s, pe_s)


def _mixer(prompt, sample, weights):
    (gn_w, gn_b, w_out, ln1_g, ln1_b, w_router, b_router, w_pl_gate, b_pl_gate, w_pl_proj) = weights
    tp, ts = prompt[3].shape[0], sample[3].shape[0]
    tm = TOKEN_TILE
    assert tp % tm == 0 and ts % tm == 0
    npt, nst = tp // tm, ts // tm
    t_all = tp + ts
    avg = jnp.asarray(np.kron(np.eye(RET_HEADS), np.full((RET_V_DIM, RET_V_DIM), 1.0 / RET_V_DIM)), BF16)
    w_out_b = w_out.astype(BF16)
    row = lambda v: v.reshape(1, -1)
    tri = jnp.asarray(np.triu(np.ones((tm, tm), np.float32)), BF16)
    consts = [avg, row(gn_w), row(gn_b), w_out_b[:RET_V], w_out_b[RET_V:], row(ln1_g), row(ln1_b),
              w_pl_gate.astype(BF16), row(b_pl_gate), w_pl_proj.astype(BF16), w_router.T, b_router.reshape(-1, 1),
              tri]
    p_spec = lambda a: pl.BlockSpec((tm, a.shape[1]), lambda i: (jnp.minimum(i, npt - 1), 0))
    s_spec = lambda a: pl.BlockSpec((tm, a.shape[1]), lambda i: (jnp.maximum(i - npt, 0), 0))
    c_spec = lambda a: pl.BlockSpec(a.shape, lambda i: (0, 0))
    per_token = pl.BlockSpec((SUBLANES, tm), lambda i: (0, i))
    return pl.pallas_call(
        functools.partial(_mixer_kernel, n_prompt_tiles=npt),
        grid=(npt + nst,),
        in_specs=[p_spec(a) for a in prompt] + [s_spec(a) for a in sample] + [c_spec(a) for a in consts],
        out_specs=[pl.BlockSpec((tm * ROW_TILES, LANES), lambda i: (i, 0)),
                   pl.BlockSpec((tm, D_MODEL), lambda i: (i, 0)),
                   per_token, per_token, per_token,
                   pl.BlockSpec((N_EXPERTS, 1), lambda i: (0, 0))],
        out_shape=[jax.ShapeDtypeStruct((t_all * ROW_TILES, LANES), F32),
                   jax.ShapeDtypeStruct((t_all, D_MODEL), F32),
                   jax.ShapeDtypeStruct((SUBLANES, t_all), jnp.int32),
                   jax.ShapeDtypeStruct((SUBLANES, t_all), F32),
                   jax.ShapeDtypeStruct((SUBLANES, t_all), jnp.int32),
                   jax.ShapeDtypeStruct((N_EXPERTS, 1), F32)],
        scratch_shapes=[pltpu.VMEM((N_EXPERTS, 1), F32)],
        compiler_params=_cparams(("arbitrary",)),
        name="mixer_out",
    )(*prompt, *sample, *consts)


def _deinterleave_kernel(w_ref, g_ref, l_ref, t_ref):
    t_ref[...] = w_ref[0].T
    half = t_ref.shape[0] // 2
    g_ref[0] = t_ref[pl.ds(0, half, stride=2), :].T.astype(BF16)
    l_ref[0] = t_ref[pl.ds(1, half, stride=2), :].T.astype(BF16)


def _deinterleave_w1(w_e1):
    e, d, f2 = w_e1.shape
    spec = pl.BlockSpec((1, LANES, f2 // 2), lambda i, c: (i, c, 0))
    return pl.pallas_call(
        _deinterleave_kernel,
        grid=(e, d // LANES),
        in_specs=[pl.BlockSpec((1, LANES, f2), lambda i, c: (i, c, 0))],
        out_specs=[spec, spec],
        out_shape=[jax.ShapeDtypeStruct((e, d, f2 // 2), BF16)] * 2,
        scratch_shapes=[pltpu.VMEM((f2, LANES), F32)],
        compiler_params=_cparams(("arbitrary", "arbitrary")),
        name="deinterleave_w1",
    )(w_e1)


def _routing_plan(top_e, rank, counts, n_tokens):
    a = n_tokens * TOP_K
    mb, tc = MOE_BLOCK, COMBINE_TILE
    flat_e = top_e.reshape(-1)
    order = jnp.argsort(flat_e).astype(jnp.int32)
    padded = (counts + mb - 1) // mb * mb
    start = jnp.cumsum(counts) - counts
    pend = jnp.cumsum(padded)
    pstart = pend - padded
    n_blocks = -(-(a + N_EXPERTS * (mb - 1)) // mb)
    block_first = jnp.arange(n_blocks, dtype=jnp.int32) * mb
    block_e = jnp.minimum(jnp.sum((pend[None, :] <= block_first[:, None]).astype(jnp.int32), axis=1),
                          N_EXPERTS - 1).astype(jnp.int32)
    n_used = (pend[-1] // mb).astype(jnp.int32).reshape(1)
    e_p = jnp.repeat(block_e, mb)
    off = jnp.arange(n_blocks * mb, dtype=jnp.int32) - pstart[e_p]
    asg = order[jnp.clip(start[e_p] + off, 0, a - 1)]
    tok = jnp.where(off < counts[e_p], asg // TOP_K, 0).reshape(n_blocks, mb)
    slot_of = (pstart[flat_e] + rank.reshape(-1)).reshape(n_tokens // tc, tc, TOP_K)
    return tok, slot_of.transpose(0, 2, 1).reshape(n_tokens // tc, tc * TOP_K), block_e, n_used


EXPERT_CHUNKS = 4


def _row_tile(r):
    start = r * ROW_TILES
    return pl.ds(start if isinstance(r, int) else pl.multiple_of(start, ROW_TILES), ROW_TILES)


def _expert_kernel(be_ref, nused_ref, tok_hbm, x_hbm, w1g_ref, w1l_ref, b1g_ref, b1l_ref, w2_ref, b2_ref,
                   y_ref, xbuf, tok_smem, gsem, tsem, *, n_blocks):
    i = pl.program_id(0)
    n_used = nused_ref[0]
    mb = MOE_BLOCK
    rows = mb * ROW_TILES
    cur, other = i % 2, (i + 1) % 2

    def tok_copy(block):
        return pltpu.make_async_copy(tok_hbm.at[jnp.minimum(block, n_blocks - 1)], tok_smem.at[block % 2],
                                     tsem.at[block % 2])

    def gather_copy(r, tok, s):
        return pltpu.make_async_copy(x_hbm.at[_row_tile(tok), :], xbuf.at[s, _row_tile(r), :], gsem.at[s])

    def gathered(s):
        return pltpu.make_async_copy(x_hbm.at[pl.ds(0, rows), :], xbuf.at[s], gsem.at[s])

    @pl.when(i == 0)
    def _():
        tok_copy(i).start()
        tok_copy(i).wait()

        def first(r, carry):
            gather_copy(r, tok_smem[cur, r], cur).start()
            return carry
        lax.fori_loop(0, mb, first, 0)
        tok_copy(i + 1).start()

    @pl.when(i < n_used)
    def _():
        tok_copy(i + 1).wait()
        tok_copy(i + 2).start()
        gathered(cur).wait()

        def x_rows(first, count):
            return jnp.concatenate([xbuf[cur, pl.ds(first * ROW_TILES + s, count, stride=ROW_TILES), :]
                                    for s in range(ROW_TILES)], axis=1).astype(BF16)

        head = 16
        x_rest = x_rows(head, mb - head)
        groups = 2 * EXPERT_CHUNKS
        per = mb // groups
        cw = D_EXPERT // EXPERT_CHUNKS
        issued = 0

        def x_after_copies():
            nonlocal issued
            for r in range(issued, issued + per):
                gather_copy(r, tok_smem[other, r], other).start(priority=r % 2)
            issued += per
            return jnp.concatenate([x_rows(0, head), x_rest], axis=0)

        y = None
        for c in range(EXPERT_CHUNKS):
            cols = slice(c * cw, (c + 1) * cw)
            glu = jnp.minimum(_dot(x_after_copies(), w1g_ref[0, :, cols]) + b1g_ref[0, :, cols], SWIGLU_LIMIT)
            lin = jnp.clip(_dot(x_after_copies(), w1l_ref[0, :, cols]) + b1l_ref[0, :, cols],
                           -SWIGLU_LIMIT, SWIGLU_LIMIT)
            act = (glu * jax.nn.sigmoid(SWIGLU_ALPHA * glu) * (lin + 1.0)).astype(BF16)
            part = _dot(act, w2_ref[0, cols, :])
            y = part if y is None else y + part
        y = y + b2_ref[0]
        for s in range(ROW_TILES):
            y_ref[pl.ds(s, mb, stride=ROW_TILES), :] = y[:, s * LANES:(s + 1) * LANES]

    @pl.when(i == n_used - 1)
    def _():
        gathered(other).wait()
        tok_copy(i + 2).wait()

    @pl.when(i >= n_used)
    def _():
        y_ref[...] = jnp.zeros_like(y_ref)


def _experts(x1r, slot_tok, block_e, n_used, w1g, w1l, b_e1, w_e2, b_e2):
    n_blocks = slot_tok.shape[0]
    mb = MOE_BLOCK
    rows = mb * ROW_TILES
    b1g, b1l = b_e1[:, None, 0::2], b_e1[:, None, 1::2]
    w2 = w_e2.astype(BF16)
    b2 = b_e2[:, None, :]
    wspec = pl.BlockSpec((1, D_MODEL, D_EXPERT), lambda i, be, nu: (be[i], 0, 0))
    bspec = pl.BlockSpec((1, 1, D_EXPERT), lambda i, be, nu: (be[i], 0, 0))
    return pl.pallas_call(
        functools.partial(_expert_kernel, n_blocks=n_blocks),
        grid_spec=pltpu.PrefetchScalarGridSpec(
            num_scalar_prefetch=2,
            grid=(n_blocks,),
            in_specs=[pl.BlockSpec(memory_space=pl.ANY),
                      pl.BlockSpec(memory_space=pl.ANY),
                      wspec, wspec, bspec, bspec,
                      pl.BlockSpec((1, D_EXPERT, D_MODEL), lambda i, be, nu: (be[i], 0, 0)),
                      pl.BlockSpec((1, 1, D_MODEL), lambda i, be, nu: (be[i], 0, 0))],
            out_specs=pl.BlockSpec((rows, LANES), lambda i, be, nu: (i, 0)),
            scratch_shapes=[pltpu.VMEM((2, rows, LANES), F32),
                            pltpu.SMEM((2, mb), jnp.int32),
                            pltpu.SemaphoreType.DMA((2,)),
                            pltpu.SemaphoreType.DMA((2,))]),
        out_shape=jax.ShapeDtypeStruct((n_blocks * rows, LANES), F32),
        compiler_params=_cparams(("arbitrary",)),
        name="experts",
    )(block_e, n_used, slot_tok, x1r, w1g, w1l, b1g, b1l, w2, b2)


def _combine_kernel(slot_hbm, y_hbm, gate_ref, resid_ref, g_ref, b_ref, op_ref, os_ref,
                    ybuf, slot_smem, gsem, ssem, *, n_prompt_tiles, n_tiles):
    i = pl.program_id(0)
    tm = COMBINE_TILE
    n_rows = tm * TOP_K
    cur, other = i % 2, (i + 1) % 2

    def slot_copy(tile):
        return pltpu.make_async_copy(slot_hbm.at[jnp.minimum(tile, n_tiles - 1)], slot_smem.at[tile % 2],
                                     ssem.at[tile % 2])

    def gather_copy(r, slot, s):
        return pltpu.make_async_copy(y_hbm.at[_row_tile(slot), :], ybuf.at[s, _row_tile(r), :], gsem.at[s])

    def gathered(s):
        return pltpu.make_async_copy(y_hbm.at[pl.ds(0, n_rows * ROW_TILES), :], ybuf.at[s], gsem.at[s])

    @pl.when(i == 0)
    def _():
        slot_copy(i).start()
        slot_copy(i).wait()

        def first(r, carry):
            gather_copy(r, slot_smem[cur, r], cur).start()
            return carry
        lax.fori_loop(0, n_rows, first, 0)
        slot_copy(i + 1).start()

    slot_copy(i + 1).wait()
    slot_copy(i + 2).start()

    @pl.when(i + 1 < n_tiles)
    def _():
        for r in range(n_rows):
            gather_copy(r, slot_smem[other, r], other).start(priority=r % 2)

    gathered(cur).wait()
    z = resid_ref[...]
    gate = gate_ref[...]
    for kk in range(TOP_K):
        y_k = jnp.concatenate(
            [ybuf[cur, pl.ds(kk * tm * ROW_TILES + s, tm, stride=ROW_TILES), :] for s in range(ROW_TILES)], axis=1)
        z = z + y_k * gate[:, kk:kk + 1]
    out = _layer_norm(z, g_ref[...], b_ref[...])

    @pl.when(i == n_tiles - 1)
    def _():
        slot_copy(i + 2).wait()

    @pl.when(pl.program_id(0) < n_prompt_tiles)
    def _():
        op_ref[...] = out

    @pl.when(pl.program_id(0) >= n_prompt_tiles)
    def _():
        os_ref[...] = out


def _combine(ys, slots, top_g, resid, ln2_g, ln2_b, n_prompt):
    t_all = resid.shape[0]
    tm = COMBINE_TILE
    assert n_prompt % tm == 0 and (t_all - n_prompt) % tm == 0
    npt, n_tiles = n_prompt // tm, t_all // tm
    return pl.pallas_call(
        functools.partial(_combine_kernel, n_prompt_tiles=npt, n_tiles=n_tiles),
        grid=(n_tiles,),
        in_specs=[pl.BlockSpec(memory_space=pl.ANY),
                  pl.BlockSpec(memory_space=pl.ANY),
                  pl.BlockSpec((tm, top_g.shape[1]), lambda i: (i, 0)),
                  pl.BlockSpec((tm, D_MODEL), lambda i: (i, 0)),
                  pl.BlockSpec((1, D_MODEL), lambda i: (0, 0)),
                  pl.BlockSpec((1, D_MODEL), lambda i: (0, 0))],
        out_specs=[pl.BlockSpec((tm, D_MODEL), lambda i: (jnp.minimum(i, npt - 1), 0)),
                   pl.BlockSpec((tm, D_MODEL), lambda i: (jnp.maximum(i - npt, 0), 0))],
        out_shape=[jax.ShapeDtypeStruct((n_prompt, D_MODEL), F32),
                   jax.ShapeDtypeStruct((t_all - n_prompt, D_MODEL), F32)],
        scratch_shapes=[pltpu.VMEM((2, tm * TOP_K * ROW_TILES, LANES), F32),
                        pltpu.SMEM((2, tm * TOP_K), jnp.int32),
                        pltpu.SemaphoreType.DMA((2,)),
                        pltpu.SemaphoreType.DMA((2,))],
        compiler_params=_cparams(("arbitrary",)),
        name="combine_ln2",
    )(slots, ys, top_g, resid, ln2_g.reshape(1, -1), ln2_b.reshape(1, -1))


def kernel(x_prompt, x_sample, cache_att_k, cache_att_v, state_ret, p_prompt, p_sample, w_in, ret_gn_w, ret_gn_b, w_out, ln1_g, ln1_b, w_router, b_router, w_e1, b_e1, w_e2, b_e2, w_pl_gate, b_pl_gate, w_pl_proj, ln2_g, ln2_b):
    assert w_in.shape[0] == DEPTH == 1
    B, S, _ = x_prompt.shape
    DB, T, _ = x_sample.shape
    H, D = ATT_HEADS, ATT_HEAD_DIM
    l = 0
    w_in_b = w_in[l].astype(BF16)

    pos_p = jnp.arange(S, dtype=F32)
    pd, pt = _in_projection(x_prompt, w_in_b, pos_p, PROJ_NAMES, ("k_a", "v_a"), TOKEN_TILE, False)
    zero_state = jnp.zeros((B, 2, RET_QK // 2, RET_V // 2), F32)
    o_ret_p, st_p = _retention(pd["q_r"], pd["k_r"], pd["v_r"], zero_state, RET_CHUNK, RET_CHUNK)
    o_att_p = _attention_prompt(pd["q_a"], pd["k_a"], pd["v_a"])
    keep = min(MAX_WINDOW, S)
    as_cache = lambda a: a.reshape(B, H, D, S).transpose(0, 3, 1, 2)[None, :, S - keep:]
    new_k_p, new_v_p = as_cache(pt["k_a"]), as_cache(pt["v_a"])
    new_st_p = _blockdiag_to_state(st_p)[None]

    ts = DB * T
    by_step = lambda a: a.transpose(1, 0, 2).reshape(ts, a.shape[-1])
    xs = by_step(x_sample)
    pos_s = jnp.repeat(PAST_LEN + jnp.arange(T, dtype=F32), DB)
    sd, st = _in_projection(xs[None], w_in_b, pos_s, ("g_r", "q_a", "k_a", "v_a"),
                            ("q_r", "k_r", "v_r", "k_a", "v_a"), DB, True)
    state_t = jnp.transpose(state_ret[l], (1, 2, 3, 0))
    o_ret_t, state_new = _retention_sample(st["q_r"], st["k_r"], st["v_r"], state_t)
    o_ret_s = o_ret_t.transpose(0, 2, 1).reshape(ts, RET_V)
    by_batch = lambda a: a[0].reshape(T, DB, -1).transpose(1, 0, 2)
    cache_t = lambda c: jnp.transpose(c[l], (0, 2, 3, 1))
    o_att_s = _attention_sample(by_batch(sd["q_a"]), by_batch(sd["k_a"]), by_batch(sd["v_a"]),
                                cache_t(cache_att_k), cache_t(cache_att_v))
    as_new = lambda a: a.reshape(T, H, D, DB).transpose(3, 0, 1, 2)[None]
    new_k_s, new_v_s = as_new(st["k_a"]), as_new(st["v_a"])
    new_st_s = jnp.transpose(state_new, (3, 0, 1, 2)).astype(state_ret.dtype)[None]

    tp = B * S
    flat2 = lambda a: a.reshape(-1, a.shape[-1])
    prompt = (flat2(o_ret_p), flat2(pd["g_r"]), flat2(o_att_p), flat2(x_prompt), flat2(p_prompt[l]))
    sample = (o_ret_s, sd["g_r"][0], by_step(o_att_s), xs, by_step(p_sample[l]))
    x1r, resid, tope, topg, rank, counts = _mixer(prompt, sample, (
        ret_gn_w[l], ret_gn_b[l], w_out[l], ln1_g[l], ln1_b[l], w_router[l], b_router[l],
        w_pl_gate[l], b_pl_gate[l], w_pl_proj[l]))

    t_all = tp + ts
    slot_tok, slots, block_e, n_used = _routing_plan(tope[:TOP_K].T, rank[:TOP_K].T,
                                                     counts[:, 0].astype(jnp.int32), t_all)
    w1g, w1l = _deinterleave_w1(w_e1[l])
    ys = _experts(x1r, slot_tok, block_e, n_used, w1g, w1l, b_e1[l], w_e2[l], b_e2[l])
    y_p, y_s = _combine(ys, slots, topg.T, resid, ln2_g[l], ln2_b[l], tp)
    return (y_p.reshape(B, S, D_MODEL), y_s.reshape(T, DB, D_MODEL).transpose(1, 0, 2),
            new_k_p, new_v_p, new_st_p, new_k_s, new_v_s, new_st_s)
```

```python
import functools

import numpy as np
import jax
import jax.numpy as jnp
from jax import lax
from jax.experimental import pallas as pl
from jax.experimental.pallas import tpu as pltpu

F32 = jnp.float32
BF16 = jnp.bfloat16

D_MODEL = 1024
DEPTH = 1
PAST_LEN = 8192
RET_HEADS = 8
RET_QK_DIM = 32
RET_V_DIM = 64
RET_CHUNK = 128
RET_ROPE_BASE = 10000.0
ATT_HEADS = 8
ATT_HEAD_DIM = 64
ROT_DIMS = ATT_HEAD_DIM // 4
ROPE_THETA = 500000.0
DILATED_PATTERNS = ((128, 1), (512, 4), (2048, 16))
MAX_WINDOW = 2048
RET_QK = RET_HEADS * RET_QK_DIM
RET_V = RET_HEADS * RET_V_DIM
ATT_W = ATT_HEADS * ATT_HEAD_DIM
IN_COLS = RET_QK * 2 + RET_V * 2 + ATT_W * 3
N_EXPERTS = 32
TOP_K = 4
D_EXPERT = D_MODEL
SWIGLU_ALPHA = 1.702
SWIGLU_LIMIT = 7.0
PLE_DIM = 256
LN_EPS = 1e-5
GN_EPS = 1e-6
DEEPNORM_ALPHA = (2 * DEPTH) ** 0.25
NEG_INF = -1e30

LANES = 128
SUBLANES = 8
ROW_TILES = D_MODEL // LANES
VMEM_LIMIT = 56 * 1024 * 1024

TOKEN_TILE = 512
MOE_BLOCK = 256
COMBINE_TILE = 256
ATT_BLOCK = 128
ATT_GROUP = 16


def _dot(a, b):
    return jnp.dot(a, b, preferred_element_type=F32)


def _dot_nt(a, b):
    return lax.dot_general(a, b, (((1,), (1,)), ((), ())), preferred_element_type=F32)


def _dot_tn(a, b):
    return lax.dot_general(a, b, (((0,), (0,)), ((), ())), preferred_element_type=F32)


def _cparams(sem):
    return pltpu.CompilerParams(dimension_semantics=sem, vmem_limit_bytes=VMEM_LIMIT)


def _rotary_tables(pos, inv_freq, head_dim, n_heads):
    half = inv_freq.shape[0]
    ang = pos.astype(F32)[:, None] * inv_freq[None, :]
    cos, sin = jnp.cos(ang), jnp.sin(ang)
    rest = head_dim - 2 * half
    n = pos.shape[0]
    cos_h = jnp.concatenate([cos, cos, jnp.ones((n, rest), F32)], axis=1)
    sin_h = jnp.concatenate([-sin, sin, jnp.zeros((n, rest), F32)], axis=1)
    return jnp.tile(cos_h, (1, n_heads)), jnp.tile(sin_h, (1, n_heads))


def _ret_inv_freq():
    return 1.0 / (RET_ROPE_BASE ** jnp.linspace(0.0, 1.0, RET_QK_DIM // 2, dtype=F32))


def _att_inv_freq():
    return ROPE_THETA ** (-jnp.arange(0, ROT_DIMS, 2, dtype=F32) / ROT_DIMS)


def _ret_log_decay():
    return jnp.log(1.0 - 2.0 ** (-5.0 - jnp.arange(RET_HEADS, dtype=F32)))


def _retention_tables(c_true, c_pad):
    lg = _ret_log_decay()
    idx = jnp.arange(c_pad, dtype=F32)
    live = idx < c_true
    diff = idx[:, None] - idx[None, :]
    decay = jnp.where(diff[None] >= 0, jnp.exp(jnp.maximum(diff, 0.0)[None] * lg[:, None, None]), 0.0)
    decay = jnp.where(live[None, :, None] & live[None, None, :], decay, 0.0)
    cross = jnp.exp((idx + 1.0)[:, None] * lg[None, :])
    cross = jnp.where(live[:, None], cross, 0.0)
    kdec = jnp.exp((c_true - 1.0 - idx)[:, None] * lg[None, :])
    kdec = jnp.where(live[:, None], kdec, 0.0)
    sdec = jnp.exp(c_true * lg)
    hh = RET_HEADS // 2
    decay = decay.reshape(2, hh, c_pad, c_pad)
    cross = jnp.repeat(cross, RET_V_DIM, axis=1).reshape(c_pad, 2, hh * RET_V_DIM).transpose(1, 0, 2)
    kdec = jnp.repeat(kdec, RET_QK_DIM, axis=1).reshape(c_pad, 2, hh * RET_QK_DIM).transpose(1, 0, 2)
    sdec = jnp.broadcast_to(jnp.repeat(sdec, RET_V_DIM)[None, :], (hh * RET_QK_DIM, RET_V)) \
        .reshape(hh * RET_QK_DIM, 2, hh * RET_V_DIM).transpose(1, 0, 2)
    return decay, cross, kdec, sdec


def _rotate(h, cos, sin, half, period):
    outs = []
    for j in range(h.shape[1] // LANES):
        blk = h[:, j * LANES:(j + 1) * LANES]
        lane = lax.broadcasted_iota(jnp.int32, blk.shape, 1)
        first = (lane % period) < half
        partner = jnp.where(first, pltpu.roll(blk, LANES - half, 1), pltpu.roll(blk, half, 1))
        outs.append(blk * cos[:, j * LANES:(j + 1) * LANES] + partner * sin[:, j * LANES:(j + 1) * LANES])
    return jnp.concatenate(outs, axis=1)


PROJ_NAMES = ("q_r", "k_r", "v_r", "g_r", "q_a", "k_a", "v_a")
PROJ_WIDTHS = dict(zip(PROJ_NAMES, (RET_QK, RET_QK, RET_V, RET_V, ATT_W, ATT_W, ATT_W)))


def _inproj_kernel(x_ref, w_ref, cr_ref, sr_ref, ca_ref, sa_ref, *out_refs, direct, transposed):
    x = x_ref[0].astype(BF16)
    cr, sr, ca, sa = cr_ref[...], sr_ref[...], ca_ref[...], sa_ref[...]
    finish = {
        "q_r": lambda h: _rotate(h, cr, sr, RET_QK_DIM // 2, RET_QK_DIM),
        "k_r": lambda h: _rotate(h, cr, sr, RET_QK_DIM // 2, RET_QK_DIM) * (RET_QK_DIM ** -0.5),
        "q_a": lambda h: _rotate(h, ca, sa, ROT_DIMS // 2, ATT_HEAD_DIM) * (ATT_HEAD_DIM ** -0.5),
        "k_a": lambda h: _rotate(h, ca, sa, ROT_DIMS // 2, ATT_HEAD_DIM),
    }
    refs = dict(zip([("d", n) for n in direct] + [("t", n) for n in transposed], out_refs))
    c = 0
    for name in PROJ_NAMES:
        width = PROJ_WIDTHS[name]
        if name in direct or name in transposed:
            h = _dot(x, w_ref[:, c:c + width])
            h = finish.get(name, lambda v: v)(h)
            if name in direct:
                refs[("d", name)][0] = h
            if name in transposed:
                refs[("t", name)][0] = h.T
        c += width


def _in_projection(x, w_in_b, pos, direct, transposed, tm, tile_major):
    B, S, _ = x.shape
    assert S % tm == 0 and (B == 1 or not tile_major)
    cr, sr = _rotary_tables(pos, _ret_inv_freq(), RET_QK_DIM, RET_HEADS)
    ca, sa = _rotary_tables(pos, _att_inv_freq(), ATT_HEAD_DIM, ATT_HEADS)
    tab = lambda w: pl.BlockSpec((tm, w), lambda s, b: (s, 0))
    specs = [pl.BlockSpec((1, tm, PROJ_WIDTHS[n]), lambda s, b: (b, s, 0)) for n in direct]
    shapes = [jax.ShapeDtypeStruct((B, S, PROJ_WIDTHS[n]), F32) for n in direct]
    for n in transposed:
        w = PROJ_WIDTHS[n]
        if tile_major:
            specs.append(pl.BlockSpec((1, w, tm), lambda s, b: (s, 0, 0)))
            shapes.append(jax.ShapeDtypeStruct((S // tm, w, tm), F32))
        else:
            specs.append(pl.BlockSpec((1, w, tm), lambda s, b: (b, 0, s)))
            shapes.append(jax.ShapeDtypeStruct((B, w, S), F32))
    outs = pl.pallas_call(
        functools.partial(_inproj_kernel, direct=direct, transposed=transposed),
        grid=(S // tm, B),
        in_specs=[pl.BlockSpec((1, tm, D_MODEL), lambda s, b: (b, s, 0)),
                  pl.BlockSpec((D_MODEL, IN_COLS), lambda s, b: (0, 0)),
                  tab(RET_QK), tab(RET_QK), tab(ATT_W), tab(ATT_W)],
        out_specs=specs,
        out_shape=shapes,
        compiler_params=_cparams(("arbitrary", "arbitrary")),
        name="in_projection",
    )(x, w_in_b, cr, sr, ca, sa)
    return dict(zip(direct, outs[:len(direct)])), dict(zip(transposed, outs[len(direct):]))


def _retention_kernel(q_ref, k_ref, v_ref, s0_ref, dec_ref, cross_ref, kdec_ref, sdec_ref,
                      o_ref, so_ref, *, chunk, n_chunks):
    hh = RET_HEADS // 2
    qk_w, v_w = hh * RET_QK_DIM, hh * RET_V_DIM
    lane_q = lax.broadcasted_iota(jnp.int32, (chunk, qk_w), 1) // RET_QK_DIM
    lane_v = lax.broadcasted_iota(jnp.int32, (chunk, v_w), 1) // RET_V_DIM
    blockdiag = (lax.broadcasted_iota(jnp.int32, (qk_w, v_w), 0) // RET_QK_DIM
                 == lax.broadcasted_iota(jnp.int32, (qk_w, v_w), 1) // RET_V_DIM)
    state = s0_ref[0, 0]
    cross, kdec, sdec = cross_ref[0], kdec_ref[0], sdec_ref[0]
    for ci in range(n_chunks):
        rows = pl.ds(ci * chunk, chunk)
        q, k, v = q_ref[0, rows, :], k_ref[0, rows, :], v_ref[0, rows, :]
        kb, vb = k.astype(BF16), v.astype(BF16)
        o = _dot(q.astype(BF16), state.astype(BF16)) * cross
        for h in range(hh):
            qm = jnp.where(lane_q == h, q, 0.0).astype(BF16)
            inner = _dot_nt(qm, kb) * dec_ref[0, h]
            o = jnp.where(lane_v == h, o + _dot(inner.astype(BF16), vb), o)
        o_ref[0, rows, :] = o
        upd = _dot_tn((k * kdec).astype(BF16), vb)
        state = state * sdec + jnp.where(blockdiag, upd, 0.0)
    so_ref[0, 0] = state


def _retention(q, k, v, state_bd, c_true, c_pad):
    B, S, _ = q.shape
    hh = RET_HEADS // 2
    qk_w, v_w = hh * RET_QK_DIM, hh * RET_V_DIM
    decay, cross, kdec, sdec = _retention_tables(c_true, c_pad)
    kern = functools.partial(_retention_kernel, chunk=c_pad, n_chunks=S // c_pad)
    return pl.pallas_call(
        kern,
        grid=(B, 2),
        in_specs=[pl.BlockSpec((1, S, qk_w), lambda b, g: (b, 0, g)),
                  pl.BlockSpec((1, S, qk_w), lambda b, g: (b, 0, g)),
                  pl.BlockSpec((1, S, v_w), lambda b, g: (b, 0, g)),
                  pl.BlockSpec((1, 1, qk_w, v_w), lambda b, g: (b, g, 0, 0)),
                  pl.BlockSpec((1, hh, c_pad, c_pad), lambda b, g: (g, 0, 0, 0)),
                  pl.BlockSpec((1, c_pad, v_w), lambda b, g: (g, 0, 0)),
                  pl.BlockSpec((1, c_pad, qk_w), lambda b, g: (g, 0, 0)),
                  pl.BlockSpec((1, qk_w, v_w), lambda b, g: (g, 0, 0))],
        out_specs=[pl.BlockSpec((1, S, v_w), lambda b, g: (b, 0, g)),
                   pl.BlockSpec((1, 1, qk_w, v_w), lambda b, g: (b, g, 0, 0))],
        out_shape=[jax.ShapeDtypeStruct((B, S, RET_V), F32),
                   jax.ShapeDtypeStruct((B, 2, qk_w, v_w), F32)],
        compiler_params=_cparams(("arbitrary", "arbitrary")),
        name="retention",
    )(q, k, v, state_bd, decay, cross, kdec, sdec)


def _blockdiag_to_state(bd):
    B = bd.shape[0]
    hh = RET_HEADS // 2
    s = bd.reshape(B, 2, hh, RET_QK_DIM, hh, RET_V_DIM)
    s = jnp.stack([s[:, :, h, :, h, :] for h in range(hh)], axis=2)
    return s.reshape(B, RET_HEADS, RET_QK_DIM, RET_V_DIM)


def _retention_sample_kernel(gam_ref, q_ref, k_ref, v_ref, s_ref, o_ref, so_ref, *, steps):
    h = pl.program_id(0)
    gam = [gam_ref[h, p] for p in range(steps + 1)]
    q = [q_ref[t] for t in range(steps)]
    k = [k_ref[t] for t in range(steps)]
    v = [v_ref[t] for t in range(steps)]
    o = [jnp.zeros(v[0].shape, F32) for _ in range(steps)]
    for d in range(RET_QK_DIM):
        s_d = s_ref[0, d]
        new = s_d * gam[steps]
        for t in range(steps):
            o[t] = o[t] + q[t][d:d + 1, :] * s_d
            new = new + (k[t][d:d + 1, :] * gam[steps - 1 - t]) * v[t]
        so_ref[0, d] = new
    for i in range(steps):
        o[i] = o[i] * gam[i + 1]
        for j in range(i + 1):
            qk = jnp.sum(q[i] * k[j], axis=0, keepdims=True) * gam[i - j]
            o[i] = o[i] + qk * v[j]
        o_ref[i] = o[i]


def _retention_sample(q_t, k_t, v_t, state):
    steps, _, nb = q_t.shape
    gam = jnp.exp(_ret_log_decay()[:, None] * jnp.arange(steps + 1, dtype=F32)[None, :])
    qk_spec = pl.BlockSpec((steps, RET_QK_DIM, nb), lambda h: (0, h, 0))
    v_spec = pl.BlockSpec((steps, RET_V_DIM, nb), lambda h: (0, h, 0))
    s_spec = pl.BlockSpec((1, RET_QK_DIM, RET_V_DIM, nb), lambda h: (h, 0, 0, 0))
    return pl.pallas_call(
        functools.partial(_retention_sample_kernel, steps=steps),
        grid=(RET_HEADS,),
        in_specs=[pl.BlockSpec(memory_space=pltpu.SMEM), qk_spec, qk_spec, v_spec, s_spec],
        out_specs=[v_spec, s_spec],
        out_shape=[jax.ShapeDtypeStruct(v_t.shape, F32), jax.ShapeDtypeStruct(state.shape, F32)],
        compiler_params=_cparams(("arbitrary",)),
        name="retention_sample",
    )(gam, q_t, k_t, v_t, state)


def _attn_prompt_kernel(q_ref, k_ref, v_ref, o_ref, o_ref_p, lse_ref, *, seq):
    n = ATT_BLOCK
    head0 = lax.broadcasted_iota(jnp.int32, (n, LANES), 1) < ATT_HEAD_DIM
    head0_kv = lax.broadcasted_iota(jnp.int32, (2 * n, LANES), 1) < ATT_HEAD_DIM
    a = lax.broadcasted_iota(jnp.int32, (n, 2 * n), 0)
    j = lax.broadcasted_iota(jnp.int32, (n, 2 * n), 1)
    bias_pc = jnp.where(((j < n) & (j >= a)) | ((j >= n) & (j - n <= a)), 0.0, NEG_INF)
    bias_c = bias_pc[:, n:]
    bias_first = jnp.where(j < n, NEG_INF, bias_pc)

    def rows(start, d):
        if d > 1:
            return pl.ds(start, n, stride=d)
        return pl.ds(start if isinstance(start, int) else pl.multiple_of(start, n), n)

    def blocks(p_idx, d, specs):
        loaded = []
        for cur, prev, has_prev in specs:
            q = q_ref[0, rows(cur, d), :]
            if has_prev is False:
                kk, vv = k_ref[0, rows(cur, d), :], v_ref[0, rows(cur, d), :]
                bias, hkv = bias_c, head0
            else:
                kk = jnp.concatenate([k_ref[0, rows(prev, d), :], k_ref[0, rows(cur, d), :]], axis=0)
                vv = jnp.concatenate([v_ref[0, rows(prev, d), :], v_ref[0, rows(cur, d), :]], axis=0)
                bias = bias_pc if has_prev is True else jnp.where(has_prev, bias_pc, bias_first)
                hkv = head0_kv
            loaded.append((q, kk.astype(BF16), vv, bias, hkv))
        scores = []
        for q, kb, vv, bias, hkv in loaded:
            for h in range(2):
                mine = head0 if h == 0 else jnp.logical_not(head0)
                scores.append(_dot_nt(jnp.where(mine, q, 0.0).astype(BF16), kb) + bias)
        probs, maxes = [], []
        for s in scores:
            m = jnp.max(s, axis=1, keepdims=True)
            probs.append(jnp.exp(s - m).astype(BF16))
            maxes.append(m)
        pvs = []
        for u, (q, kb, vv, bias, hkv) in enumerate(loaded):
            for h in range(2):
                mine_kv = hkv if h == 0 else jnp.logical_not(hkv)
                pvs.append(_dot(probs[2 * u + h], jnp.where(mine_kv, vv, 1.0).astype(BF16)))
        for u, (cur, prev, has_prev) in enumerate(specs):
            pv0, pv1 = pvs[2 * u], pvs[2 * u + 1]
            acc_u = jnp.where(head0, pv0, pv1)
            l_u = pltpu.roll(jnp.where(head0, pv1, pv0), ATT_HEAD_DIM, 1)
            m_u = jnp.where(head0, maxes[2 * u], maxes[2 * u + 1])
            o_ref_p[p_idx, rows(cur, d), :] = acc_u / l_u
            lse_ref[p_idx, rows(cur, d), :] = m_u + jnp.log(l_u)

    G = ATT_GROUP
    for p_idx, (window, d) in enumerate(DILATED_PATTERNS):
        nb = (seq // d) // n
        assert (d * nb) % G == 0 and (nb % G == 0 or G % nb == 0)

        def group(g, carry, p_idx=p_idx, d=d, nb=nb):
            specs = []
            for jj in range(G):
                if nb > G:
                    first = g * G
                    r = 0 if d == 1 else first // nb
                    c = first % nb + jj
                    specs.append((c * (n * d) + r, jnp.maximum(c - 1, 0) * (n * d) + r,
                                  True if jj > 0 else c > 0))
                else:
                    r = g * (G // nb) + jj // nb
                    c = jj % nb
                    specs.append((c * (n * d) + r, (c - 1) * (n * d) + r, c > 0))
            blocks(p_idx, d, specs)
            return carry

        lax.fori_loop(0, d * nb // G, group, 0)

    def merge(c, carry):
        rs = pl.ds(pl.multiple_of(c * n, n), n)
        lse = [lse_ref[p, rs, :] for p in range(len(DILATED_PATTERNS))]
        top = functools.reduce(jnp.maximum, lse)
        w = [jnp.exp(x - top) for x in lse]
        num = functools.reduce(lambda x, y: x + y, [w[p] * o_ref_p[p, rs, :] for p in range(len(w))])
        o_ref[0, rs, :] = num / functools.reduce(lambda x, y: x + y, w)
        return carry

    lax.fori_loop(0, seq // n, merge, 0)


def _attention_prompt(q, k, v):
    B, S, _ = q.shape
    spec = pl.BlockSpec((1, S, LANES), lambda b, g: (b, 0, g))
    return pl.pallas_call(
        functools.partial(_attn_prompt_kernel, seq=S),
        grid=(B, ATT_W // LANES),
        in_specs=[spec, spec, spec],
        out_specs=spec,
        out_shape=jax.ShapeDtypeStruct((B, S, ATT_W), F32),
        scratch_shapes=[pltpu.VMEM((len(DILATED_PATTERNS), S, LANES), F32)] * 2,
        compiler_params=_cparams(("arbitrary", "arbitrary")),
        name="attention_prompt",
    )(q, k, v)


def _attn_sample_kernel(q_ref, kn_ref, vn_ref, kc_ref, vc_ref, cc_ref, cn_ref, o_ref, *, steps):
    q = q_ref[0].astype(BF16)
    width = q.shape[1]
    kc = kc_ref[0].reshape(width, -1).astype(BF16)
    vc = vc_ref[0].reshape(width, -1).astype(BF16)
    cnt_c, cnt_n = cc_ref[...], cn_ref[...]
    s_c = jnp.where(cnt_c > 0.0, _dot(q, kc), NEG_INF)
    s_n = jnp.where(cnt_n > 0.0, _dot_nt(q, kn_ref[0].astype(BF16)), NEG_INF)
    m = jnp.maximum(s_c.max(axis=1, keepdims=True), s_n.max(axis=1, keepdims=True))
    p_c = cnt_c * jnp.exp(s_c - m)
    p_n = cnt_n * jnp.exp(s_n - m)
    den = p_c.sum(axis=1, keepdims=True) + p_n.sum(axis=1, keepdims=True)
    num = _dot_nt(p_c.astype(BF16), vc) + _dot(p_n.astype(BF16), vn_ref[0].astype(BF16))
    rows = lax.broadcasted_iota(jnp.int32, num.shape, 0)
    lanes = lax.broadcasted_iota(jnp.int32, num.shape, 1)
    out = jnp.where(rows % ATT_HEADS == lanes // ATT_HEAD_DIM, num / den, 0.0)
    for t in range(steps):
        o_ref[0, t:t + 1, :] = jnp.sum(out[t * ATT_HEADS:(t + 1) * ATT_HEADS], axis=0, keepdims=True)


def _sample_key_counts(dec_seq, buf, new_pad):
    t_q = np.repeat(np.arange(dec_seq), ATT_HEADS)[:, None]
    back = buf + t_q - np.arange(buf)[None, :]
    cnt_c = np.zeros(back.shape, np.float32)
    for w, d in DILATED_PATTERNS:
        cnt_c += ((back % d == 0) & (back >= d) & (back <= w)).astype(np.float32)
    back = t_q - np.arange(new_pad)[None, :]
    cnt_n = np.zeros(back.shape, np.float32)
    for w, d in DILATED_PATTERNS:
        cnt_n += ((back >= 0) & (back % d == 0) & (back <= w)).astype(np.float32)
    cnt_n *= (np.arange(new_pad) < dec_seq)[None, :]
    return jnp.asarray(cnt_c), jnp.asarray(cnt_n)


def _attention_sample(q, k_new, v_new, cache_k_t, cache_v_t):
    B, T, width = q.shape
    _, H, D, buf = cache_k_t.shape
    new_pad = SUBLANES
    cnt_c, cnt_n = _sample_key_counts(T, buf, new_pad)
    q_rows = (q.reshape(B, T, 1, H, D) * jnp.eye(H, dtype=F32)[None, None, :, :, None]).reshape(B, T * H, width)
    pad = lambda x: jnp.pad(x, ((0, 0), (0, new_pad - T), (0, 0)))
    new_spec = pl.BlockSpec((1, new_pad, width), lambda b: (b, 0, 0))
    cache_spec = pl.BlockSpec((1, H, D, buf), lambda b: (b, 0, 0, 0))
    const = lambda x: pl.BlockSpec(x.shape, lambda b: (0, 0))
    return pl.pallas_call(
        functools.partial(_attn_sample_kernel, steps=T),
        grid=(B,),
        in_specs=[pl.BlockSpec((1, T * H, width), lambda b: (b, 0, 0)), new_spec, new_spec,
                  cache_spec, cache_spec, const(cnt_c), const(cnt_n)],
        out_specs=pl.BlockSpec((1, T, width), lambda b: (b, 0, 0)),
        out_shape=jax.ShapeDtypeStruct((B, T, width), F32),
        compiler_params=_cparams(("arbitrary",)),
        name="attention_sample",
    )(q_rows, pad(k_new), pad(v_new), cache_k_t, cache_v_t, cnt_c, cnt_n)


def _layer_norm(y, g, b):
    mu = jnp.mean(y, axis=-1, keepdims=True)
    d = y - mu
    var = jnp.mean(d * d, axis=-1, keepdims=True)
    return d * lax.rsqrt(var + LN_EPS) * g + b


def _mixer_kernel(oret_p, g_p, oatt_p, x_p, pe_p, oret_s, g_s, oatt_s, x_s, pe_s,
                  avg_ref, gnw_ref, gnb_ref, wor_ref, woa_ref, ln1g_ref, ln1b_ref,
                  wg_ref, bg_ref, wp_ref, wr_ref, br_ref,
                  x1r_ref, resid_ref, tope_ref, topg_ref, *, n_prompt_tiles):
    def tile(oret_ref, g_ref, oatt_ref, x_ref, pe_ref):
        o_ret, g, o_att, x, pe = oret_ref[...], g_ref[...], oatt_ref[...], x_ref[...], pe_ref[...]
        avg = avg_ref[...]

        def head_mean(z):
            hi = z.astype(BF16)
            lo = (z - hi.astype(F32)).astype(BF16)
            return _dot(hi, avg) + _dot(lo, avg)

        d = o_ret - head_mean(o_ret)
        o_n = d * lax.rsqrt(head_mean(d * d) + GN_EPS) * gnw_ref[...] + gnb_ref[...]
        ret = g * jax.nn.sigmoid(g) * o_n
        mix = _dot(ret.astype(BF16), wor_ref[...]) + _dot(o_att.astype(BF16), woa_ref[...])
        x1 = _layer_norm(DEEPNORM_ALPHA * x + mix, ln1g_ref[...], ln1b_ref[...])
        x1b = x1.astype(BF16)
        ple = jax.nn.sigmoid(_dot(x1b, wg_ref[...]) + bg_ref[...]) * _dot(pe.astype(BF16), wp_ref[...])
        resid_ref[...] = DEEPNORM_ALPHA * x1 + ple
        tm = x1.shape[0]
        for s in range(ROW_TILES):
            x1r_ref[pl.ds(s, tm, stride=ROW_TILES), :] = x1[:, s * LANES:(s + 1) * LANES]

        logits = lax.dot_general(wr_ref[...], x1, (((1,), (1,)), ((), ())), preferred_element_type=F32,
                                 precision=lax.Precision.HIGHEST) + br_ref[...]
        row = lax.broadcasted_iota(jnp.int32, logits.shape, 0).astype(F32)
        work = logits
        vals, idxs = [], []
        for _ in range(TOP_K):
            m = jnp.max(work, axis=0, keepdims=True)
            idx = jnp.min(jnp.where(work == m, row, float(N_EXPERTS)), axis=0, keepdims=True)
            vals.append(m)
            idxs.append(idx)
            work = jnp.where(row == idx, -jnp.inf, work)
        exps = [jnp.exp(v - vals[0]) for v in vals]
        den = exps[0]
        for e in exps[1:]:
            den = den + e
        fill = [jnp.zeros_like(den)] * (SUBLANES - TOP_K)
        tope_ref[...] = jnp.concatenate(idxs + fill, axis=0).astype(jnp.int32)
        topg_ref[...] = jnp.concatenate([e / den for e in exps] + fill, axis=0)

    @pl.when(pl.program_id(0) < n_prompt_tiles)
    def _():
        tile(oret_p, g_p, oatt_p, x_p, pe_p)

    @pl.when(pl.program_id(0) >= n_prompt_tiles)
    def _():
        tile(oret_s, g_s, oatt_s, x_s, pe_s)


def _mixer(prompt, sample, weights):
    (gn_w, gn_b, w_out, ln1_g, ln1_b, w_router, b_router, w_pl_gate, b_pl_gate, w_pl_proj) = weights
    tp, ts = prompt[3].shape[0], sample[3].shape[0]
    tm = TOKEN_TILE
    assert tp % tm == 0 and ts % tm == 0
    npt, nst = tp // tm, ts // tm
    t_all = tp + ts
    avg = jnp.asarray(np.kron(np.eye(RET_HEADS), np.full((RET_V_DIM, RET_V_DIM), 1.0 / RET_V_DIM)), BF16)
    w_out_b = w_out.astype(BF16)
    row = lambda v: v.reshape(1, -1)
    consts = [avg, row(gn_w), row(gn_b), w_out_b[:RET_V], w_out_b[RET_V:], row(ln1_g), row(ln1_b),
              w_pl_gate.astype(BF16), row(b_pl_gate), w_pl_proj.astype(BF16), w_router.T, b_router.reshape(-1, 1)]
    p_spec = lambda a: pl.BlockSpec((tm, a.shape[1]), lambda i: (jnp.minimum(i, npt - 1), 0))
    s_spec = lambda a: pl.BlockSpec((tm, a.shape[1]), lambda i: (jnp.maximum(i - npt, 0), 0))
    c_spec = lambda a: pl.BlockSpec(a.shape, lambda i: (0, 0))
    return pl.pallas_call(
        functools.partial(_mixer_kernel, n_prompt_tiles=npt),
        grid=(npt + nst,),
        in_specs=[p_spec(a) for a in prompt] + [s_spec(a) for a in sample] + [c_spec(a) for a in consts],
        out_specs=[pl.BlockSpec((tm * ROW_TILES, LANES), lambda i: (i, 0)),
                   pl.BlockSpec((tm, D_MODEL), lambda i: (i, 0)),
                   pl.BlockSpec((SUBLANES, tm), lambda i: (0, i)),
                   pl.BlockSpec((SUBLANES, tm), lambda i: (0, i))],
        out_shape=[jax.ShapeDtypeStruct((t_all * ROW_TILES, LANES), F32),
                   jax.ShapeDtypeStruct((t_all, D_MODEL), F32),
                   jax.ShapeDtypeStruct((SUBLANES, t_all), jnp.int32),
                   jax.ShapeDtypeStruct((SUBLANES, t_all), F32)],
        compiler_params=_cparams(("arbitrary",)),
        name="mixer_out",
    )(*prompt, *sample, *consts)


def _deinterleave_kernel(w_ref, g_ref, l_ref, t_ref):
    t_ref[...] = w_ref[0].T
    half = t_ref.shape[0] // 2
    g_ref[0] = t_ref[pl.ds(0, half, stride=2), :].T.astype(BF16)
    l_ref[0] = t_ref[pl.ds(1, half, stride=2), :].T.astype(BF16)


def _deinterleave_w1(w_e1):
    e, d, f2 = w_e1.shape
    spec = pl.BlockSpec((1, LANES, f2 // 2), lambda i, c: (i, c, 0))
    return pl.pallas_call(
        _deinterleave_kernel,
        grid=(e, d // LANES),
        in_specs=[pl.BlockSpec((1, LANES, f2), lambda i, c: (i, c, 0))],
        out_specs=[spec, spec],
        out_shape=[jax.ShapeDtypeStruct((e, d, f2 // 2), BF16)] * 2,
        scratch_shapes=[pltpu.VMEM((f2, LANES), F32)],
        compiler_params=_cparams(("arbitrary", "arbitrary")),
        name="deinterleave_w1",
    )(w_e1)


def _routing_plan(top_e, n_tokens):
    a = n_tokens * TOP_K
    mb, tc = MOE_BLOCK, COMBINE_TILE
    flat_e = top_e.reshape(-1)
    order = jnp.argsort(flat_e).astype(jnp.int32)
    experts = jnp.arange(N_EXPERTS, dtype=jnp.int32)
    counts = jnp.sum((flat_e[:, None] == experts[None, :]).astype(jnp.int32), axis=0)
    padded = (counts + mb - 1) // mb * mb
    start = jnp.cumsum(counts) - counts
    pend = jnp.cumsum(padded)
    pstart = pend - padded
    n_blocks = -(-(a + N_EXPERTS * (mb - 1)) // mb)
    block_first = jnp.arange(n_blocks, dtype=jnp.int32) * mb
    block_e = jnp.minimum(jnp.sum((pend[None, :] <= block_first[:, None]).astype(jnp.int32), axis=1),
                          N_EXPERTS - 1).astype(jnp.int32)
    n_used = (pend[-1] // mb).astype(jnp.int32).reshape(1)
    e_p = jnp.repeat(block_e, mb)
    off = jnp.arange(n_blocks * mb, dtype=jnp.int32) - pstart[e_p]
    valid = off < counts[e_p]
    asg = order[jnp.clip(start[e_p] + off, 0, a - 1)]
    t, k = asg // TOP_K, asg % TOP_K
    row_real = ((t // tc) * TOP_K + k) * tc + t % tc
    row_pad = a + (pstart[e_p] - start[e_p]) + (off - counts[e_p])
    tok = jnp.where(valid, t, 0)
    row = jnp.where(valid, row_real, row_pad)
    idx = jnp.stack([tok.reshape(n_blocks, mb), row.reshape(n_blocks, mb)], axis=1)
    extra = jnp.stack([jnp.zeros((mb,), jnp.int32), n_blocks * mb + jnp.arange(mb, dtype=jnp.int32)])[None]
    return jnp.concatenate([idx, extra], axis=0), block_e, n_used


IDX_RING = 4


def _row_tile(r):
    start = r * ROW_TILES
    return pl.ds(start if isinstance(r, int) else pl.multiple_of(start, ROW_TILES), ROW_TILES)


def _expert_kernel(be_ref, nused_ref, idx_hbm, x_hbm, w1g_ref, w1l_ref, b1g_ref, b1l_ref, w2_ref, b2_ref,
                   y_hbm, xbuf, ybuf, idx_smem, gsem, ssem, isem, *, n_blocks):
    i = pl.program_id(0)
    n_used = nused_ref[0]
    mb = MOE_BLOCK
    rows = mb * ROW_TILES
    cur, other = i % 2, (i + 1) % 2

    def ring(block):
        return (block + IDX_RING) % IDX_RING

    def idx_copy(block):
        src = jnp.where(block < 0, n_blocks, jnp.minimum(block, n_blocks))
        return pltpu.make_async_copy(idx_hbm.at[src], idx_smem.at[ring(block)], isem.at[ring(block)])

    def gather_copy(r, tok, s):
        return pltpu.make_async_copy(x_hbm.at[_row_tile(tok), :], xbuf.at[s, _row_tile(r), :], gsem.at[s])

    def scatter_copy(r, row, s):
        return pltpu.make_async_copy(ybuf.at[s, _row_tile(r), :], y_hbm.at[_row_tile(row), :], ssem.at[s])

    def gathered(s):
        return pltpu.make_async_copy(x_hbm.at[pl.ds(0, rows), :], xbuf.at[s], gsem.at[s])

    def scattered(s):
        return pltpu.make_async_copy(ybuf.at[s], y_hbm.at[pl.ds(0, rows), :], ssem.at[s])

    def issue_rolled(copy, block, column, s):
        def body(r, carry):
            copy(r, idx_smem[ring(block), column, r], s).start()
            return carry
        lax.fori_loop(0, mb, body, 0)

    @pl.when(i == 0)
    def _():
        idx_copy(i).start()
        idx_copy(i - 1).start()
        idx_copy(i + 1).start()
        idx_copy(i).wait()
        idx_copy(i - 1).wait()
        issue_rolled(gather_copy, i, 0, cur)
        ybuf[other] = jnp.zeros((rows, LANES), F32)

    @pl.when(i < n_used)
    def _():
        idx_copy(i + 1).wait()
        idx_copy(i + 2).start()
        gathered(cur).wait()

        @pl.when(i >= 1)
        def _():
            scattered(cur).wait()

        for r in range(mb):
            gather_copy(r, idx_smem[ring(i + 1), 0, r], other).start(priority=r % 2)
        for r in range(mb):
            scatter_copy(r, idx_smem[ring(i - 1), 1, r], other).start(priority=r % 2)
        x = jnp.concatenate([xbuf[cur, pl.ds(s, mb, stride=ROW_TILES), :] for s in range(ROW_TILES)],
                            axis=1).astype(BF16)
        glu = jnp.minimum(_dot(x, w1g_ref[0]) + b1g_ref[0], SWIGLU_LIMIT)
        lin = jnp.clip(_dot(x, w1l_ref[0]) + b1l_ref[0], -SWIGLU_LIMIT, SWIGLU_LIMIT)
        act = glu * jax.nn.sigmoid(SWIGLU_ALPHA * glu) * (lin + 1.0)
        y = _dot(act.astype(BF16), w2_ref[0]) + b2_ref[0]
        for s in range(ROW_TILES):
            ybuf[cur, pl.ds(s, mb, stride=ROW_TILES), :] = y[:, s * LANES:(s + 1) * LANES]

    @pl.when(i == n_used - 1)
    def _():
        issue_rolled(scatter_copy, i, 1, cur)
        scattered(other).wait()
        scattered(cur).wait()
        gathered(other).wait()
        idx_copy(i + 2).wait()

    @pl.when(i >= n_used)
    def _():
        ybuf[cur] = jnp.zeros((rows, LANES), F32)
        fill = pltpu.make_async_copy(ybuf.at[cur], y_hbm.at[pl.ds(pl.multiple_of(i * rows, rows), rows), :],
                                     ssem.at[cur])
        fill.start()
        fill.wait()


def _experts(x1r, idx, block_e, n_used, w1g, w1l, b_e1, w_e2, b_e2):
    n_blocks = idx.shape[0] - 1
    mb = MOE_BLOCK
    rows = mb * ROW_TILES
    b1g, b1l = b_e1[:, None, 0::2], b_e1[:, None, 1::2]
    w2 = w_e2.astype(BF16)
    b2 = b_e2[:, None, :]
    wspec = pl.BlockSpec((1, D_MODEL, D_EXPERT), lambda i, be, nu: (be[i], 0, 0))
    bspec = pl.BlockSpec((1, 1, D_EXPERT), lambda i, be, nu: (be[i], 0, 0))
    return pl.pallas_call(
        functools.partial(_expert_kernel, n_blocks=n_blocks),
        grid_spec=pltpu.PrefetchScalarGridSpec(
            num_scalar_prefetch=2,
            grid=(n_blocks,),
            in_specs=[pl.BlockSpec(memory_space=pl.ANY),
                      pl.BlockSpec(memory_space=pl.ANY),
                      wspec, wspec, bspec, bspec,
                      pl.BlockSpec((1, D_EXPERT, D_MODEL), lambda i, be, nu: (be[i], 0, 0)),
                      pl.BlockSpec((1, 1, D_MODEL), lambda i, be, nu: (be[i], 0, 0))],
            out_specs=pl.BlockSpec(memory_space=pl.ANY),
            scratch_shapes=[pltpu.VMEM((2, rows, LANES), F32),
                            pltpu.VMEM((2, rows, LANES), F32),
                            pltpu.SMEM((IDX_RING, 2, mb), jnp.int32),
                            pltpu.SemaphoreType.DMA((2,)),
                            pltpu.SemaphoreType.DMA((2,)),
                            pltpu.SemaphoreType.DMA((IDX_RING,))]),
        out_shape=jax.ShapeDtypeStruct(((n_blocks + 1) * rows, LANES), F32),
        compiler_params=_cparams(("arbitrary",)),
        name="experts",
    )(block_e, n_used, idx, x1r, w1g, w1l, b1g, b1l, w2, b2)


def _combine_kernel(y_ref, gate_ref, resid_ref, g_ref, b_ref, op_ref, os_ref, *, n_prompt_tiles):
    tm = COMBINE_TILE
    z = resid_ref[...]
    gate = gate_ref[...]
    for kk in range(TOP_K):
        y_k = jnp.concatenate(
            [y_ref[pl.ds(kk * tm * ROW_TILES + s, tm, stride=ROW_TILES), :] for s in range(ROW_TILES)], axis=1)
        z = z + y_k * gate[:, kk:kk + 1]
    out = _layer_norm(z, g_ref[...], b_ref[...])

    @pl.when(pl.program_id(0) < n_prompt_tiles)
    def _():
        op_ref[...] = out

    @pl.when(pl.program_id(0) >= n_prompt_tiles)
    def _():
        os_ref[...] = out


def _combine(ys, top_g, resid, ln2_g, ln2_b, n_prompt):
    t_all = resid.shape[0]
    tm = COMBINE_TILE
    assert n_prompt % tm == 0 and (t_all - n_prompt) % tm == 0
    npt, n_tiles = n_prompt // tm, t_all // tm
    return pl.pallas_call(
        functools.partial(_combine_kernel, n_prompt_tiles=npt),
        grid=(n_tiles,),
        in_specs=[pl.BlockSpec((tm * TOP_K * ROW_TILES, LANES), lambda i: (i, 0)),
                  pl.BlockSpec((tm, top_g.shape[1]), lambda i: (i, 0)),
                  pl.BlockSpec((tm, D_MODEL), lambda i: (i, 0)),
                  pl.BlockSpec((1, D_MODEL), lambda i: (0, 0)),
                  pl.BlockSpec((1, D_MODEL), lambda i: (0, 0))],
        out_specs=[pl.BlockSpec((tm, D_MODEL), lambda i: (jnp.minimum(i, npt - 1), 0)),
                   pl.BlockSpec((tm, D_MODEL), lambda i: (jnp.maximum(i - npt, 0), 0))],
        out_shape=[jax.ShapeDtypeStruct((n_prompt, D_MODEL), F32),
                   jax.ShapeDtypeStruct((t_all - n_prompt, D_MODEL), F32)],
        compiler_params=_cparams(("arbitrary",)),
        name="combine_ln2",
    )(ys, top_g, resid, ln2_g.reshape(1, -1), ln2_b.reshape(1, -1))


def kernel(x_prompt, x_sample, cache_att_k, cache_att_v, state_ret, p_prompt, p_sample, w_in, ret_gn_w, ret_gn_b, w_out, ln1_g, ln1_b, w_router, b_router, w_e1, b_e1, w_e2, b_e2, w_pl_gate, b_pl_gate, w_pl_proj, ln2_g, ln2_b):
    assert w_in.shape[0] == DEPTH == 1
    B, S, _ = x_prompt.shape
    DB, T, _ = x_sample.shape
    H, D = ATT_HEADS, ATT_HEAD_DIM
    l = 0
    w_in_b = w_in[l].astype(BF16)

    pos_p = jnp.arange(S, dtype=F32)
    pd, pt = _in_projection(x_prompt, w_in_b, pos_p, PROJ_NAMES, ("k_a", "v_a"), TOKEN_TILE, False)
    zero_state = jnp.zeros((B, 2, RET_QK // 2, RET_V // 2), F32)
    o_ret_p, st_p = _retention(pd["q_r"], pd["k_r"], pd["v_r"], zero_state, RET_CHUNK, RET_CHUNK)
    o_att_p = _attention_prompt(pd["q_a"], pd["k_a"], pd["v_a"])
    keep = min(MAX_WINDOW, S)
    as_cache = lambda a: a.reshape(B, H, D, S).transpose(0, 3, 1, 2)[None, :, S - keep:]
    new_k_p, new_v_p = as_cache(pt["k_a"]), as_cache(pt["v_a"])
    new_st_p = _blockdiag_to_state(st_p)[None]

    ts = DB * T
    by_step = lambda a: a.transpose(1, 0, 2).reshape(ts, a.shape[-1])
    xs = by_step(x_sample)
    pos_s = jnp.repeat(PAST_LEN + jnp.arange(T, dtype=F32), DB)
    sd, st = _in_projection(xs[None], w_in_b, pos_s, ("g_r", "q_a", "k_a", "v_a"),
                            ("q_r", "k_r", "v_r", "k_a", "v_a"), DB, True)
    state_t = jnp.transpose(state_ret[l], (1, 2, 3, 0))
    o_ret_t, state_new = _retention_sample(st["q_r"], st["k_r"], st["v_r"], state_t)
    o_ret_s = o_ret_t.transpose(0, 2, 1).reshape(ts, RET_V)
    by_batch = lambda a: a[0].reshape(T, DB, -1).transpose(1, 0, 2)
    cache_t = lambda c: jnp.transpose(c[l], (0, 2, 3, 1))
    o_att_s = _attention_sample(by_batch(sd["q_a"]), by_batch(sd["k_a"]), by_batch(sd["v_a"]),
                                cache_t(cache_att_k), cache_t(cache_att_v))
    as_new = lambda a: a.reshape(T, H, D, DB).transpose(3, 0, 1, 2)[None]
    new_k_s, new_v_s = as_new(st["k_a"]), as_new(st["v_a"])
    new_st_s = jnp.transpose(state_new, (3, 0, 1, 2)).astype(state_ret.dtype)[None]

    tp = B * S
    flat2 = lambda a: a.reshape(-1, a.shape[-1])
    prompt = (flat2(o_ret_p), flat2(pd["g_r"]), flat2(o_att_p), flat2(x_prompt), flat2(p_prompt[l]))
    sample = (o_ret_s, sd["g_r"][0], by_step(o_att_s), xs, by_step(p_sample[l]))
    x1r, resid, tope, topg = _mixer(prompt, sample, (ret_gn_w[l], ret_gn_b[l], w_out[l], ln1_g[l], ln1_b[l],
                                                     w_router[l], b_router[l], w_pl_gate[l], b_pl_gate[l],
                                                     w_pl_proj[l]))

    t_all = tp + ts
    idx, block_e, n_used = _routing_plan(tope[:TOP_K].T, t_all)
    w1g, w1l = _deinterleave_w1(w_e1[l])
    ys = _experts(x1r, idx, block_e, n_used, w1g, w1l, b_e1[l], w_e2[l], b_e2[l])
    y_p, y_s = _combine(ys, topg.T, resid, ln2_g[l], ln2_b[l], tp)
    return (y_p.reshape(B, S, D_MODEL), y_s.reshape(T, DB, D_MODEL).transpose(1, 0, 2),
            new_k_p, new_v_p, new_st_p, new_k_s, new_v_s, new_st_s)
```

```python
import functools

import numpy as np
import jax
import jax.numpy as jnp
from jax import lax
from jax.experimental import pallas as pl
from jax.experimental.pallas import tpu as pltpu

F32 = jnp.float32
BF16 = jnp.bfloat16

D_MODEL = 1024
DEPTH = 1
PAST_LEN = 8192
RET_HEADS = 8
RET_QK_DIM = 32
RET_V_DIM = 64
RET_CHUNK = 128
RET_ROPE_BASE = 10000.0
ATT_HEADS = 8
ATT_HEAD_DIM = 64
ROT_DIMS = ATT_HEAD_DIM // 4
ROPE_THETA = 500000.0
DILATED_PATTERNS = ((128, 1), (512, 4), (2048, 16))
MAX_WINDOW = 2048
RET_QK = RET_HEADS * RET_QK_DIM
RET_V = RET_HEADS * RET_V_DIM
ATT_W = ATT_HEADS * ATT_HEAD_DIM
IN_COLS = RET_QK * 2 + RET_V * 2 + ATT_W * 3
N_EXPERTS = 32
TOP_K = 4
D_EXPERT = D_MODEL
SWIGLU_ALPHA = 1.702
SWIGLU_LIMIT = 7.0
PLE_DIM = 256
LN_EPS = 1e-5
GN_EPS = 1e-6
DEEPNORM_ALPHA = (2 * DEPTH) ** 0.25
NEG_INF = -1e30

LANES = 128
SUBLANES = 8
ROW_TILES = D_MODEL // LANES
VMEM_LIMIT = 56 * 1024 * 1024

TOKEN_TILE = 512
MOE_BLOCK = 512
COMBINE_TILE = 256
ATT_BLOCK = 128
ATT_GROUP = 16


def _dot(a, b):
    return jnp.dot(a, b, preferred_element_type=F32)


def _dot_nt(a, b):
    return lax.dot_general(a, b, (((1,), (1,)), ((), ())), preferred_element_type=F32)


def _dot_tn(a, b):
    return lax.dot_general(a, b, (((0,), (0,)), ((), ())), preferred_element_type=F32)


def _cparams(sem):
    return pltpu.CompilerParams(dimension_semantics=sem, vmem_limit_bytes=VMEM_LIMIT)


def _rotary_tables(pos, inv_freq, head_dim, n_heads):
    half = inv_freq.shape[0]
    ang = pos.astype(F32)[:, None] * inv_freq[None, :]
    cos, sin = jnp.cos(ang), jnp.sin(ang)
    rest = head_dim - 2 * half
    n = pos.shape[0]
    cos_h = jnp.concatenate([cos, cos, jnp.ones((n, rest), F32)], axis=1)
    sin_h = jnp.concatenate([-sin, sin, jnp.zeros((n, rest), F32)], axis=1)
    return jnp.tile(cos_h, (1, n_heads)), jnp.tile(sin_h, (1, n_heads))


def _ret_inv_freq():
    return 1.0 / (RET_ROPE_BASE ** jnp.linspace(0.0, 1.0, RET_QK_DIM // 2, dtype=F32))


def _att_inv_freq():
    return ROPE_THETA ** (-jnp.arange(0, ROT_DIMS, 2, dtype=F32) / ROT_DIMS)


def _ret_log_decay():
    return jnp.log(1.0 - 2.0 ** (-5.0 - jnp.arange(RET_HEADS, dtype=F32)))


def _retention_tables(c_true, c_pad):
    lg = _ret_log_decay()
    idx = jnp.arange(c_pad, dtype=F32)
    live = idx < c_true
    diff = idx[:, None] - idx[None, :]
    decay = jnp.where(diff[None] >= 0, jnp.exp(jnp.maximum(diff, 0.0)[None] * lg[:, None, None]), 0.0)
    decay = jnp.where(live[None, :, None] & live[None, None, :], decay, 0.0)
    cross = jnp.exp((idx + 1.0)[:, None] * lg[None, :])
    cross = jnp.where(live[:, None], cross, 0.0)
    kdec = jnp.exp((c_true - 1.0 - idx)[:, None] * lg[None, :])
    kdec = jnp.where(live[:, None], kdec, 0.0)
    sdec = jnp.exp(c_true * lg)
    hh = RET_HEADS // 2
    decay = decay.reshape(2, hh, c_pad, c_pad)
    cross = jnp.repeat(cross, RET_V_DIM, axis=1).reshape(c_pad, 2, hh * RET_V_DIM).transpose(1, 0, 2)
    kdec = jnp.repeat(kdec, RET_QK_DIM, axis=1).reshape(c_pad, 2, hh * RET_QK_DIM).transpose(1, 0, 2)
    sdec = jnp.broadcast_to(jnp.repeat(sdec, RET_V_DIM)[None, :], (hh * RET_QK_DIM, RET_V)) \
        .reshape(hh * RET_QK_DIM, 2, hh * RET_V_DIM).transpose(1, 0, 2)
    return decay, cross, kdec, sdec


def _rotate(h, cos, sin, half, period):
    outs = []
    for j in range(h.shape[1] // LANES):
        blk = h[:, j * LANES:(j + 1) * LANES]
        lane = lax.broadcasted_iota(jnp.int32, blk.shape, 1)
        first = (lane % period) < half
        partner = jnp.where(first, pltpu.roll(blk, LANES - half, 1), pltpu.roll(blk, half, 1))
        outs.append(blk * cos[:, j * LANES:(j + 1) * LANES] + partner * sin[:, j * LANES:(j + 1) * LANES])
    return jnp.concatenate(outs, axis=1)


PROJ_NAMES = ("q_r", "k_r", "v_r", "g_r", "q_a", "k_a", "v_a")
PROJ_WIDTHS = dict(zip(PROJ_NAMES, (RET_QK, RET_QK, RET_V, RET_V, ATT_W, ATT_W, ATT_W)))


def _inproj_kernel(x_ref, w_ref, cr_ref, sr_ref, ca_ref, sa_ref, *out_refs, direct, transposed):
    x = x_ref[0].astype(BF16)
    cr, sr, ca, sa = cr_ref[...], sr_ref[...], ca_ref[...], sa_ref[...]
    finish = {
        "q_r": lambda h: _rotate(h, cr, sr, RET_QK_DIM // 2, RET_QK_DIM),
        "k_r": lambda h: _rotate(h, cr, sr, RET_QK_DIM // 2, RET_QK_DIM) * (RET_QK_DIM ** -0.5),
        "q_a": lambda h: _rotate(h, ca, sa, ROT_DIMS // 2, ATT_HEAD_DIM) * (ATT_HEAD_DIM ** -0.5),
        "k_a": lambda h: _rotate(h, ca, sa, ROT_DIMS // 2, ATT_HEAD_DIM),
    }
    refs = dict(zip([("d", n) for n in direct] + [("t", n) for n in transposed], out_refs))
    c = 0
    for name in PROJ_NAMES:
        width = PROJ_WIDTHS[name]
        if name in direct or name in transposed:
            h = _dot(x, w_ref[:, c:c + width])
            h = finish.get(name, lambda v: v)(h)
            if name in direct:
                refs[("d", name)][0] = h
            if name in transposed:
                refs[("t", name)][0] = h.T
        c += width


def _in_projection(x, w_in_b, pos, direct, transposed, tm, tile_major):
    B, S, _ = x.shape
    assert S % tm == 0 and (B == 1 or not tile_major)
    cr, sr = _rotary_tables(pos, _ret_inv_freq(), RET_QK_DIM, RET_HEADS)
    ca, sa = _rotary_tables(pos, _att_inv_freq(), ATT_HEAD_DIM, ATT_HEADS)
    tab = lambda w: pl.BlockSpec((tm, w), lambda s, b: (s, 0))
    specs = [pl.BlockSpec((1, tm, PROJ_WIDTHS[n]), lambda s, b: (b, s, 0)) for n in direct]
    shapes = [jax.ShapeDtypeStruct((B, S, PROJ_WIDTHS[n]), F32) for n in direct]
    for n in transposed:
        w = PROJ_WIDTHS[n]
        if tile_major:
            specs.append(pl.BlockSpec((1, w, tm), lambda s, b: (s, 0, 0)))
            shapes.append(jax.ShapeDtypeStruct((S // tm, w, tm), F32))
        else:
            specs.append(pl.BlockSpec((1, w, tm), lambda s, b: (b, 0, s)))
            shapes.append(jax.ShapeDtypeStruct((B, w, S), F32))
    outs = pl.pallas_call(
        functools.partial(_inproj_kernel, direct=direct, transposed=transposed),
        grid=(S // tm, B),
        in_specs=[pl.BlockSpec((1, tm, D_MODEL), lambda s, b: (b, s, 0)),
                  pl.BlockSpec((D_MODEL, IN_COLS), lambda s, b: (0, 0)),
                  tab(RET_QK), tab(RET_QK), tab(ATT_W), tab(ATT_W)],
        out_specs=specs,
        out_shape=shapes,
        compiler_params=_cparams(("arbitrary", "arbitrary")),
        name="in_projection",
    )(x, w_in_b, cr, sr, ca, sa)
    return dict(zip(direct, outs[:len(direct)])), dict(zip(transposed, outs[len(direct):]))


def _retention_kernel(q_ref, k_ref, v_ref, s0_ref, dec_ref, cross_ref, kdec_ref, sdec_ref,
                      o_ref, so_ref, *, chunk, n_chunks):
    hh = RET_HEADS // 2
    qk_w, v_w = hh * RET_QK_DIM, hh * RET_V_DIM
    lane_q = lax.broadcasted_iota(jnp.int32, (chunk, qk_w), 1) // RET_QK_DIM
    lane_v = lax.broadcasted_iota(jnp.int32, (chunk, v_w), 1) // RET_V_DIM
    blockdiag = (lax.broadcasted_iota(jnp.int32, (qk_w, v_w), 0) // RET_QK_DIM
                 == lax.broadcasted_iota(jnp.int32, (qk_w, v_w), 1) // RET_V_DIM)
    state = s0_ref[0, 0]
    cross, kdec, sdec = cross_ref[0], kdec_ref[0], sdec_ref[0]
    for ci in range(n_chunks):
        rows = pl.ds(ci * chunk, chunk)
        q, k, v = q_ref[0, rows, :], k_ref[0, rows, :], v_ref[0, rows, :]
        kb, vb = k.astype(BF16), v.astype(BF16)
        o = _dot(q.astype(BF16), state.astype(BF16)) * cross
        for h in range(hh):
            qm = jnp.where(lane_q == h, q, 0.0).astype(BF16)
            inner = _dot_nt(qm, kb) * dec_ref[0, h]
            o = jnp.where(lane_v == h, o + _dot(inner.astype(BF16), vb), o)
        o_ref[0, rows, :] = o
        upd = _dot_tn((k * kdec).astype(BF16), vb)
        state = state * sdec + jnp.where(blockdiag, upd, 0.0)
    so_ref[0, 0] = state


def _retention(q, k, v, state_bd, c_true, c_pad):
    B, S, _ = q.shape
    hh = RET_HEADS // 2
    qk_w, v_w = hh * RET_QK_DIM, hh * RET_V_DIM
    decay, cross, kdec, sdec = _retention_tables(c_true, c_pad)
    kern = functools.partial(_retention_kernel, chunk=c_pad, n_chunks=S // c_pad)
    return pl.pallas_call(
        kern,
        grid=(B, 2),
        in_specs=[pl.BlockSpec((1, S, qk_w), lambda b, g: (b, 0, g)),
                  pl.BlockSpec((1, S, qk_w), lambda b, g: (b, 0, g)),
                  pl.BlockSpec((1, S, v_w), lambda b, g: (b, 0, g)),
                  pl.BlockSpec((1, 1, qk_w, v_w), lambda b, g: (b, g, 0, 0)),
                  pl.BlockSpec((1, hh, c_pad, c_pad), lambda b, g: (g, 0, 0, 0)),
                  pl.BlockSpec((1, c_pad, v_w), lambda b, g: (g, 0, 0)),
                  pl.BlockSpec((1, c_pad, qk_w), lambda b, g: (g, 0, 0)),
                  pl.BlockSpec((1, qk_w, v_w), lambda b, g: (g, 0, 0))],
        out_specs=[pl.BlockSpec((1, S, v_w), lambda b, g: (b, 0, g)),
                   pl.BlockSpec((1, 1, qk_w, v_w), lambda b, g: (b, g, 0, 0))],
        out_shape=[jax.ShapeDtypeStruct((B, S, RET_V), F32),
                   jax.ShapeDtypeStruct((B, 2, qk_w, v_w), F32)],
        compiler_params=_cparams(("arbitrary", "arbitrary")),
        name="retention",
    )(q, k, v, state_bd, decay, cross, kdec, sdec)


def _blockdiag_to_state(bd):
    B = bd.shape[0]
    hh = RET_HEADS // 2
    s = bd.reshape(B, 2, hh, RET_QK_DIM, hh, RET_V_DIM)
    s = jnp.stack([s[:, :, h, :, h, :] for h in range(hh)], axis=2)
    return s.reshape(B, RET_HEADS, RET_QK_DIM, RET_V_DIM)


def _retention_sample_kernel(gam_ref, q_ref, k_ref, v_ref, s_ref, o_ref, so_ref, *, steps):
    h = pl.program_id(0)
    gam = [gam_ref[h, p] for p in range(steps + 1)]
    q = [q_ref[t] for t in range(steps)]
    k = [k_ref[t] for t in range(steps)]
    v = [v_ref[t] for t in range(steps)]
    o = [jnp.zeros(v[0].shape, F32) for _ in range(steps)]
    for d in range(RET_QK_DIM):
        s_d = s_ref[0, d]
        new = s_d * gam[steps]
        for t in range(steps):
            o[t] = o[t] + q[t][d:d + 1, :] * s_d
            new = new + (k[t][d:d + 1, :] * gam[steps - 1 - t]) * v[t]
        so_ref[0, d] = new
    for i in range(steps):
        o[i] = o[i] * gam[i + 1]
        for j in range(i + 1):
            qk = jnp.sum(q[i] * k[j], axis=0, keepdims=True) * gam[i - j]
            o[i] = o[i] + qk * v[j]
        o_ref[i] = o[i]


def _retention_sample(q_t, k_t, v_t, state):
    steps, _, nb = q_t.shape
    gam = jnp.exp(_ret_log_decay()[:, None] * jnp.arange(steps + 1, dtype=F32)[None, :])
    qk_spec = pl.BlockSpec((steps, RET_QK_DIM, nb), lambda h: (0, h, 0))
    v_spec = pl.BlockSpec((steps, RET_V_DIM, nb), lambda h: (0, h, 0))
    s_spec = pl.BlockSpec((1, RET_QK_DIM, RET_V_DIM, nb), lambda h: (h, 0, 0, 0))
    return pl.pallas_call(
        functools.partial(_retention_sample_kernel, steps=steps),
        grid=(RET_HEADS,),
        in_specs=[pl.BlockSpec(memory_space=pltpu.SMEM), qk_spec, qk_spec, v_spec, s_spec],
        out_specs=[v_spec, s_spec],
        out_shape=[jax.ShapeDtypeStruct(v_t.shape, F32), jax.ShapeDtypeStruct(state.shape, F32)],
        compiler_params=_cparams(("arbitrary",)),
        name="retention_sample",
    )(gam, q_t, k_t, v_t, state)


def _attn_prompt_kernel(q_ref, k_ref, v_ref, o_ref, o_ref_p, lse_ref, *, seq):
    n = ATT_BLOCK
    head0 = lax.broadcasted_iota(jnp.int32, (n, LANES), 1) < ATT_HEAD_DIM
    head0_kv = lax.broadcasted_iota(jnp.int32, (2 * n, LANES), 1) < ATT_HEAD_DIM
    a = lax.broadcasted_iota(jnp.int32, (n, 2 * n), 0)
    j = lax.broadcasted_iota(jnp.int32, (n, 2 * n), 1)
    bias_pc = jnp.where(((j < n) & (j >= a)) | ((j >= n) & (j - n <= a)), 0.0, NEG_INF)
    bias_c = bias_pc[:, n:]
    bias_first = jnp.where(j < n, NEG_INF, bias_pc)

    def rows(start, d):
        if d > 1:
            return pl.ds(start, n, stride=d)
        return pl.ds(start if isinstance(start, int) else pl.multiple_of(start, n), n)

    def blocks(p_idx, d, specs):
        loaded = []
        for cur, prev, has_prev in specs:
            q = q_ref[0, rows(cur, d), :]
            if has_prev is False:
                kk, vv = k_ref[0, rows(cur, d), :], v_ref[0, rows(cur, d), :]
                bias, hkv = bias_c, head0
            else:
                kk = jnp.concatenate([k_ref[0, rows(prev, d), :], k_ref[0, rows(cur, d), :]], axis=0)
                vv = jnp.concatenate([v_ref[0, rows(prev, d), :], v_ref[0, rows(cur, d), :]], axis=0)
                bias = bias_pc if has_prev is True else jnp.where(has_prev, bias_pc, bias_first)
                hkv = head0_kv
            loaded.append((q, kk.astype(BF16), vv, bias, hkv))
        scores = []
        for q, kb, vv, bias, hkv in loaded:
            for h in range(2):
                mine = head0 if h == 0 else jnp.logical_not(head0)
                scores.append(_dot_nt(jnp.where(mine, q, 0.0).astype(BF16), kb) + bias)
        probs, maxes = [], []
        for s in scores:
            m = jnp.max(s, axis=1, keepdims=True)
            probs.append(jnp.exp(s - m).astype(BF16))
            maxes.append(m)
        pvs = []
        for u, (q, kb, vv, bias, hkv) in enumerate(loaded):
            for h in range(2):
                mine_kv = hkv if h == 0 else jnp.logical_not(hkv)
                pvs.append(_dot(probs[2 * u + h], jnp.where(mine_kv, vv, 1.0).astype(BF16)))
        for u, (cur, prev, has_prev) in enumerate(specs):
            pv0, pv1 = pvs[2 * u], pvs[2 * u + 1]
            acc_u = jnp.where(head0, pv0, pv1)
            l_u = pltpu.roll(jnp.where(head0, pv1, pv0), ATT_HEAD_DIM, 1)
            m_u = jnp.where(head0, maxes[2 * u], maxes[2 * u + 1])
            o_ref_p[p_idx, rows(cur, d), :] = acc_u / l_u
            lse_ref[p_idx, rows(cur, d), :] = m_u + jnp.log(l_u)

    G = ATT_GROUP
    for p_idx, (window, d) in enumerate(DILATED_PATTERNS):
        nb = (seq // d) // n
        assert (d * nb) % G == 0 and (nb % G == 0 or G % nb == 0)

        def group(g, carry, p_idx=p_idx, d=d, nb=nb):
            specs = []
            for jj in range(G):
                if nb > G:
                    first = g * G
                    r = 0 if d == 1 else first // nb
                    c = first % nb + jj
                    specs.append((c * (n * d) + r, jnp.maximum(c - 1, 0) * (n * d) + r,
                                  True if jj > 0 else c > 0))
                else:
                    r = g * (G // nb) + jj // nb
                    c = jj % nb
                    specs.append((c * (n * d) + r, (c - 1) * (n * d) + r, c > 0))
            blocks(p_idx, d, specs)
            return carry

        lax.fori_loop(0, d * nb // G, group, 0)

    def merge(c, carry):
        rs = pl.ds(pl.multiple_of(c * n, n), n)
        lse = [lse_ref[p, rs, :] for p in range(len(DILATED_PATTERNS))]
        top = functools.reduce(jnp.maximum, lse)
        w = [jnp.exp(x - top) for x in lse]
        num = functools.reduce(lambda x, y: x + y, [w[p] * o_ref_p[p, rs, :] for p in range(len(w))])
        o_ref[0, rs, :] = num / functools.reduce(lambda x, y: x + y, w)
        return carry

    lax.fori_loop(0, seq // n, merge, 0)


def _attention_prompt(q, k, v):
    B, S, _ = q.shape
    spec = pl.BlockSpec((1, S, LANES), lambda b, g: (b, 0, g))
    return pl.pallas_call(
        functools.partial(_attn_prompt_kernel, seq=S),
        grid=(B, ATT_W // LANES),
        in_specs=[spec, spec, spec],
        out_specs=spec,
        out_shape=jax.ShapeDtypeStruct((B, S, ATT_W), F32),
        scratch_shapes=[pltpu.VMEM((len(DILATED_PATTERNS), S, LANES), F32)] * 2,
        compiler_params=_cparams(("arbitrary", "arbitrary")),
        name="attention_prompt",
    )(q, k, v)


def _attn_sample_kernel(q_ref, kn_ref, vn_ref, kc_ref, vc_ref, cc_ref, cn_ref, o_ref, *, steps):
    q = q_ref[0].astype(BF16)
    width = q.shape[1]
    kc = kc_ref[0].reshape(width, -1).astype(BF16)
    vc = vc_ref[0].reshape(width, -1).astype(BF16)
    cnt_c, cnt_n = cc_ref[...], cn_ref[...]
    s_c = jnp.where(cnt_c > 0.0, _dot(q, kc), NEG_INF)
    s_n = jnp.where(cnt_n > 0.0, _dot_nt(q, kn_ref[0].astype(BF16)), NEG_INF)
    m = jnp.maximum(s_c.max(axis=1, keepdims=True), s_n.max(axis=1, keepdims=True))
    p_c = cnt_c * jnp.exp(s_c - m)
    p_n = cnt_n * jnp.exp(s_n - m)
    den = p_c.sum(axis=1, keepdims=True) + p_n.sum(axis=1, keepdims=True)
    num = _dot_nt(p_c.astype(BF16), vc) + _dot(p_n.astype(BF16), vn_ref[0].astype(BF16))
    rows = lax.broadcasted_iota(jnp.int32, num.shape, 0)
    lanes = lax.broadcasted_iota(jnp.int32, num.shape, 1)
    out = jnp.where(rows % ATT_HEADS == lanes // ATT_HEAD_DIM, num / den, 0.0)
    for t in range(steps):
        o_ref[0, t:t + 1, :] = jnp.sum(out[t * ATT_HEADS:(t + 1) * ATT_HEADS], axis=0, keepdims=True)


def _sample_key_counts(dec_seq, buf, new_pad):
    t_q = np.repeat(np.arange(dec_seq), ATT_HEADS)[:, None]
    back = buf + t_q - np.arange(buf)[None, :]
    cnt_c = np.zeros(back.shape, np.float32)
    for w, d in DILATED_PATTERNS:
        cnt_c += ((back % d == 0) & (back >= d) & (back <= w)).astype(np.float32)
    back = t_q - np.arange(new_pad)[None, :]
    cnt_n = np.zeros(back.shape, np.float32)
    for w, d in DILATED_PATTERNS:
        cnt_n += ((back >= 0) & (back % d == 0) & (back <= w)).astype(np.float32)
    cnt_n *= (np.arange(new_pad) < dec_seq)[None, :]
    return jnp.asarray(cnt_c), jnp.asarray(cnt_n)


def _attention_sample(q, k_new, v_new, cache_k_t, cache_v_t):
    B, T, width = q.shape
    _, H, D, buf = cache_k_t.shape
    new_pad = SUBLANES
    cnt_c, cnt_n = _sample_key_counts(T, buf, new_pad)
    q_rows = (q.reshape(B, T, 1, H, D) * jnp.eye(H, dtype=F32)[None, None, :, :, None]).reshape(B, T * H, width)
    pad = lambda x: jnp.pad(x, ((0, 0), (0, new_pad - T), (0, 0)))
    new_spec = pl.BlockSpec((1, new_pad, width), lambda b: (b, 0, 0))
    cache_spec = pl.BlockSpec((1, H, D, buf), lambda b: (b, 0, 0, 0))
    const = lambda x: pl.BlockSpec(x.shape, lambda b: (0, 0))
    return pl.pallas_call(
        functools.partial(_attn_sample_kernel, steps=T),
        grid=(B,),
        in_specs=[pl.BlockSpec((1, T * H, width), lambda b: (b, 0, 0)), new_spec, new_spec,
                  cache_spec, cache_spec, const(cnt_c), const(cnt_n)],
        out_specs=pl.BlockSpec((1, T, width), lambda b: (b, 0, 0)),
        out_shape=jax.ShapeDtypeStruct((B, T, width), F32),
        compiler_params=_cparams(("arbitrary",)),
        name="attention_sample",
    )(q_rows, pad(k_new), pad(v_new), cache_k_t, cache_v_t, cnt_c, cnt_n)


def _layer_norm(y, g, b):
    mu = jnp.mean(y, axis=-1, keepdims=True)
    d = y - mu
    var = jnp.mean(d * d, axis=-1, keepdims=True)
    return d * lax.rsqrt(var + LN_EPS) * g + b


def _mixer_kernel(oret_p, g_p, oatt_p, x_p, pe_p, oret_s, g_s, oatt_s, x_s, pe_s,
                  avg_ref, gnw_ref, gnb_ref, wor_ref, woa_ref, ln1g_ref, ln1b_ref,
                  wg_ref, bg_ref, wp_ref, wr_ref, br_ref,
                  x1r_ref, resid_ref, tope_ref, topg_ref, *, n_prompt_tiles):
    def tile(oret_ref, g_ref, oatt_ref, x_ref, pe_ref):
        o_ret, g, o_att, x, pe = oret_ref[...], g_ref[...], oatt_ref[...], x_ref[...], pe_ref[...]
        avg = avg_ref[...]

        def head_mean(z):
            hi = z.astype(BF16)
            lo = (z - hi.astype(F32)).astype(BF16)
            return _dot(hi, avg) + _dot(lo, avg)

        d = o_ret - head_mean(o_ret)
        o_n = d * lax.rsqrt(head_mean(d * d) + GN_EPS) * gnw_ref[...] + gnb_ref[...]
        ret = g * jax.nn.sigmoid(g) * o_n
        mix = _dot(ret.astype(BF16), wor_ref[...]) + _dot(o_att.astype(BF16), woa_ref[...])
        x1 = _layer_norm(DEEPNORM_ALPHA * x + mix, ln1g_ref[...], ln1b_ref[...])
        x1b = x1.astype(BF16)
        ple = jax.nn.sigmoid(_dot(x1b, wg_ref[...]) + bg_ref[...]) * _dot(pe.astype(BF16), wp_ref[...])
        resid_ref[...] = DEEPNORM_ALPHA * x1 + ple
        tm = x1.shape[0]
        for s in range(ROW_TILES):
            x1r_ref[pl.ds(s, tm, stride=ROW_TILES), :] = x1[:, s * LANES:(s + 1) * LANES]

        logits = lax.dot_general(wr_ref[...], x1, (((1,), (1,)), ((), ())), preferred_element_type=F32,
                                 precision=lax.Precision.HIGHEST) + br_ref[...]
        row = lax.broadcasted_iota(jnp.int32, logits.shape, 0).astype(F32)
        work = logits
        vals, idxs = [], []
        for _ in range(TOP_K):
            m = jnp.max(work, axis=0, keepdims=True)
            idx = jnp.min(jnp.where(work == m, row, float(N_EXPERTS)), axis=0, keepdims=True)
            vals.append(m)
            idxs.append(idx)
            work = jnp.where(row == idx, -jnp.inf, work)
        exps = [jnp.exp(v - vals[0]) for v in vals]
        den = exps[0]
        for e in exps[1:]:
            den = den + e
        fill = [jnp.zeros_like(den)] * (SUBLANES - TOP_K)
        tope_ref[...] = jnp.concatenate(idxs + fill, axis=0).astype(jnp.int32)
        topg_ref[...] = jnp.concatenate([e / den for e in exps] + fill, axis=0)

    @pl.when(pl.program_id(0) < n_prompt_tiles)
    def _():
        tile(oret_p, g_p, oatt_p, x_p, pe_p)

    @pl.when(pl.program_id(0) >= n_prompt_tiles)
    def _():
        tile(oret_s, g_s, oatt_s, x_s, pe_s)


def _mixer(prompt, sample, weights):
    (gn_w, gn_b, w_out, ln1_g, ln1_b, w_router, b_router, w_pl_gate, b_pl_gate, w_pl_proj) = weights
    tp, ts = prompt[3].shape[0], sample[3].shape[0]
    tm = TOKEN_TILE
    assert tp % tm == 0 and ts % tm == 0
    npt, nst = tp // tm, ts // tm
    t_all = tp + ts
    avg = jnp.asarray(np.kron(np.eye(RET_HEADS), np.full((RET_V_DIM, RET_V_DIM), 1.0 / RET_V_DIM)), BF16)
    w_out_b = w_out.astype(BF16)
    row = lambda v: v.reshape(1, -1)
    consts = [avg, row(gn_w), row(gn_b), w_out_b[:RET_V], w_out_b[RET_V:], row(ln1_g), row(ln1_b),
              w_pl_gate.astype(BF16), row(b_pl_gate), w_pl_proj.astype(BF16), w_router.T, b_router.reshape(-1, 1)]
    p_spec = lambda a: pl.BlockSpec((tm, a.shape[1]), lambda i: (jnp.minimum(i, npt - 1), 0))
    s_spec = lambda a: pl.BlockSpec((tm, a.shape[1]), lambda i: (jnp.maximum(i - npt, 0), 0))
    c_spec = lambda a: pl.BlockSpec(a.shape, lambda i: (0, 0))
    return pl.pallas_call(
        functools.partial(_mixer_kernel, n_prompt_tiles=npt),
        grid=(npt + nst,),
        in_specs=[p_spec(a) for a in prompt] + [s_spec(a) for a in sample] + [c_spec(a) for a in consts],
        out_specs=[pl.BlockSpec((tm * ROW_TILES, LANES), lambda i: (i, 0)),
                   pl.BlockSpec((tm, D_MODEL), lambda i: (i, 0)),
                   pl.BlockSpec((SUBLANES, tm), lambda i: (0, i)),
                   pl.BlockSpec((SUBLANES, tm), lambda i: (0, i))],
        out_shape=[jax.ShapeDtypeStruct((t_all * ROW_TILES, LANES), F32),
                   jax.ShapeDtypeStruct((t_all, D_MODEL), F32),
                   jax.ShapeDtypeStruct((SUBLANES, t_all), jnp.int32),
                   jax.ShapeDtypeStruct((SUBLANES, t_all), F32)],
        compiler_params=_cparams(("arbitrary",)),
        name="mixer_out",
    )(*prompt, *sample, *consts)


def _deinterleave_kernel(w_ref, g_ref, l_ref, t_ref):
    t_ref[...] = w_ref[0].T
    half = t_ref.shape[0] // 2
    g_ref[0] = t_ref[pl.ds(0, half, stride=2), :].T.astype(BF16)
    l_ref[0] = t_ref[pl.ds(1, half, stride=2), :].T.astype(BF16)


def _deinterleave_w1(w_e1):
    e, d, f2 = w_e1.shape
    spec = pl.BlockSpec((1, LANES, f2 // 2), lambda i, c: (i, c, 0))
    return pl.pallas_call(
        _deinterleave_kernel,
        grid=(e, d // LANES),
        in_specs=[pl.BlockSpec((1, LANES, f2), lambda i, c: (i, c, 0))],
        out_specs=[spec, spec],
        out_shape=[jax.ShapeDtypeStruct((e, d, f2 // 2), BF16)] * 2,
        scratch_shapes=[pltpu.VMEM((f2, LANES), F32)],
        compiler_params=_cparams(("arbitrary", "arbitrary")),
        name="deinterleave_w1",
    )(w_e1)


def _routing_plan(top_e, n_tokens):
    a = n_tokens * TOP_K
    mb, tc = MOE_BLOCK, COMBINE_TILE
    flat_e = top_e.reshape(-1)
    order = jnp.argsort(flat_e).astype(jnp.int32)
    experts = jnp.arange(N_EXPERTS, dtype=jnp.int32)
    counts = jnp.sum((flat_e[:, None] == experts[None, :]).astype(jnp.int32), axis=0)
    padded = (counts + mb - 1) // mb * mb
    start = jnp.cumsum(counts) - counts
    pend = jnp.cumsum(padded)
    pstart = pend - padded
    n_blocks = -(-(a + N_EXPERTS * (mb - 1)) // mb)
    block_first = jnp.arange(n_blocks, dtype=jnp.int32) * mb
    block_e = jnp.minimum(jnp.sum((pend[None, :] <= block_first[:, None]).astype(jnp.int32), axis=1),
                          N_EXPERTS - 1).astype(jnp.int32)
    n_used = (pend[-1] // mb).astype(jnp.int32).reshape(1)
    e_p = jnp.repeat(block_e, mb)
    off = jnp.arange(n_blocks * mb, dtype=jnp.int32) - pstart[e_p]
    valid = off < counts[e_p]
    asg = order[jnp.clip(start[e_p] + off, 0, a - 1)]
    t, k = asg // TOP_K, asg % TOP_K
    row_real = ((t // tc) * TOP_K + k) * tc + t % tc
    row_pad = a + (pstart[e_p] - start[e_p]) + (off - counts[e_p])
    tok = jnp.where(valid, t, 0)
    row = jnp.where(valid, row_real, row_pad)
    idx = jnp.stack([tok.reshape(n_blocks, mb), row.reshape(n_blocks, mb)], axis=1)
    extra = jnp.stack([jnp.zeros((mb,), jnp.int32), n_blocks * mb + jnp.arange(mb, dtype=jnp.int32)])[None]
    return jnp.concatenate([idx, extra], axis=0), block_e, n_used


IDX_RING = 4


def _row_tile(r):
    start = r * ROW_TILES
    return pl.ds(start if isinstance(r, int) else pl.multiple_of(start, ROW_TILES), ROW_TILES)


def _expert_kernel(be_ref, nused_ref, idx_hbm, x_hbm, w1g_ref, w1l_ref, b1g_ref, b1l_ref, w2_ref, b2_ref,
                   y_hbm, xbuf, ybuf, idx_smem, gsem, ssem, isem, *, n_blocks):
    i = pl.program_id(0)
    n_used = nused_ref[0]
    mb = MOE_BLOCK
    rows = mb * ROW_TILES
    cur, other = i % 2, (i + 1) % 2

    def ring(block):
        return (block + IDX_RING) % IDX_RING

    def idx_copy(block):
        src = jnp.where(block < 0, n_blocks, jnp.minimum(block, n_blocks))
        return pltpu.make_async_copy(idx_hbm.at[src], idx_smem.at[ring(block)], isem.at[ring(block)])

    def gather_copy(r, tok, s):
        return pltpu.make_async_copy(x_hbm.at[_row_tile(tok), :], xbuf.at[s, _row_tile(r), :], gsem.at[s])

    def scatter_copy(r, row, s):
        return pltpu.make_async_copy(ybuf.at[s, _row_tile(r), :], y_hbm.at[_row_tile(row), :], ssem.at[s])

    def gathered(s):
        return pltpu.make_async_copy(x_hbm.at[pl.ds(0, rows), :], xbuf.at[s], gsem.at[s])

    def scattered(s):
        return pltpu.make_async_copy(ybuf.at[s], y_hbm.at[pl.ds(0, rows), :], ssem.at[s])

    def issue_rolled(copy, block, column, s):
        def body(r, carry):
            copy(r, idx_smem[ring(block), column, r], s).start()
            return carry
        lax.fori_loop(0, mb, body, 0)

    @pl.when(i == 0)
    def _():
        idx_copy(i).start()
        idx_copy(i - 1).start()
        idx_copy(i + 1).start()
        idx_copy(i).wait()
        idx_copy(i - 1).wait()
        issue_rolled(gather_copy, i, 0, cur)
        ybuf[other] = jnp.zeros((rows, LANES), F32)

    @pl.when(i < n_used)
    def _():
        idx_copy(i + 1).wait()
        idx_copy(i + 2).start()
        gathered(cur).wait()

        @pl.when(i >= 1)
        def _():
            scattered(cur).wait()

        for r in range(mb):
            gather_copy(r, idx_smem[ring(i + 1), 0, r], other).start(priority=r % 2)
        for r in range(mb):
            scatter_copy(r, idx_smem[ring(i - 1), 1, r], other).start(priority=r % 2)
        x = jnp.concatenate([xbuf[cur, pl.ds(s, mb, stride=ROW_TILES), :] for s in range(ROW_TILES)],
                            axis=1).astype(BF16)
        glu = jnp.minimum(_dot(x, w1g_ref[0]) + b1g_ref[0], SWIGLU_LIMIT)
        lin = jnp.clip(_dot(x, w1l_ref[0]) + b1l_ref[0], -SWIGLU_LIMIT, SWIGLU_LIMIT)
        act = glu * jax.nn.sigmoid(SWIGLU_ALPHA * glu) * (lin + 1.0)
        y = _dot(act.astype(BF16), w2_ref[0]) + b2_ref[0]
        for s in range(ROW_TILES):
            ybuf[cur, pl.ds(s, mb, stride=ROW_TILES), :] = y[:, s * LANES:(s + 1) * LANES]

    @pl.when(i == n_used - 1)
    def _():
        issue_rolled(scatter_copy, i, 1, cur)
        scattered(other).wait()
        scattered(cur).wait()
        gathered(other).wait()
        idx_copy(i + 2).wait()

    @pl.when(i >= n_used)
    def _():
        ybuf[cur] = jnp.zeros((rows, LANES), F32)
        fill = pltpu.make_async_copy(ybuf.at[cur], y_hbm.at[pl.ds(pl.multiple_of(i * rows, rows), rows), :],
                                     ssem.at[cur])
        fill.start()
        fill.wait()


def _experts(x1r, idx, block_e, n_used, w1g, w1l, b_e1, w_e2, b_e2):
    n_blocks = idx.shape[0] - 1
    mb = MOE_BLOCK
    rows = mb * ROW_TILES
    b1g, b1l = b_e1[:, None, 0::2], b_e1[:, None, 1::2]
    w2 = w_e2.astype(BF16)
    b2 = b_e2[:, None, :]
    wspec = pl.BlockSpec((1, D_MODEL, D_EXPERT), lambda i, be, nu: (be[i], 0, 0))
    bspec = pl.BlockSpec((1, 1, D_EXPERT), lambda i, be, nu: (be[i], 0, 0))
    return pl.pallas_call(
        functools.partial(_expert_kernel, n_blocks=n_blocks),
        grid_spec=pltpu.PrefetchScalarGridSpec(
            num_scalar_prefetch=2,
            grid=(n_blocks,),
            in_specs=[pl.BlockSpec(memory_space=pl.ANY),
                      pl.BlockSpec(memory_space=pl.ANY),
                      wspec, wspec, bspec, bspec,
                      pl.BlockSpec((1, D_EXPERT, D_MODEL), lambda i, be, nu: (be[i], 0, 0)),
                      pl.BlockSpec((1, 1, D_MODEL), lambda i, be, nu: (be[i], 0, 0))],
            out_specs=pl.BlockSpec(memory_space=pl.ANY),
            scratch_shapes=[pltpu.VMEM((2, rows, LANES), F32),
                            pltpu.VMEM((2, rows, LANES), F32),
                            pltpu.SMEM((IDX_RING, 2, mb), jnp.int32),
                            pltpu.SemaphoreType.DMA((2,)),
                            pltpu.SemaphoreType.DMA((2,)),
                            pltpu.SemaphoreType.DMA((IDX_RING,))]),
        out_shape=jax.ShapeDtypeStruct(((n_blocks + 1) * rows, LANES), F32),
        compiler_params=_cparams(("arbitrary",)),
        name="experts",
    )(block_e, n_used, idx, x1r, w1g, w1l, b1g, b1l, w2, b2)


def _combine_kernel(y_ref, gate_ref, resid_ref, g_ref, b_ref, op_ref, os_ref, *, n_prompt_tiles):
    tm = COMBINE_TILE
    z = resid_ref[...]
    gate = gate_ref[...]
    for kk in range(TOP_K):
        y_k = jnp.concatenate(
            [y_ref[pl.ds(kk * tm * ROW_TILES + s, tm, stride=ROW_TILES), :] for s in range(ROW_TILES)], axis=1)
        z = z + y_k * gate[:, kk:kk + 1]
    out = _layer_norm(z, g_ref[...], b_ref[...])

    @pl.when(pl.program_id(0) < n_prompt_tiles)
    def _():
        op_ref[...] = out

    @pl.when(pl.program_id(0) >= n_prompt_tiles)
    def _():
        os_ref[...] = out


def _combine(ys, top_g, resid, ln2_g, ln2_b, n_prompt):
    t_all = resid.shape[0]
    tm = COMBINE_TILE
    assert n_prompt % tm == 0 and (t_all - n_prompt) % tm == 0
    npt, n_tiles = n_prompt // tm, t_all // tm
    return pl.pallas_call(
        functools.partial(_combine_kernel, n_prompt_tiles=npt),
        grid=(n_tiles,),
        in_specs=[pl.BlockSpec((tm * TOP_K * ROW_TILES, LANES), lambda i: (i, 0)),
                  pl.BlockSpec((tm, top_g.shape[1]), lambda i: (i, 0)),
                  pl.BlockSpec((tm, D_MODEL), lambda i: (i, 0)),
                  pl.BlockSpec((1, D_MODEL), lambda i: (0, 0)),
                  pl.BlockSpec((1, D_MODEL), lambda i: (0, 0))],
        out_specs=[pl.BlockSpec((tm, D_MODEL), lambda i: (jnp.minimum(i, npt - 1), 0)),
                   pl.BlockSpec((tm, D_MODEL), lambda i: (jnp.maximum(i - npt, 0), 0))],
        out_shape=[jax.ShapeDtypeStruct((n_prompt, D_MODEL), F32),
                   jax.ShapeDtypeStruct((t_all - n_prompt, D_MODEL), F32)],
        compiler_params=_cparams(("arbitrary",)),
        name="combine_ln2",
    )(ys, top_g, resid, ln2_g.reshape(1, -1), ln2_b.reshape(1, -1))


def kernel(x_prompt, x_sample, cache_att_k, cache_att_v, state_ret, p_prompt, p_sample, w_in, ret_gn_w, ret_gn_b, w_out, ln1_g, ln1_b, w_router, b_router, w_e1, b_e1, w_e2, b_e2, w_pl_gate, b_pl_gate, w_pl_proj, ln2_g, ln2_b):
    assert w_in.shape[0] == DEPTH == 1
    B, S, _ = x_prompt.shape
    DB, T, _ = x_sample.shape
    H, D = ATT_HEADS, ATT_HEAD_DIM
    l = 0
    w_in_b = w_in[l].astype(BF16)

    pos_p = jnp.arange(S, dtype=F32)
    pd, pt = _in_projection(x_prompt, w_in_b, pos_p, PROJ_NAMES, ("k_a", "v_a"), TOKEN_TILE, False)
    zero_state = jnp.zeros((B, 2, RET_QK // 2, RET_V // 2), F32)
    o_ret_p, st_p = _retention(pd["q_r"], pd["k_r"], pd["v_r"], zero_state, RET_CHUNK, RET_CHUNK)
    o_att_p = _attention_prompt(pd["q_a"], pd["k_a"], pd["v_a"])
    keep = min(MAX_WINDOW, S)
    as_cache = lambda a: a.reshape(B, H, D, S).transpose(0, 3, 1, 2)[None, :, S - keep:]
    new_k_p, new_v_p = as_cache(pt["k_a"]), as_cache(pt["v_a"])
    new_st_p = _blockdiag_to_state(st_p)[None]

    ts = DB * T
    by_step = lambda a: a.transpose(1, 0, 2).reshape(ts, a.shape[-1])
    xs = by_step(x_sample)
    pos_s = jnp.repeat(PAST_LEN + jnp.arange(T, dtype=F32), DB)
    sd, st = _in_projection(xs[None], w_in_b, pos_s, ("g_r", "q_a", "k_a", "v_a"),
                            ("q_r", "k_r", "v_r", "k_a", "v_a"), DB, True)
    state_t = jnp.transpose(state_ret[l], (1, 2, 3, 0))
    o_ret_t, state_new = _retention_sample(st["q_r"], st["k_r"], st["v_r"], state_t)
    o_ret_s = o_ret_t.transpose(0, 2, 1).reshape(ts, RET_V)
    by_batch = lambda a: a[0].reshape(T, DB, -1).transpose(1, 0, 2)
    cache_t = lambda c: jnp.transpose(c[l], (0, 2, 3, 1))
    o_att_s = _attention_sample(by_batch(sd["q_a"]), by_batch(sd["k_a"]), by_batch(sd["v_a"]),
                                cache_t(cache_att_k), cache_t(cache_att_v))
    as_new = lambda a: a.reshape(T, H, D, DB).transpose(3, 0, 1, 2)[None]
    new_k_s, new_v_s = as_new(st["k_a"]), as_new(st["v_a"])
    new_st_s = jnp.transpose(state_new, (3, 0, 1, 2)).astype(state_ret.dtype)[None]

    tp = B * S
    flat2 = lambda a: a.reshape(-1, a.shape[-1])
    prompt = (flat2(o_ret_p), flat2(pd["g_r"]), flat2(o_att_p), flat2(x_prompt), flat2(p_prompt[l]))
    sample = (o_ret_s, sd["g_r"][0], by_step(o_att_s), xs, by_step(p_sample[l]))
    x1r, resid, tope, topg = _mixer(prompt, sample, (ret_gn_w[l], ret_gn_b[l], w_out[l], ln1_g[l], ln1_b[l],
                                                     w_router[l], b_router[l], w_pl_gate[l], b_pl_gate[l],
                                                     w_pl_proj[l]))

    t_all = tp + ts
    idx, block_e, n_used = _routing_plan(tope[:TOP_K].T, t_all)
    w1g, w1l = _deinterleave_w1(w_e1[l])
    ys = _experts(x1r, idx, block_e, n_used, w1g, w1l, b_e1[l], w_e2[l], b_e2[l])
    y_p, y_s = _combine(ys, topg.T, resid, ln2_g[l], ln2_b[l], tp)
    return (y_p.reshape(B, S, D_MODEL), y_s.reshape(T, DB, D_MODEL).transpose(1, 0, 2),
            new_k_p, new_v_p, new_st_p, new_k_s, new_v_s, new_st_s)
```

```python
import functools

import numpy as np
import jax
import jax.numpy as jnp
from jax import lax
from jax.experimental import pallas as pl
from jax.experimental.pallas import tpu as pltpu

F32 = jnp.float32
BF16 = jnp.bfloat16

D_MODEL = 1024
DEPTH = 1
PAST_LEN = 8192
RET_HEADS = 8
RET_QK_DIM = 32
RET_V_DIM = 64
RET_CHUNK = 128
RET_ROPE_BASE = 10000.0
ATT_HEADS = 8
ATT_HEAD_DIM = 64
ROT_DIMS = ATT_HEAD_DIM // 4
ROPE_THETA = 500000.0
DILATED_PATTERNS = ((128, 1), (512, 4), (2048, 16))
MAX_WINDOW = 2048
RET_QK = RET_HEADS * RET_QK_DIM
RET_V = RET_HEADS * RET_V_DIM
ATT_W = ATT_HEADS * ATT_HEAD_DIM
IN_COLS = RET_QK * 2 + RET_V * 2 + ATT_W * 3
N_EXPERTS = 32
TOP_K = 4
D_EXPERT = D_MODEL
SWIGLU_ALPHA = 1.702
SWIGLU_LIMIT = 7.0
PLE_DIM = 256
LN_EPS = 1e-5
GN_EPS = 1e-6
DEEPNORM_ALPHA = (2 * DEPTH) ** 0.25
NEG_INF = -1e30

LANES = 128
SUBLANES = 8
ROW_TILES = D_MODEL // LANES
VMEM_LIMIT = 56 * 1024 * 1024

TOKEN_TILE = 512
MOE_BLOCK = 256
COMBINE_TILE = 256
ATT_BLOCK = 128
ATT_GROUP = 16


def _dot(a, b):
    return jnp.dot(a, b, preferred_element_type=F32)


def _dot_nt(a, b):
    return lax.dot_general(a, b, (((1,), (1,)), ((), ())), preferred_element_type=F32)


def _dot_tn(a, b):
    return lax.dot_general(a, b, (((0,), (0,)), ((), ())), preferred_element_type=F32)


def _cparams(sem):
    return pltpu.CompilerParams(dimension_semantics=sem, vmem_limit_bytes=VMEM_LIMIT)


def _rotary_tables(pos, inv_freq, head_dim, n_heads):
    half = inv_freq.shape[0]
    ang = pos.astype(F32)[:, None] * inv_freq[None, :]
    cos, sin = jnp.cos(ang), jnp.sin(ang)
    rest = head_dim - 2 * half
    n = pos.shape[0]
    cos_h = jnp.concatenate([cos, cos, jnp.ones((n, rest), F32)], axis=1)
    sin_h = jnp.concatenate([-sin, sin, jnp.zeros((n, rest), F32)], axis=1)
    return jnp.tile(cos_h, (1, n_heads)), jnp.tile(sin_h, (1, n_heads))


def _ret_inv_freq():
    return 1.0 / (RET_ROPE_BASE ** jnp.linspace(0.0, 1.0, RET_QK_DIM // 2, dtype=F32))


def _att_inv_freq():
    return ROPE_THETA ** (-jnp.arange(0, ROT_DIMS, 2, dtype=F32) / ROT_DIMS)


def _ret_log_decay():
    return jnp.log(1.0 - 2.0 ** (-5.0 - jnp.arange(RET_HEADS, dtype=F32)))


def _retention_tables(c_true, c_pad):
    lg = _ret_log_decay()
    idx = jnp.arange(c_pad, dtype=F32)
    live = idx < c_true
    diff = idx[:, None] - idx[None, :]
    decay = jnp.where(diff[None] >= 0, jnp.exp(jnp.maximum(diff, 0.0)[None] * lg[:, None, None]), 0.0)
    decay = jnp.where(live[None, :, None] & live[None, None, :], decay, 0.0)
    cross = jnp.exp((idx + 1.0)[:, None] * lg[None, :])
    cross = jnp.where(live[:, None], cross, 0.0)
    kdec = jnp.exp((c_true - 1.0 - idx)[:, None] * lg[None, :])
    kdec = jnp.where(live[:, None], kdec, 0.0)
    sdec = jnp.exp(c_true * lg)
    hh = RET_HEADS // 2
    decay = decay.reshape(2, hh, c_pad, c_pad)
    cross = jnp.repeat(cross, RET_V_DIM, axis=1).reshape(c_pad, 2, hh * RET_V_DIM).transpose(1, 0, 2)
    kdec = jnp.repeat(kdec, RET_QK_DIM, axis=1).reshape(c_pad, 2, hh * RET_QK_DIM).transpose(1, 0, 2)
    sdec = jnp.broadcast_to(jnp.repeat(sdec, RET_V_DIM)[None, :], (hh * RET_QK_DIM, RET_V)) \
        .reshape(hh * RET_QK_DIM, 2, hh * RET_V_DIM).transpose(1, 0, 2)
    return decay, cross, kdec, sdec


def _rotate(h, cos, sin, half, period):
    outs = []
    for j in range(h.shape[1] // LANES):
        blk = h[:, j * LANES:(j + 1) * LANES]
        lane = lax.broadcasted_iota(jnp.int32, blk.shape, 1)
        first = (lane % period) < half
        partner = jnp.where(first, pltpu.roll(blk, LANES - half, 1), pltpu.roll(blk, half, 1))
        outs.append(blk * cos[:, j * LANES:(j + 1) * LANES] + partner * sin[:, j * LANES:(j + 1) * LANES])
    return jnp.concatenate(outs, axis=1)


PROJ_NAMES = ("q_r", "k_r", "v_r", "g_r", "q_a", "k_a", "v_a")
PROJ_WIDTHS = dict(zip(PROJ_NAMES, (RET_QK, RET_QK, RET_V, RET_V, ATT_W, ATT_W, ATT_W)))


def _inproj_kernel(x_ref, w_ref, cr_ref, sr_ref, ca_ref, sa_ref, *out_refs, direct, transposed):
    x = x_ref[0].astype(BF16)
    cr, sr, ca, sa = cr_ref[...], sr_ref[...], ca_ref[...], sa_ref[...]
    finish = {
        "q_r": lambda h: _rotate(h, cr, sr, RET_QK_DIM // 2, RET_QK_DIM),
        "k_r": lambda h: _rotate(h, cr, sr, RET_QK_DIM // 2, RET_QK_DIM) * (RET_QK_DIM ** -0.5),
        "q_a": lambda h: _rotate(h, ca, sa, ROT_DIMS // 2, ATT_HEAD_DIM) * (ATT_HEAD_DIM ** -0.5),
        "k_a": lambda h: _rotate(h, ca, sa, ROT_DIMS // 2, ATT_HEAD_DIM),
    }
    refs = dict(zip([("d", n) for n in direct] + [("t", n) for n in transposed], out_refs))
    c = 0
    for name in PROJ_NAMES:
        width = PROJ_WIDTHS[name]
        if name in direct or name in transposed:
            h = _dot(x, w_ref[:, c:c + width])
            h = finish.get(name, lambda v: v)(h)
            if name in direct:
                refs[("d", name)][0] = h
            if name in transposed:
                refs[("t", name)][0] = h.T
        c += width


def _in_projection(x, w_in_b, pos, direct, transposed, tm, tile_major):
    B, S, _ = x.shape
    assert S % tm == 0 and (B == 1 or not tile_major)
    cr, sr = _rotary_tables(pos, _ret_inv_freq(), RET_QK_DIM, RET_HEADS)
    ca, sa = _rotary_tables(pos, _att_inv_freq(), ATT_HEAD_DIM, ATT_HEADS)
    tab = lambda w: pl.BlockSpec((tm, w), lambda s, b: (s, 0))
    specs = [pl.BlockSpec((1, tm, PROJ_WIDTHS[n]), lambda s, b: (b, s, 0)) for n in direct]
    shapes = [jax.ShapeDtypeStruct((B, S, PROJ_WIDTHS[n]), F32) for n in direct]
    for n in transposed:
        w = PROJ_WIDTHS[n]
        if tile_major:
            specs.append(pl.BlockSpec((1, w, tm), lambda s, b: (s, 0, 0)))
            shapes.append(jax.ShapeDtypeStruct((S // tm, w, tm), F32))
        else:
            specs.append(pl.BlockSpec((1, w, tm), lambda s, b: (b, 0, s)))
            shapes.append(jax.ShapeDtypeStruct((B, w, S), F32))
    outs = pl.pallas_call(
        functools.partial(_inproj_kernel, direct=direct, transposed=transposed),
        grid=(S // tm, B),
        in_specs=[pl.BlockSpec((1, tm, D_MODEL), lambda s, b: (b, s, 0)),
                  pl.BlockSpec((D_MODEL, IN_COLS), lambda s, b: (0, 0)),
                  tab(RET_QK), tab(RET_QK), tab(ATT_W), tab(ATT_W)],
        out_specs=specs,
        out_shape=shapes,
        compiler_params=_cparams(("arbitrary", "arbitrary")),
        name="in_projection",
    )(x, w_in_b, cr, sr, ca, sa)
    return dict(zip(direct, outs[:len(direct)])), dict(zip(transposed, outs[len(direct):]))


RET_GROUP = 4


def _retention_kernel(q_ref, k_ref, v_ref, s0_ref, dec_ref, cross_ref, kdec_ref, sdec_ref,
                      o_ref, so_ref, upd_ref, st_ref, *, chunk, n_chunks):
    hh = RET_HEADS // 2
    qk_w, v_w = hh * RET_QK_DIM, hh * RET_V_DIM
    lane_q = lax.broadcasted_iota(jnp.int32, (chunk, qk_w), 1) // RET_QK_DIM
    lane_v = lax.broadcasted_iota(jnp.int32, (chunk, v_w), 1) // RET_V_DIM
    blockdiag = (lax.broadcasted_iota(jnp.int32, (qk_w, v_w), 0) // RET_QK_DIM
                 == lax.broadcasted_iota(jnp.int32, (qk_w, v_w), 1) // RET_V_DIM)
    cross, kdec, sdec = cross_ref[0], kdec_ref[0], sdec_ref[0]
    rows = lambda ci: pl.ds(ci * chunk, chunk)

    group = min(RET_GROUP, n_chunks)
    for c0 in range(0, n_chunks, group):
        cs = range(c0, min(c0 + group, n_chunks))
        inner = {}
        for ci in cs:
            q, kb = q_ref[0, rows(ci), :], k_ref[0, rows(ci), :].astype(BF16)
            for h in range(hh):
                qm = jnp.where(lane_q == h, q, 0.0).astype(BF16)
                inner[ci, h] = (_dot_nt(qm, kb) * dec_ref[0, h]).astype(BF16)
        for ci in cs:
            k, vb = k_ref[0, rows(ci), :], v_ref[0, rows(ci), :].astype(BF16)
            o = jnp.zeros((chunk, v_w), F32)
            for h in range(hh):
                o = jnp.where(lane_v == h, _dot(inner[ci, h], vb), o)
            o_ref[0, rows(ci), :] = o
            upd_ref[ci] = jnp.where(blockdiag, _dot_tn((k * kdec).astype(BF16), vb), 0.0)

    state = s0_ref[0, 0]
    for ci in range(n_chunks):
        st_ref[ci] = state.astype(BF16)
        state = state * sdec + upd_ref[ci]
    so_ref[0, 0] = state

    for ci in range(n_chunks):
        o_ref[0, rows(ci), :] += _dot(q_ref[0, rows(ci), :].astype(BF16), st_ref[ci]) * cross


def _retention(q, k, v, state_bd, c_true, c_pad):
    B, S, _ = q.shape
    hh = RET_HEADS // 2
    qk_w, v_w = hh * RET_QK_DIM, hh * RET_V_DIM
    decay, cross, kdec, sdec = _retention_tables(c_true, c_pad)
    kern = functools.partial(_retention_kernel, chunk=c_pad, n_chunks=S // c_pad)
    return pl.pallas_call(
        kern,
        grid=(B, 2),
        in_specs=[pl.BlockSpec((1, S, qk_w), lambda b, g: (b, 0, g)),
                  pl.BlockSpec((1, S, qk_w), lambda b, g: (b, 0, g)),
                  pl.BlockSpec((1, S, v_w), lambda b, g: (b, 0, g)),
                  pl.BlockSpec((1, 1, qk_w, v_w), lambda b, g: (b, g, 0, 0)),
                  pl.BlockSpec((1, hh, c_pad, c_pad), lambda b, g: (g, 0, 0, 0)),
                  pl.BlockSpec((1, c_pad, v_w), lambda b, g: (g, 0, 0)),
                  pl.BlockSpec((1, c_pad, qk_w), lambda b, g: (g, 0, 0)),
                  pl.BlockSpec((1, qk_w, v_w), lambda b, g: (g, 0, 0))],
        out_specs=[pl.BlockSpec((1, S, v_w), lambda b, g: (b, 0, g)),
                   pl.BlockSpec((1, 1, qk_w, v_w), lambda b, g: (b, g, 0, 0))],
        out_shape=[jax.ShapeDtypeStruct((B, S, RET_V), F32),
                   jax.ShapeDtypeStruct((B, 2, qk_w, v_w), F32)],
        scratch_shapes=[pltpu.VMEM((S // c_pad, qk_w, v_w), F32),
                        pltpu.VMEM((S // c_pad, qk_w, v_w), BF16)],
        compiler_params=_cparams(("arbitrary", "arbitrary")),
        name="retention",
    )(q, k, v, state_bd, decay, cross, kdec, sdec)


def _blockdiag_to_state(bd):
    B = bd.shape[0]
    hh = RET_HEADS // 2
    s = bd.reshape(B, 2, hh, RET_QK_DIM, hh, RET_V_DIM)
    s = jnp.stack([s[:, :, h, :, h, :] for h in range(hh)], axis=2)
    return s.reshape(B, RET_HEADS, RET_QK_DIM, RET_V_DIM)


def _retention_sample_kernel(gam_ref, q_ref, k_ref, v_ref, s_ref, o_ref, so_ref, *, steps):
    h = pl.program_id(0)
    gam = [gam_ref[h, p] for p in range(steps + 1)]
    q = [q_ref[t] for t in range(steps)]
    k = [k_ref[t] for t in range(steps)]
    v = [v_ref[t] for t in range(steps)]
    o = [jnp.zeros(v[0].shape, F32) for _ in range(steps)]
    for d in range(RET_QK_DIM):
        s_d = s_ref[0, d]
        new = s_d * gam[steps]
        for t in range(steps):
            o[t] = o[t] + q[t][d:d + 1, :] * s_d
            new = new + (k[t][d:d + 1, :] * gam[steps - 1 - t]) * v[t]
        so_ref[0, d] = new
    for i in range(steps):
        o[i] = o[i] * gam[i + 1]
        for j in range(i + 1):
            qk = jnp.sum(q[i] * k[j], axis=0, keepdims=True) * gam[i - j]
            o[i] = o[i] + qk * v[j]
        o_ref[i] = o[i]


def _retention_sample(q_t, k_t, v_t, state):
    steps, _, nb = q_t.shape
    gam = jnp.exp(_ret_log_decay()[:, None] * jnp.arange(steps + 1, dtype=F32)[None, :])
    qk_spec = pl.BlockSpec((steps, RET_QK_DIM, nb), lambda h: (0, h, 0))
    v_spec = pl.BlockSpec((steps, RET_V_DIM, nb), lambda h: (0, h, 0))
    s_spec = pl.BlockSpec((1, RET_QK_DIM, RET_V_DIM, nb), lambda h: (h, 0, 0, 0))
    return pl.pallas_call(
        functools.partial(_retention_sample_kernel, steps=steps),
        grid=(RET_HEADS,),
        in_specs=[pl.BlockSpec(memory_space=pltpu.SMEM), qk_spec, qk_spec, v_spec, s_spec],
        out_specs=[v_spec, s_spec],
        out_shape=[jax.ShapeDtypeStruct(v_t.shape, F32), jax.ShapeDtypeStruct(state.shape, F32)],
        compiler_params=_cparams(("arbitrary",)),
        name="retention_sample",
    )(gam, q_t, k_t, v_t, state)


def _attn_prompt_kernel(q_ref, k_ref, v_ref, o_ref, o_ref_p, lse_ref, *, seq):
    n = ATT_BLOCK
    head0 = lax.broadcasted_iota(jnp.int32, (n, LANES), 1) < ATT_HEAD_DIM
    head0_kv = lax.broadcasted_iota(jnp.int32, (2 * n, LANES), 1) < ATT_HEAD_DIM
    a = lax.broadcasted_iota(jnp.int32, (n, 2 * n), 0)
    j = lax.broadcasted_iota(jnp.int32, (n, 2 * n), 1)
    bias_pc = jnp.where(((j < n) & (j >= a)) | ((j >= n) & (j - n <= a)), 0.0, NEG_INF)
    bias_c = bias_pc[:, n:]
    bias_first = jnp.where(j < n, NEG_INF, bias_pc)

    def rows(start, d):
        if d > 1:
            return pl.ds(start, n, stride=d)
        return pl.ds(start if isinstance(start, int) else pl.multiple_of(start, n), n)

    def blocks(p_idx, d, specs):
        loaded = []
        for cur, prev, has_prev in specs:
            q = q_ref[0, rows(cur, d), :]
            if has_prev is False:
                kk, vv = k_ref[0, rows(cur, d), :], v_ref[0, rows(cur, d), :]
                bias, hkv = bias_c, head0
            else:
                kk = jnp.concatenate([k_ref[0, rows(prev, d), :], k_ref[0, rows(cur, d), :]], axis=0)
                vv = jnp.concatenate([v_ref[0, rows(prev, d), :], v_ref[0, rows(cur, d), :]], axis=0)
                bias = bias_pc if has_prev is True else jnp.where(has_prev, bias_pc, bias_first)
                hkv = head0_kv
            loaded.append((q, kk.astype(BF16), vv, bias, hkv))
        scores = []
        for q, kb, vv, bias, hkv in loaded:
            for h in range(2):
                mine = head0 if h == 0 else jnp.logical_not(head0)
                scores.append(_dot_nt(jnp.where(mine, q, 0.0).astype(BF16), kb) + bias)
        probs, maxes = [], []
        for s in scores:
            m = jnp.max(s, axis=1, keepdims=True)
            probs.append(jnp.exp(s - m).astype(BF16))
            maxes.append(m)
        pvs = []
        for u, (q, kb, vv, bias, hkv) in enumerate(loaded):
            for h in range(2):
                mine_kv = hkv if h == 0 else jnp.logical_not(hkv)
                pvs.append(_dot(probs[2 * u + h], jnp.where(mine_kv, vv, 1.0).astype(BF16)))
        for u, (cur, prev, has_prev) in enumerate(specs):
            pv0, pv1 = pvs[2 * u], pvs[2 * u + 1]
            acc_u = jnp.where(head0, pv0, pv1)
            l_u = pltpu.roll(jnp.where(head0, pv1, pv0), ATT_HEAD_DIM, 1)
            m_u = jnp.where(head0, maxes[2 * u], maxes[2 * u + 1])
            o_ref_p[p_idx, rows(cur, d), :] = acc_u / l_u
            lse_ref[p_idx, rows(cur, d), :] = m_u + jnp.log(l_u)

    G = ATT_GROUP
    for p_idx, (window, d) in enumerate(DILATED_PATTERNS):
        nb = (seq // d) // n
        assert (d * nb) % G == 0 and (nb % G == 0 or G % nb == 0)

        def group(g, carry, p_idx=p_idx, d=d, nb=nb):
            specs = []
            for jj in range(G):
                if nb > G:
                    first = g * G
                    r = 0 if d == 1 else first // nb
                    c = first % nb + jj
                    specs.append((c * (n * d) + r, jnp.maximum(c - 1, 0) * (n * d) + r,
                                  True if jj > 0 else c > 0))
                else:
                    r = g * (G // nb) + jj // nb
                    c = jj % nb
                    specs.append((c * (n * d) + r, (c - 1) * (n * d) + r, c > 0))
            blocks(p_idx, d, specs)
            return carry

        lax.fori_loop(0, d * nb // G, group, 0)

    def merge(c, carry):
        rs = pl.ds(pl.multiple_of(c * n, n), n)
        lse = [lse_ref[p, rs, :] for p in range(len(DILATED_PATTERNS))]
        top = functools.reduce(jnp.maximum, lse)
        w = [jnp.exp(x - top) for x in lse]
        num = functools.reduce(lambda x, y: x + y, [w[p] * o_ref_p[p, rs, :] for p in range(len(w))])
        o_ref[0, rs, :] = num / functools.reduce(lambda x, y: x + y, w)
        return carry

    lax.fori_loop(0, seq // n, merge, 0)


def _attention_prompt(q, k, v):
    B, S, _ = q.shape
    spec = pl.BlockSpec((1, S, LANES), lambda b, g: (b, 0, g))
    return pl.pallas_call(
        functools.partial(_attn_prompt_kernel, seq=S),
        grid=(B, ATT_W // LANES),
        in_specs=[spec, spec, spec],
        out_specs=spec,
        out_shape=jax.ShapeDtypeStruct((B, S, ATT_W), F32),
        scratch_shapes=[pltpu.VMEM((len(DILATED_PATTERNS), S, LANES), F32)] * 2,
        compiler_params=_cparams(("arbitrary", "arbitrary")),
        name="attention_prompt",
    )(q, k, v)


def _attn_sample_kernel(q_ref, kn_ref, vn_ref, kc_ref, vc_ref, cc_ref, cn_ref, o_ref, *, steps):
    q = q_ref[0].astype(BF16)
    width = q.shape[1]
    kc = kc_ref[0].reshape(width, -1).astype(BF16)
    vc = vc_ref[0].reshape(width, -1).astype(BF16)
    cnt_c, cnt_n = cc_ref[...], cn_ref[...]
    s_c = jnp.where(cnt_c > 0.0, _dot(q, kc), NEG_INF)
    s_n = jnp.where(cnt_n > 0.0, _dot_nt(q, kn_ref[0].astype(BF16)), NEG_INF)
    m = jnp.maximum(s_c.max(axis=1, keepdims=True), s_n.max(axis=1, keepdims=True))
    p_c = cnt_c * jnp.exp(s_c - m)
    p_n = cnt_n * jnp.exp(s_n - m)
    den = p_c.sum(axis=1, keepdims=True) + p_n.sum(axis=1, keepdims=True)
    num = _dot_nt(p_c.astype(BF16), vc) + _dot(p_n.astype(BF16), vn_ref[0].astype(BF16))
    rows = lax.broadcasted_iota(jnp.int32, num.shape, 0)
    lanes = lax.broadcasted_iota(jnp.int32, num.shape, 1)
    out = jnp.where(rows % ATT_HEADS == lanes // ATT_HEAD_DIM, num / den, 0.0)
    for t in range(steps):
        o_ref[0, t:t + 1, :] = jnp.sum(out[t * ATT_HEADS:(t + 1) * ATT_HEADS], axis=0, keepdims=True)


def _sample_key_counts(dec_seq, buf, new_pad):
    t_q = np.repeat(np.arange(dec_seq), ATT_HEADS)[:, None]
    back = buf + t_q - np.arange(buf)[None, :]
    cnt_c = np.zeros(back.shape, np.float32)
    for w, d in DILATED_PATTERNS:
        cnt_c += ((back % d == 0) & (back >= d) & (back <= w)).astype(np.float32)
    back = t_q - np.arange(new_pad)[None, :]
    cnt_n = np.zeros(back.shape, np.float32)
    for w, d in DILATED_PATTERNS:
        cnt_n += ((back >= 0) & (back % d == 0) & (back <= w)).astype(np.float32)
    cnt_n *= (np.arange(new_pad) < dec_seq)[None, :]
    return jnp.asarray(cnt_c), jnp.asarray(cnt_n)


def _attention_sample(q, k_new, v_new, cache_k_t, cache_v_t):
    B, T, width = q.shape
    _, H, D, buf = cache_k_t.shape
    new_pad = SUBLANES
    cnt_c, cnt_n = _sample_key_counts(T, buf, new_pad)
    q_rows = (q.reshape(B, T, 1, H, D) * jnp.eye(H, dtype=F32)[None, None, :, :, None]).reshape(B, T * H, width)
    pad = lambda x: jnp.pad(x, ((0, 0), (0, new_pad - T), (0, 0)))
    new_spec = pl.BlockSpec((1, new_pad, width), lambda b: (b, 0, 0))
    cache_spec = pl.BlockSpec((1, H, D, buf), lambda b: (b, 0, 0, 0))
    const = lambda x: pl.BlockSpec(x.shape, lambda b: (0, 0))
    return pl.pallas_call(
        functools.partial(_attn_sample_kernel, steps=T),
        grid=(B,),
        in_specs=[pl.BlockSpec((1, T * H, width), lambda b: (b, 0, 0)), new_spec, new_spec,
                  cache_spec, cache_spec, const(cnt_c), const(cnt_n)],
        out_specs=pl.BlockSpec((1, T, width), lambda b: (b, 0, 0)),
        out_shape=jax.ShapeDtypeStruct((B, T, width), F32),
        compiler_params=_cparams(("arbitrary",)),
        name="attention_sample",
    )(q_rows, pad(k_new), pad(v_new), cache_k_t, cache_v_t, cnt_c, cnt_n)


def _layer_norm(y, g, b):
    mu = jnp.mean(y, axis=-1, keepdims=True)
    d = y - mu
    var = jnp.mean(d * d, axis=-1, keepdims=True)
    return d * lax.rsqrt(var + LN_EPS) * g + b


def _mixer_kernel(oret_p, g_p, oatt_p, x_p, pe_p, oret_s, g_s, oatt_s, x_s, pe_s,
                  avg_ref, gnw_ref, gnb_ref, wor_ref, woa_ref, ln1g_ref, ln1b_ref,
                  wg_ref, bg_ref, wp_ref, wr_ref, br_ref,
                  x1r_ref, resid_ref, tope_ref, topg_ref, *, n_prompt_tiles):
    def tile(oret_ref, g_ref, oatt_ref, x_ref, pe_ref):
        o_ret, g, o_att, x, pe = oret_ref[...], g_ref[...], oatt_ref[...], x_ref[...], pe_ref[...]
        avg = avg_ref[...]

        def head_mean(z):
            hi = z.astype(BF16)
            lo = (z - hi.astype(F32)).astype(BF16)
            return _dot(hi, avg) + _dot(lo, avg)

        d = o_ret - head_mean(o_ret)
        o_n = d * lax.rsqrt(head_mean(d * d) + GN_EPS) * gnw_ref[...] + gnb_ref[...]
        ret = g * jax.nn.sigmoid(g) * o_n
        mix = _dot(ret.astype(BF16), wor_ref[...]) + _dot(o_att.astype(BF16), woa_ref[...])
        x1 = _layer_norm(DEEPNORM_ALPHA * x + mix, ln1g_ref[...], ln1b_ref[...])
        x1b = x1.astype(BF16)
        ple = jax.nn.sigmoid(_dot(x1b, wg_ref[...]) + bg_ref[...]) * _dot(pe.astype(BF16), wp_ref[...])
        resid_ref[...] = DEEPNORM_ALPHA * x1 + ple
        tm = x1.shape[0]
        for s in range(ROW_TILES):
            x1r_ref[pl.ds(s, tm, stride=ROW_TILES), :] = x1[:, s * LANES:(s + 1) * LANES]

        logits = lax.dot_general(wr_ref[...], x1, (((1,), (1,)), ((), ())), preferred_element_type=F32,
                                 precision=lax.Precision.HIGHEST) + br_ref[...]
        row = lax.broadcasted_iota(jnp.int32, logits.shape, 0).astype(F32)
        work = logits
        vals, idxs = [], []
        for _ in range(TOP_K):
            m = jnp.max(work, axis=0, keepdims=True)
            idx = jnp.min(jnp.where(work == m, row, float(N_EXPERTS)), axis=0, keepdims=True)
            vals.append(m)
            idxs.append(idx)
            work = jnp.where(row == idx, -jnp.inf, work)
        exps = [jnp.exp(v - vals[0]) for v in vals]
        den = exps[0]
        for e in exps[1:]:
            den = den + e
        fill = [jnp.zeros_like(den)] * (SUBLANES - TOP_K)
        tope_ref[...] = jnp.concatenate(idxs + fill, axis=0).astype(jnp.int32)
        topg_ref[...] = jnp.concatenate([e / den for e in exps] + fill, axis=0)

    @pl.when(pl.program_id(0) < n_prompt_tiles)
    def _():
        tile(oret_p, g_p, oatt_p, x_p, pe_p)

    @pl.when(pl.program_id(0) >= n_prompt_tiles)
    def _():
        tile(oret_s, g_s, oatt_s, x_s, pe_s)


def _mixer(prompt, sample, weights):
    (gn_w, gn_b, w_out, ln1_g, ln1_b, w_router, b_router, w_pl_gate, b_pl_gate, w_pl_proj) = weights
    tp, ts = prompt[3].shape[0], sample[3].shape[0]
    tm = TOKEN_TILE
    assert tp % tm == 0 and ts % tm == 0
    npt, nst = tp // tm, ts // tm
    t_all = tp + ts
    avg = jnp.asarray(np.kron(np.eye(RET_HEADS), np.full((RET_V_DIM, RET_V_DIM), 1.0 / RET_V_DIM)), BF16)
    w_out_b = w_out.astype(BF16)
    row = lambda v: v.reshape(1, -1)
    consts = [avg, row(gn_w), row(gn_b), w_out_b[:RET_V], w_out_b[RET_V:], row(ln1_g), row(ln1_b),
              w_pl_gate.astype(BF16), row(b_pl_gate), w_pl_proj.astype(BF16), w_router.T, b_router.reshape(-1, 1)]
    p_spec = lambda a: pl.BlockSpec((tm, a.shape[1]), lambda i: (jnp.minimum(i, npt - 1), 0))
    s_spec = lambda a: pl.BlockSpec((tm, a.shape[1]), lambda i: (jnp.maximum(i - npt, 0), 0))
    c_spec = lambda a: pl.BlockSpec(a.shape, lambda i: (0, 0))
    return pl.pallas_call(
        functools.partial(_mixer_kernel, n_prompt_tiles=npt),
        grid=(npt + nst,),
        in_specs=[p_spec(a) for a in prompt] + [s_spec(a) for a in sample] + [c_spec(a) for a in consts],
        out_specs=[pl.BlockSpec((tm * ROW_TILES, LANES), lambda i: (i, 0)),
                   pl.BlockSpec((tm, D_MODEL), lambda i: (i, 0)),
                   pl.BlockSpec((SUBLANES, tm), lambda i: (0, i)),
                   pl.BlockSpec((SUBLANES, tm), lambda i: (0, i))],
        out_shape=[jax.ShapeDtypeStruct((t_all * ROW_TILES, LANES), F32),
                   jax.ShapeDtypeStruct((t_all, D_MODEL), F32),
                   jax.ShapeDtypeStruct((SUBLANES, t_all), jnp.int32),
                   jax.ShapeDtypeStruct((SUBLANES, t_all), F32)],
        compiler_params=_cparams(("arbitrary",)),
        name="mixer_out",
    )(*prompt, *sample, *consts)


def _deinterleave_kernel(w_ref, g_ref, l_ref, t_ref):
    half = t_ref.shape[1] // 2
    for j in range(t_ref.shape[0]):
        rows = slice(j * LANES, (j + 1) * LANES)
        t_ref[j] = w_ref[0, rows, :].T
        g_ref[0, rows, :] = t_ref[j, pl.ds(0, half, stride=2), :].T.astype(BF16)
        l_ref[0, rows, :] = t_ref[j, pl.ds(1, half, stride=2), :].T.astype(BF16)


DEINTERLEAVE_ROWS = 512


def _deinterleave_w1(w_e1):
    e, d, f2 = w_e1.shape
    rows = DEINTERLEAVE_ROWS
    spec = pl.BlockSpec((1, rows, f2 // 2), lambda i, c: (i, c, 0))
    return pl.pallas_call(
        _deinterleave_kernel,
        grid=(e, d // rows),
        in_specs=[pl.BlockSpec((1, rows, f2), lambda i, c: (i, c, 0))],
        out_specs=[spec, spec],
        out_shape=[jax.ShapeDtypeStruct((e, d, f2 // 2), BF16)] * 2,
        scratch_shapes=[pltpu.VMEM((rows // LANES, f2, LANES), F32)],
        compiler_params=_cparams(("arbitrary", "arbitrary")),
        name="deinterleave_w1",
    )(w_e1)


def _routing_plan(top_e, n_tokens):
    a = n_tokens * TOP_K
    mb, tc = MOE_BLOCK, COMBINE_TILE
    flat_e = top_e.reshape(-1)
    order = jnp.argsort(flat_e).astype(jnp.int32)
    experts = jnp.arange(N_EXPERTS, dtype=jnp.int32)
    counts = jnp.sum((flat_e[:, None] == experts[None, :]).astype(jnp.int32), axis=0)
    padded = (counts + mb - 1) // mb * mb
    start = jnp.cumsum(counts) - counts
    pend = jnp.cumsum(padded)
    pstart = pend - padded
    n_blocks = -(-(a + N_EXPERTS * (mb - 1)) // mb)
    block_first = jnp.arange(n_blocks, dtype=jnp.int32) * mb
    block_e = jnp.minimum(jnp.sum((pend[None, :] <= block_first[:, None]).astype(jnp.int32), axis=1),
                          N_EXPERTS - 1).astype(jnp.int32)
    n_used = (pend[-1] // mb).astype(jnp.int32).reshape(1)
    e_p = jnp.repeat(block_e, mb)
    off = jnp.arange(n_blocks * mb, dtype=jnp.int32) - pstart[e_p]
    valid = off < counts[e_p]
    asg = order[jnp.clip(start[e_p] + off, 0, a - 1)]
    t, k = asg // TOP_K, asg % TOP_K
    row_real = ((t // tc) * TOP_K + k) * tc + t % tc
    row_pad = a + (pstart[e_p] - start[e_p]) + (off - counts[e_p])
    tok = jnp.where(valid, t, 0)
    row = jnp.where(valid, row_real, row_pad)
    idx = jnp.stack([tok.reshape(n_blocks, mb), row.reshape(n_blocks, mb)], axis=1)
    extra = jnp.stack([jnp.zeros((mb,), jnp.int32), n_blocks * mb + jnp.arange(mb, dtype=jnp.int32)])[None]
    return jnp.concatenate([idx, extra], axis=0), block_e, n_used


IDX_RING = 4


def _row_tile(r):
    start = r * ROW_TILES
    return pl.ds(start if isinstance(r, int) else pl.multiple_of(start, ROW_TILES), ROW_TILES)


def _expert_kernel(be_ref, nused_ref, idx_hbm, x_hbm, w1g_ref, w1l_ref, b1g_ref, b1l_ref, w2_ref, b2_ref,
                   y_hbm, xbuf, ybuf, idx_smem, gsem, ssem, isem, *, n_blocks):
    i = pl.program_id(0)
    n_used = nused_ref[0]
    mb = MOE_BLOCK
    rows = mb * ROW_TILES
    cur, other = i % 2, (i + 1) % 2

    def ring(block):
        return (block + IDX_RING) % IDX_RING

    def idx_copy(block):
        src = jnp.where(block < 0, n_blocks, jnp.minimum(block, n_blocks))
        return pltpu.make_async_copy(idx_hbm.at[src], idx_smem.at[ring(block)], isem.at[ring(block)])

    def gather_copy(r, tok, s):
        return pltpu.make_async_copy(x_hbm.at[_row_tile(tok), :], xbuf.at[s, _row_tile(r), :], gsem.at[s])

    def scatter_copy(r, row, s):
        return pltpu.make_async_copy(ybuf.at[s, _row_tile(r), :], y_hbm.at[_row_tile(row), :], ssem.at[s])

    def gathered(s):
        return pltpu.make_async_copy(x_hbm.at[pl.ds(0, rows), :], xbuf.at[s], gsem.at[s])

    def scattered(s):
        return pltpu.make_async_copy(ybuf.at[s], y_hbm.at[pl.ds(0, rows), :], ssem.at[s])

    def issue_rolled(copy, block, column, s):
        def body(r, carry):
            copy(r, idx_smem[ring(block), column, r], s).start()
            return carry
        lax.fori_loop(0, mb, body, 0)

    @pl.when(i == 0)
    def _():
        idx_copy(i).start()
        idx_copy(i - 1).start()
        idx_copy(i + 1).start()
        idx_copy(i).wait()
        idx_copy(i - 1).wait()
        issue_rolled(gather_copy, i, 0, cur)
        ybuf[other] = jnp.zeros((rows, LANES), F32)

    @pl.when(i < n_used)
    def _():
        idx_copy(i + 1).wait()
        idx_copy(i + 2).start()
        gathered(cur).wait()

        @pl.when(i >= 1)
        def _():
            scattered(cur).wait()

        for r in range(mb):
            gather_copy(r, idx_smem[ring(i + 1), 0, r], other).start(priority=r % 2)
        for r in range(mb):
            scatter_copy(r, idx_smem[ring(i - 1), 1, r], other).start(priority=r % 2)
        x = jnp.concatenate([xbuf[cur, pl.ds(s, mb, stride=ROW_TILES), :] for s in range(ROW_TILES)],
                            axis=1).astype(BF16)
        glu = jnp.minimum(_dot(x, w1g_ref[0]) + b1g_ref[0], SWIGLU_LIMIT)
        lin = jnp.clip(_dot(x, w1l_ref[0]) + b1l_ref[0], -SWIGLU_LIMIT, SWIGLU_LIMIT)
        act = glu * jax.nn.sigmoid(SWIGLU_ALPHA * glu) * (lin + 1.0)
        y = _dot(act.astype(BF16), w2_ref[0]) + b2_ref[0]
        for s in range(ROW_TILES):
            ybuf[cur, pl.ds(s, mb, stride=ROW_TILES), :] = y[:, s * LANES:(s + 1) * LANES]

    @pl.when(i == n_used - 1)
    def _():
        issue_rolled(scatter_copy, i, 1, cur)
        scattered(other).wait()
        scattered(cur).wait()
        gathered(other).wait()
        idx_copy(i + 2).wait()

    @pl.when(i >= n_used)
    def _():
        ybuf[cur] = jnp.zeros((rows, LANES), F32)
        fill = pltpu.make_async_copy(ybuf.at[cur], y_hbm.at[pl.ds(pl.multiple_of(i * rows, rows), rows), :],
                                     ssem.at[cur])
        fill.start()
        fill.wait()


def _experts(x1r, idx, block_e, n_used, w1g, w1l, b_e1, w_e2, b_e2):
    n_blocks = idx.shape[0] - 1
    mb = MOE_BLOCK
    rows = mb * ROW_TILES
    b1g, b1l = b_e1[:, None, 0::2], b_e1[:, None, 1::2]
    w2 = w_e2.astype(BF16)
    b2 = b_e2[:, None, :]
    wspec = pl.BlockSpec((1, D_MODEL, D_EXPERT), lambda i, be, nu: (be[i], 0, 0))
    bspec = pl.BlockSpec((1, 1, D_EXPERT), lambda i, be, nu: (be[i], 0, 0))
    return pl.pallas_call(
        functools.partial(_expert_kernel, n_blocks=n_blocks),
        grid_spec=pltpu.PrefetchScalarGridSpec(
            num_scalar_prefetch=2,
            grid=(n_blocks,),
            in_specs=[pl.BlockSpec(memory_space=pl.ANY),
                      pl.BlockSpec(memory_space=pl.ANY),
                      wspec, wspec, bspec, bspec,
                      pl.BlockSpec((1, D_EXPERT, D_MODEL), lambda i, be, nu: (be[i], 0, 0)),
                      pl.BlockSpec((1, 1, D_MODEL), lambda i, be, nu: (be[i], 0, 0))],
            out_specs=pl.BlockSpec(memory_space=pl.ANY),
            scratch_shapes=[pltpu.VMEM((2, rows, LANES), F32),
                            pltpu.VMEM((2, rows, LANES), F32),
                            pltpu.SMEM((IDX_RING, 2, mb), jnp.int32),
                            pltpu.SemaphoreType.DMA((2,)),
                            pltpu.SemaphoreType.DMA((2,)),
                            pltpu.SemaphoreType.DMA((IDX_RING,))]),
        out_shape=jax.ShapeDtypeStruct(((n_blocks + 1) * rows, LANES), F32),
        compiler_params=_cparams(("arbitrary",)),
        name="experts",
    )(block_e, n_used, idx, x1r, w1g, w1l, b1g, b1l, w2, b2)


def _combine_kernel(y_ref, gate_ref, resid_ref, g_ref, b_ref, op_ref, os_ref, *, n_prompt_tiles):
    tm = COMBINE_TILE
    z = resid_ref[...]
    gate = gate_ref[...]
    for kk in range(TOP_K):
        y_k = jnp.concatenate(
            [y_ref[pl.ds(kk * tm * ROW_TILES + s, tm, stride=ROW_TILES), :] for s in range(ROW_TILES)], axis=1)
        z = z + y_k * gate[:, kk:kk + 1]
    out = _layer_norm(z, g_ref[...], b_ref[...])

    @pl.when(pl.program_id(0) < n_prompt_tiles)
    def _():
        op_ref[...] = out

    @pl.when(pl.program_id(0) >= n_prompt_tiles)
    def _():
        os_ref[...] = out


def _combine(ys, top_g, resid, ln2_g, ln2_b, n_prompt):
    t_all = resid.shape[0]
    tm = COMBINE_TILE
    assert n_prompt % tm == 0 and (t_all - n_prompt) % tm == 0
    npt, n_tiles = n_prompt // tm, t_all // tm
    return pl.pallas_call(
        functools.partial(_combine_kernel, n_prompt_tiles=npt),
        grid=(n_tiles,),
        in_specs=[pl.BlockSpec((tm * TOP_K * ROW_TILES, LANES), lambda i: (i, 0)),
                  pl.BlockSpec((tm, top_g.shape[1]), lambda i: (i, 0)),
                  pl.BlockSpec((tm, D_MODEL), lambda i: (i, 0)),
                  pl.BlockSpec((1, D_MODEL), lambda i: (0, 0)),
                  pl.BlockSpec((1, D_MODEL), lambda i: (0, 0))],
        out_specs=[pl.BlockSpec((tm, D_MODEL), lambda i: (jnp.minimum(i, npt - 1), 0)),
                   pl.BlockSpec((tm, D_MODEL), lambda i: (jnp.maximum(i - npt, 0), 0))],
        out_shape=[jax.ShapeDtypeStruct((n_prompt, D_MODEL), F32),
                   jax.ShapeDtypeStruct((t_all - n_prompt, D_MODEL), F32)],
        compiler_params=_cparams(("arbitrary",)),
        name="combine_ln2",
    )(ys, top_g, resid, ln2_g.reshape(1, -1), ln2_b.reshape(1, -1))


def kernel(x_prompt, x_sample, cache_att_k, cache_att_v, state_ret, p_prompt, p_sample, w_in, ret_gn_w, ret_gn_b, w_out, ln1_g, ln1_b, w_router, b_router, w_e1, b_e1, w_e2, b_e2, w_pl_gate, b_pl_gate, w_pl_proj, ln2_g, ln2_b):
    assert w_in.shape[0] == DEPTH == 1
    B, S, _ = x_prompt.shape
    DB, T, _ = x_sample.shape
    H, D = ATT_HEADS, ATT_HEAD_DIM
    l = 0
    w_in_b = w_in[l].astype(BF16)

    pos_p = jnp.arange(S, dtype=F32)
    pd, pt = _in_projection(x_prompt, w_in_b, pos_p, PROJ_NAMES, ("k_a", "v_a"), TOKEN_TILE, False)
    zero_state = jnp.zeros((B, 2, RET_QK // 2, RET_V // 2), F32)
    o_ret_p, st_p = _retention(pd["q_r"], pd["k_r"], pd["v_r"], zero_state, RET_CHUNK, RET_CHUNK)
    o_att_p = _attention_prompt(pd["q_a"], pd["k_a"], pd["v_a"])
    keep = min(MAX_WINDOW, S)
    as_cache = lambda a: a.reshape(B, H, D, S).transpose(0, 3, 1, 2)[None, :, S - keep:]
    new_k_p, new_v_p = as_cache(pt["k_a"]), as_cache(pt["v_a"])
    new_st_p = _blockdiag_to_state(st_p)[None]

    ts = DB * T
    by_step = lambda a: a.transpose(1, 0, 2).reshape(ts, a.shape[-1])
    xs = by_step(x_sample)
    pos_s = jnp.repeat(PAST_LEN + jnp.arange(T, dtype=F32), DB)
    sd, st = _in_projection(xs[None], w_in_b, pos_s, ("g_r", "q_a", "k_a", "v_a"),
                            ("q_r", "k_r", "v_r", "k_a", "v_a"), DB, True)
    state_t = jnp.transpose(state_ret[l], (1, 2, 3, 0))
    o_ret_t, state_new = _retention_sample(st["q_r"], st["k_r"], st["v_r"], state_t)
    o_ret_s = o_ret_t.transpose(0, 2, 1).reshape(ts, RET_V)
    by_batch = lambda a: a[0].reshape(T, DB, -1).transpose(1, 0, 2)
    cache_t = lambda c: jnp.transpose(c[l], (0, 2, 3, 1))
    o_att_s = _attention_sample(by_batch(sd["q_a"]), by_batch(sd["k_a"]), by_batch(sd["v_a"]),
                                cache_t(cache_att_k), cache_t(cache_att_v))
    as_new = lambda a: a.reshape(T, H, D, DB).transpose(3, 0, 1, 2)[None]
    new_k_s, new_v_s = as_new(st["k_a"]), as_new(st["v_a"])
    new_st_s = jnp.transpose(state_new, (3, 0, 1, 2)).astype(state_ret.dtype)[None]

    tp = B * S
    flat2 = lambda a: a.reshape(-1, a.shape[-1])
    prompt = (flat2(o_ret_p), flat2(pd["g_r"]), flat2(o_att_p), flat2(x_prompt), flat2(p_prompt[l]))
    sample = (o_ret_s, sd["g_r"][0], by_step(o_att_s), xs, by_step(p_sample[l]))
    x1r, resid, tope, topg = _mixer(prompt, sample, (ret_gn_w[l], ret_gn_b[l], w_out[l], ln1_g[l], ln1_b[l],
                                                     w_router[l], b_router[l], w_pl_gate[l], b_pl_gate[l],
                                                     w_pl_proj[l]))

    t_all = tp + ts
    idx, block_e, n_used = _routing_plan(tope[:TOP_K].T, t_all)
    w1g, w1l = _deinterleave_w1(w_e1[l])
    ys = _experts(x1r, idx, block_e, n_used, w1g, w1l, b_e1[l], w_e2[l], b_e2[l])
    y_p, y_s = _combine(ys, topg.T, resid, ln2_g[l], ln2_b[l], tp)
    return (y_p.reshape(B, S, D_MODEL), y_s.reshape(T, DB, D_MODEL).transpose(1, 0, 2),
            new_k_p, new_v_p, new_st_p, new_k_s, new_v_s, new_st_s)
```

```python
import functools

import numpy as np
import jax
import jax.numpy as jnp
from jax import lax
from jax.experimental import pallas as pl
from jax.experimental.pallas import tpu as pltpu

F32 = jnp.float32
BF16 = jnp.bfloat16

D_MODEL = 1024
DEPTH = 1
PAST_LEN = 8192
RET_HEADS = 8
RET_QK_DIM = 32
RET_V_DIM = 64
RET_CHUNK = 128
RET_ROPE_BASE = 10000.0
ATT_HEADS = 8
ATT_HEAD_DIM = 64
ROT_DIMS = ATT_HEAD_DIM // 4
ROPE_THETA = 500000.0
DILATED_PATTERNS = ((128, 1), (512, 4), (2048, 16))
MAX_WINDOW = 2048
RET_QK = RET_HEADS * RET_QK_DIM
RET_V = RET_HEADS * RET_V_DIM
ATT_W = ATT_HEADS * ATT_HEAD_DIM
IN_COLS = RET_QK * 2 + RET_V * 2 + ATT_W * 3
N_EXPERTS = 32
TOP_K = 4
D_EXPERT = D_MODEL
SWIGLU_ALPHA = 1.702
SWIGLU_LIMIT = 7.0
PLE_DIM = 256
LN_EPS = 1e-5
GN_EPS = 1e-6
DEEPNORM_ALPHA = (2 * DEPTH) ** 0.25
NEG_INF = -1e30

LANES = 128
SUBLANES = 8
ROW_TILES = D_MODEL // LANES
VMEM_LIMIT = 56 * 1024 * 1024

TOKEN_TILE = 512
MOE_BLOCK = 256
COMBINE_TILE = 512
ATT_BLOCK = 128
ATT_GROUP = 16


def _dot(a, b):
    return jnp.dot(a, b, preferred_element_type=F32)


def _dot_nt(a, b):
    return lax.dot_general(a, b, (((1,), (1,)), ((), ())), preferred_element_type=F32)


def _dot_tn(a, b):
    return lax.dot_general(a, b, (((0,), (0,)), ((), ())), preferred_element_type=F32)


def _cparams(sem):
    return pltpu.CompilerParams(dimension_semantics=sem, vmem_limit_bytes=VMEM_LIMIT)


def _rotary_tables(pos, inv_freq, head_dim, n_heads):
    half = inv_freq.shape[0]
    ang = pos.astype(F32)[:, None] * inv_freq[None, :]
    cos, sin = jnp.cos(ang), jnp.sin(ang)
    rest = head_dim - 2 * half
    n = pos.shape[0]
    cos_h = jnp.concatenate([cos, cos, jnp.ones((n, rest), F32)], axis=1)
    sin_h = jnp.concatenate([-sin, sin, jnp.zeros((n, rest), F32)], axis=1)
    return jnp.tile(cos_h, (1, n_heads)), jnp.tile(sin_h, (1, n_heads))


def _ret_inv_freq():
    return 1.0 / (RET_ROPE_BASE ** jnp.linspace(0.0, 1.0, RET_QK_DIM // 2, dtype=F32))


def _att_inv_freq():
    return ROPE_THETA ** (-jnp.arange(0, ROT_DIMS, 2, dtype=F32) / ROT_DIMS)


def _ret_log_decay():
    return jnp.log(1.0 - 2.0 ** (-5.0 - jnp.arange(RET_HEADS, dtype=F32)))


def _retention_tables(c_true, c_pad):
    lg = _ret_log_decay()
    idx = jnp.arange(c_pad, dtype=F32)
    live = idx < c_true
    diff = idx[:, None] - idx[None, :]
    decay = jnp.where(diff[None] >= 0, jnp.exp(jnp.maximum(diff, 0.0)[None] * lg[:, None, None]), 0.0)
    decay = jnp.where(live[None, :, None] & live[None, None, :], decay, 0.0)
    cross = jnp.exp((idx + 1.0)[:, None] * lg[None, :])
    cross = jnp.where(live[:, None], cross, 0.0)
    kdec = jnp.exp((c_true - 1.0 - idx)[:, None] * lg[None, :])
    kdec = jnp.where(live[:, None], kdec, 0.0)
    sdec = jnp.exp(c_true * lg)
    hh = RET_HEADS // 2
    decay = decay.reshape(2, hh, c_pad, c_pad)
    cross = jnp.repeat(cross, RET_V_DIM, axis=1).reshape(c_pad, 2, hh * RET_V_DIM).transpose(1, 0, 2)
    kdec = jnp.repeat(kdec, RET_QK_DIM, axis=1).reshape(c_pad, 2, hh * RET_QK_DIM).transpose(1, 0, 2)
    sdec = jnp.broadcast_to(jnp.repeat(sdec, RET_V_DIM)[None, :], (hh * RET_QK_DIM, RET_V)) \
        .reshape(hh * RET_QK_DIM, 2, hh * RET_V_DIM).transpose(1, 0, 2)
    return decay, cross, kdec, sdec


def _rotate(h, cos, sin, half, period):
    outs = []
    for j in range(h.shape[1] // LANES):
        blk = h[:, j * LANES:(j + 1) * LANES]
        lane = lax.broadcasted_iota(jnp.int32, blk.shape, 1)
        first = (lane % period) < half
        partner = jnp.where(first, pltpu.roll(blk, LANES - half, 1), pltpu.roll(blk, half, 1))
        outs.append(blk * cos[:, j * LANES:(j + 1) * LANES] + partner * sin[:, j * LANES:(j + 1) * LANES])
    return jnp.concatenate(outs, axis=1)


PROJ_NAMES = ("q_r", "k_r", "v_r", "g_r", "q_a", "k_a", "v_a")
PROJ_WIDTHS = dict(zip(PROJ_NAMES, (RET_QK, RET_QK, RET_V, RET_V, ATT_W, ATT_W, ATT_W)))


def _inproj_kernel(x_ref, w_ref, cr_ref, sr_ref, ca_ref, sa_ref, *out_refs, direct, transposed):
    x = x_ref[0].astype(BF16)
    cr, sr, ca, sa = cr_ref[...], sr_ref[...], ca_ref[...], sa_ref[...]
    finish = {
        "q_r": lambda h: _rotate(h, cr, sr, RET_QK_DIM // 2, RET_QK_DIM),
        "k_r": lambda h: _rotate(h, cr, sr, RET_QK_DIM // 2, RET_QK_DIM) * (RET_QK_DIM ** -0.5),
        "q_a": lambda h: _rotate(h, ca, sa, ROT_DIMS // 2, ATT_HEAD_DIM) * (ATT_HEAD_DIM ** -0.5),
        "k_a": lambda h: _rotate(h, ca, sa, ROT_DIMS // 2, ATT_HEAD_DIM),
    }
    refs = dict(zip([("d", n) for n in direct] + [("t", n) for n in transposed], out_refs))
    c = 0
    for name in PROJ_NAMES:
        width = PROJ_WIDTHS[name]
        if name in direct or name in transposed:
            h = _dot(x, w_ref[:, c:c + width])
            h = finish.get(name, lambda v: v)(h)
            if name in direct:
                refs[("d", name)][0] = h
            if name in transposed:
                refs[("t", name)][0] = h.T
        c += width


def _in_projection(x, w_in_b, pos, direct, transposed, tm, tile_major):
    B, S, _ = x.shape
    assert S % tm == 0 and (B == 1 or not tile_major)
    cr, sr = _rotary_tables(pos, _ret_inv_freq(), RET_QK_DIM, RET_HEADS)
    ca, sa = _rotary_tables(pos, _att_inv_freq(), ATT_HEAD_DIM, ATT_HEADS)
    tab = lambda w: pl.BlockSpec((tm, w), lambda s, b: (s, 0))
    specs = [pl.BlockSpec((1, tm, PROJ_WIDTHS[n]), lambda s, b: (b, s, 0)) for n in direct]
    shapes = [jax.ShapeDtypeStruct((B, S, PROJ_WIDTHS[n]), F32) for n in direct]
    for n in transposed:
        w = PROJ_WIDTHS[n]
        if tile_major:
            specs.append(pl.BlockSpec((1, w, tm), lambda s, b: (s, 0, 0)))
            shapes.append(jax.ShapeDtypeStruct((S // tm, w, tm), F32))
        else:
            specs.append(pl.BlockSpec((1, w, tm), lambda s, b: (b, 0, s)))
            shapes.append(jax.ShapeDtypeStruct((B, w, S), F32))
    outs = pl.pallas_call(
        functools.partial(_inproj_kernel, direct=direct, transposed=transposed),
        grid=(S // tm, B),
        in_specs=[pl.BlockSpec((1, tm, D_MODEL), lambda s, b: (b, s, 0)),
                  pl.BlockSpec((D_MODEL, IN_COLS), lambda s, b: (0, 0)),
                  tab(RET_QK), tab(RET_QK), tab(ATT_W), tab(ATT_W)],
        out_specs=specs,
        out_shape=shapes,
        compiler_params=_cparams(("arbitrary", "arbitrary")),
        name="in_projection",
    )(x, w_in_b, cr, sr, ca, sa)
    return dict(zip(direct, outs[:len(direct)])), dict(zip(transposed, outs[len(direct):]))


RET_GROUP = 4


def _retention_kernel(q_ref, k_ref, v_ref, s0_ref, dec_ref, cross_ref, kdec_ref, sdec_ref,
                      o_ref, so_ref, upd_ref, st_ref, *, chunk, n_chunks):
    hh = RET_HEADS // 2
    qk_w, v_w = hh * RET_QK_DIM, hh * RET_V_DIM
    lane_q = lax.broadcasted_iota(jnp.int32, (chunk, qk_w), 1) // RET_QK_DIM
    lane_v = lax.broadcasted_iota(jnp.int32, (chunk, v_w), 1) // RET_V_DIM
    blockdiag = (lax.broadcasted_iota(jnp.int32, (qk_w, v_w), 0) // RET_QK_DIM
                 == lax.broadcasted_iota(jnp.int32, (qk_w, v_w), 1) // RET_V_DIM)
    cross, kdec, sdec = cross_ref[0], kdec_ref[0], sdec_ref[0]
    rows = lambda ci: pl.ds(ci * chunk, chunk)

    group = min(RET_GROUP, n_chunks)
    for c0 in range(0, n_chunks, group):
        cs = range(c0, min(c0 + group, n_chunks))
        inner = {}
        for ci in cs:
            q, kb = q_ref[0, rows(ci), :], k_ref[0, rows(ci), :].astype(BF16)
            for h in range(hh):
                qm = jnp.where(lane_q == h, q, 0.0).astype(BF16)
                inner[ci, h] = (_dot_nt(qm, kb) * dec_ref[0, h]).astype(BF16)
        for ci in cs:
            k, vb = k_ref[0, rows(ci), :], v_ref[0, rows(ci), :].astype(BF16)
            o = jnp.zeros((chunk, v_w), F32)
            for h in range(hh):
                o = jnp.where(lane_v == h, _dot(inner[ci, h], vb), o)
            o_ref[0, rows(ci), :] = o
            upd_ref[ci] = jnp.where(blockdiag, _dot_tn((k * kdec).astype(BF16), vb), 0.0)

    state = s0_ref[0, 0]
    for ci in range(n_chunks):
        st_ref[ci] = state.astype(BF16)
        state = state * sdec + upd_ref[ci]
    so_ref[0, 0] = state

    for ci in range(n_chunks):
        o_ref[0, rows(ci), :] += _dot(q_ref[0, rows(ci), :].astype(BF16), st_ref[ci]) * cross


def _retention(q, k, v, state_bd, c_true, c_pad):
    B, S, _ = q.shape
    hh = RET_HEADS // 2
    qk_w, v_w = hh * RET_QK_DIM, hh * RET_V_DIM
    decay, cross, kdec, sdec = _retention_tables(c_true, c_pad)
    kern = functools.partial(_retention_kernel, chunk=c_pad, n_chunks=S // c_pad)
    return pl.pallas_call(
        kern,
        grid=(B, 2),
        in_specs=[pl.BlockSpec((1, S, qk_w), lambda b, g: (b, 0, g)),
                  pl.BlockSpec((1, S, qk_w), lambda b, g: (b, 0, g)),
                  pl.BlockSpec((1, S, v_w), lambda b, g: (b, 0, g)),
                  pl.BlockSpec((1, 1, qk_w, v_w), lambda b, g: (b, g, 0, 0)),
                  pl.BlockSpec((1, hh, c_pad, c_pad), lambda b, g: (g, 0, 0, 0)),
                  pl.BlockSpec((1, c_pad, v_w), lambda b, g: (g, 0, 0)),
                  pl.BlockSpec((1, c_pad, qk_w), lambda b, g: (g, 0, 0)),
                  pl.BlockSpec((1, qk_w, v_w), lambda b, g: (g, 0, 0))],
        out_specs=[pl.BlockSpec((1, S, v_w), lambda b, g: (b, 0, g)),
                   pl.BlockSpec((1, 1, qk_w, v_w), lambda b, g: (b, g, 0, 0))],
        out_shape=[jax.ShapeDtypeStruct((B, S, RET_V), F32),
                   jax.ShapeDtypeStruct((B, 2, qk_w, v_w), F32)],
        scratch_shapes=[pltpu.VMEM((S // c_pad, qk_w, v_w), F32),
                        pltpu.VMEM((S // c_pad, qk_w, v_w), BF16)],
        compiler_params=_cparams(("arbitrary", "arbitrary")),
        name="retention",
    )(q, k, v, state_bd, decay, cross, kdec, sdec)


def _blockdiag_to_state(bd):
    B = bd.shape[0]
    hh = RET_HEADS // 2
    s = bd.reshape(B, 2, hh, RET_QK_DIM, hh, RET_V_DIM)
    s = jnp.stack([s[:, :, h, :, h, :] for h in range(hh)], axis=2)
    return s.reshape(B, RET_HEADS, RET_QK_DIM, RET_V_DIM)


def _retention_sample_kernel(gam_ref, q_ref, k_ref, v_ref, s_ref, o_ref, so_ref, *, steps):
    h = pl.program_id(0)
    gam = [gam_ref[h, p] for p in range(steps + 1)]
    q = [q_ref[t] for t in range(steps)]
    k = [k_ref[t] for t in range(steps)]
    v = [v_ref[t] for t in range(steps)]
    o = [jnp.zeros(v[0].shape, F32) for _ in range(steps)]
    for d in range(RET_QK_DIM):
        s_d = s_ref[0, d]
        new = s_d * gam[steps]
        for t in range(steps):
            o[t] = o[t] + q[t][d:d + 1, :] * s_d
            new = new + (k[t][d:d + 1, :] * gam[steps - 1 - t]) * v[t]
        so_ref[0, d] = new
    for i in range(steps):
        o[i] = o[i] * gam[i + 1]
        for j in range(i + 1):
            qk = jnp.sum(q[i] * k[j], axis=0, keepdims=True) * gam[i - j]
            o[i] = o[i] + qk * v[j]
        o_ref[i] = o[i]


def _retention_sample(q_t, k_t, v_t, state):
    steps, _, nb = q_t.shape
    gam = jnp.exp(_ret_log_decay()[:, None] * jnp.arange(steps + 1, dtype=F32)[None, :])
    qk_spec = pl.BlockSpec((steps, RET_QK_DIM, nb), lambda h: (0, h, 0))
    v_spec = pl.BlockSpec((steps, RET_V_DIM, nb), lambda h: (0, h, 0))
    s_spec = pl.BlockSpec((1, RET_QK_DIM, RET_V_DIM, nb), lambda h: (h, 0, 0, 0))
    return pl.pallas_call(
        functools.partial(_retention_sample_kernel, steps=steps),
        grid=(RET_HEADS,),
        in_specs=[pl.BlockSpec(memory_space=pltpu.SMEM), qk_spec, qk_spec, v_spec, s_spec],
        out_specs=[v_spec, s_spec],
        out_shape=[jax.ShapeDtypeStruct(v_t.shape, F32), jax.ShapeDtypeStruct(state.shape, F32)],
        compiler_params=_cparams(("arbitrary",)),
        name="retention_sample",
    )(gam, q_t, k_t, v_t, state)


def _attn_prompt_kernel(q_ref, k_ref, v_ref, o_ref, o_ref_p, lse_ref, *, seq):
    n = ATT_BLOCK
    head0 = lax.broadcasted_iota(jnp.int32, (n, LANES), 1) < ATT_HEAD_DIM
    head0_kv = lax.broadcasted_iota(jnp.int32, (2 * n, LANES), 1) < ATT_HEAD_DIM
    a = lax.broadcasted_iota(jnp.int32, (n, 2 * n), 0)
    j = lax.broadcasted_iota(jnp.int32, (n, 2 * n), 1)
    bias_pc = jnp.where(((j < n) & (j >= a)) | ((j >= n) & (j - n <= a)), 0.0, NEG_INF)
    bias_c = bias_pc[:, n:]
    bias_first = jnp.where(j < n, NEG_INF, bias_pc)

    def rows(start, d):
        if d > 1:
            return pl.ds(start, n, stride=d)
        return pl.ds(start if isinstance(start, int) else pl.multiple_of(start, n), n)

    def blocks(p_idx, d, specs):
        loaded = []
        for cur, prev, has_prev in specs:
            q = q_ref[0, rows(cur, d), :]
            if has_prev is False:
                kk, vv = k_ref[0, rows(cur, d), :], v_ref[0, rows(cur, d), :]
                bias, hkv = bias_c, head0
            else:
                kk = jnp.concatenate([k_ref[0, rows(prev, d), :], k_ref[0, rows(cur, d), :]], axis=0)
                vv = jnp.concatenate([v_ref[0, rows(prev, d), :], v_ref[0, rows(cur, d), :]], axis=0)
                bias = bias_pc if has_prev is True else jnp.where(has_prev, bias_pc, bias_first)
                hkv = head0_kv
            loaded.append((q, kk.astype(BF16), vv, bias, hkv))
        scores = []
        for q, kb, vv, bias, hkv in loaded:
            for h in range(2):
                mine = head0 if h == 0 else jnp.logical_not(head0)
                scores.append(_dot_nt(jnp.where(mine, q, 0.0).astype(BF16), kb) + bias)
        probs, maxes = [], []
        for s in scores:
            m = jnp.max(s, axis=1, keepdims=True)
            probs.append(jnp.exp(s - m).astype(BF16))
            maxes.append(m)
        pvs = []
        for u, (q, kb, vv, bias, hkv) in enumerate(loaded):
            for h in range(2):
                mine_kv = hkv if h == 0 else jnp.logical_not(hkv)
                pvs.append(_dot(probs[2 * u + h], jnp.where(mine_kv, vv, 1.0).astype(BF16)))
        for u, (cur, prev, has_prev) in enumerate(specs):
            pv0, pv1 = pvs[2 * u], pvs[2 * u + 1]
            acc_u = jnp.where(head0, pv0, pv1)
            l_u = pltpu.roll(jnp.where(head0, pv1, pv0), ATT_HEAD_DIM, 1)
            m_u = jnp.where(head0, maxes[2 * u], maxes[2 * u + 1])
            o_ref_p[p_idx, rows(cur, d), :] = acc_u / l_u
            lse_ref[p_idx, rows(cur, d), :] = m_u + jnp.log(l_u)

    G = ATT_GROUP
    for p_idx, (window, d) in enumerate(DILATED_PATTERNS):
        nb = (seq // d) // n
        assert (d * nb) % G == 0 and (nb % G == 0 or G % nb == 0)

        def group(g, carry, p_idx=p_idx, d=d, nb=nb):
            specs = []
            for jj in range(G):
                if nb > G:
                    first = g * G
                    r = 0 if d == 1 else first // nb
                    c = first % nb + jj
                    specs.append((c * (n * d) + r, jnp.maximum(c - 1, 0) * (n * d) + r,
                                  True if jj > 0 else c > 0))
                else:
                    r = g * (G // nb) + jj // nb
                    c = jj % nb
                    specs.append((c * (n * d) + r, (c - 1) * (n * d) + r, c > 0))
            blocks(p_idx, d, specs)
            return carry

        lax.fori_loop(0, d * nb // G, group, 0)

    def merge(c, carry):
        rs = pl.ds(pl.multiple_of(c * n, n), n)
        lse = [lse_ref[p, rs, :] for p in range(len(DILATED_PATTERNS))]
        top = functools.reduce(jnp.maximum, lse)
        w = [jnp.exp(x - top) for x in lse]
        num = functools.reduce(lambda x, y: x + y, [w[p] * o_ref_p[p, rs, :] for p in range(len(w))])
        o_ref[0, rs, :] = num / functools.reduce(lambda x, y: x + y, w)
        return carry

    lax.fori_loop(0, seq // n, merge, 0)


def _attention_prompt(q, k, v):
    B, S, _ = q.shape
    spec = pl.BlockSpec((1, S, LANES), lambda b, g: (b, 0, g))
    return pl.pallas_call(
        functools.partial(_attn_prompt_kernel, seq=S),
        grid=(B, ATT_W // LANES),
        in_specs=[spec, spec, spec],
        out_specs=spec,
        out_shape=jax.ShapeDtypeStruct((B, S, ATT_W), F32),
        scratch_shapes=[pltpu.VMEM((len(DILATED_PATTERNS), S, LANES), F32)] * 2,
        compiler_params=_cparams(("arbitrary", "arbitrary")),
        name="attention_prompt",
    )(q, k, v)


def _attn_sample_kernel(q_ref, kn_ref, vn_ref, kc_ref, vc_ref, cc_ref, cn_ref, o_ref, *, steps):
    q = q_ref[0].astype(BF16)
    width = q.shape[1]
    kc = kc_ref[0].reshape(width, -1).astype(BF16)
    vc = vc_ref[0].reshape(width, -1).astype(BF16)
    cnt_c, cnt_n = cc_ref[...], cn_ref[...]
    s_c = jnp.where(cnt_c > 0.0, _dot(q, kc), NEG_INF)
    s_n = jnp.where(cnt_n > 0.0, _dot_nt(q, kn_ref[0].astype(BF16)), NEG_INF)
    m = jnp.maximum(s_c.max(axis=1, keepdims=True), s_n.max(axis=1, keepdims=True))
    p_c = cnt_c * jnp.exp(s_c - m)
    p_n = cnt_n * jnp.exp(s_n - m)
    den = p_c.sum(axis=1, keepdims=True) + p_n.sum(axis=1, keepdims=True)
    num = _dot_nt(p_c.astype(BF16), vc) + _dot(p_n.astype(BF16), vn_ref[0].astype(BF16))
    rows = lax.broadcasted_iota(jnp.int32, num.shape, 0)
    lanes = lax.broadcasted_iota(jnp.int32, num.shape, 1)
    out = jnp.where(rows % ATT_HEADS == lanes // ATT_HEAD_DIM, num / den, 0.0)
    for t in range(steps):
        o_ref[0, t:t + 1, :] = jnp.sum(out[t * ATT_HEADS:(t + 1) * ATT_HEADS], axis=0, keepdims=True)


def _sample_key_counts(dec_seq, buf, new_pad):
    t_q = np.repeat(np.arange(dec_seq), ATT_HEADS)[:, None]
    back = buf + t_q - np.arange(buf)[None, :]
    cnt_c = np.zeros(back.shape, np.float32)
    for w, d in DILATED_PATTERNS:
        cnt_c += ((back % d == 0) & (back >= d) & (back <= w)).astype(np.float32)
    back = t_q - np.arange(new_pad)[None, :]
    cnt_n = np.zeros(back.shape, np.float32)
    for w, d in DILATED_PATTERNS:
        cnt_n += ((back >= 0) & (back % d == 0) & (back <= w)).astype(np.float32)
    cnt_n *= (np.arange(new_pad) < dec_seq)[None, :]
    return jnp.asarray(cnt_c), jnp.asarray(cnt_n)


def _attention_sample(q, k_new, v_new, cache_k_t, cache_v_t):
    B, T, width = q.shape
    _, H, D, buf = cache_k_t.shape
    new_pad = SUBLANES
    cnt_c, cnt_n = _sample_key_counts(T, buf, new_pad)
    q_rows = (q.reshape(B, T, 1, H, D) * jnp.eye(H, dtype=F32)[None, None, :, :, None]).reshape(B, T * H, width)
    pad = lambda x: jnp.pad(x, ((0, 0), (0, new_pad - T), (0, 0)))
    new_spec = pl.BlockSpec((1, new_pad, width), lambda b: (b, 0, 0))
    cache_spec = pl.BlockSpec((1, H, D, buf), lambda b: (b, 0, 0, 0))
    const = lambda x: pl.BlockSpec(x.shape, lambda b: (0, 0))
    return pl.pallas_call(
        functools.partial(_attn_sample_kernel, steps=T),
        grid=(B,),
        in_specs=[pl.BlockSpec((1, T * H, width), lambda b: (b, 0, 0)), new_spec, new_spec,
                  cache_spec, cache_spec, const(cnt_c), const(cnt_n)],
        out_specs=pl.BlockSpec((1, T, width), lambda b: (b, 0, 0)),
        out_shape=jax.ShapeDtypeStruct((B, T, width), F32),
        compiler_params=_cparams(("arbitrary",)),
        name="attention_sample",
    )(q_rows, pad(k_new), pad(v_new), cache_k_t, cache_v_t, cnt_c, cnt_n)


def _layer_norm(y, g, b):
    mu = jnp.mean(y, axis=-1, keepdims=True)
    d = y - mu
    var = jnp.mean(d * d, axis=-1, keepdims=True)
    return d * lax.rsqrt(var + LN_EPS) * g + b


def _mixer_kernel(oret_p, g_p, oatt_p, x_p, pe_p, oret_s, g_s, oatt_s, x_s, pe_s,
                  avg_ref, gnw_ref, gnb_ref, wor_ref, woa_ref, ln1g_ref, ln1b_ref,
                  wg_ref, bg_ref, wp_ref, wr_ref, br_ref,
                  x1r_ref, resid_ref, tope_ref, topg_ref, *, n_prompt_tiles):
    def tile(oret_ref, g_ref, oatt_ref, x_ref, pe_ref):
        o_ret, g, o_att, x, pe = oret_ref[...], g_ref[...], oatt_ref[...], x_ref[...], pe_ref[...]
        avg = avg_ref[...]

        def head_mean(z):
            hi = z.astype(BF16)
            lo = (z - hi.astype(F32)).astype(BF16)
            return _dot(hi, avg) + _dot(lo, avg)

        d = o_ret - head_mean(o_ret)
        o_n = d * lax.rsqrt(head_mean(d * d) + GN_EPS) * gnw_ref[...] + gnb_ref[...]
        ret = g * jax.nn.sigmoid(g) * o_n
        mix = _dot(ret.astype(BF16), wor_ref[...]) + _dot(o_att.astype(BF16), woa_ref[...])
        x1 = _layer_norm(DEEPNORM_ALPHA * x + mix, ln1g_ref[...], ln1b_ref[...])
        x1b = x1.astype(BF16)
        ple = jax.nn.sigmoid(_dot(x1b, wg_ref[...]) + bg_ref[...]) * _dot(pe.astype(BF16), wp_ref[...])
        resid_ref[...] = DEEPNORM_ALPHA * x1 + ple
        tm = x1.shape[0]
        for s in range(ROW_TILES):
            x1r_ref[pl.ds(s, tm, stride=ROW_TILES), :] = x1[:, s * LANES:(s + 1) * LANES]

        logits = lax.dot_general(wr_ref[...], x1, (((1,), (1,)), ((), ())), preferred_element_type=F32,
                                 precision=lax.Precision.HIGHEST) + br_ref[...]
        row = lax.broadcasted_iota(jnp.int32, logits.shape, 0).astype(F32)
        work = logits
        vals, idxs = [], []
        for _ in range(TOP_K):
            m = jnp.max(work, axis=0, keepdims=True)
            idx = jnp.min(jnp.where(work == m, row, float(N_EXPERTS)), axis=0, keepdims=True)
            vals.append(m)
            idxs.append(idx)
            work = jnp.where(row == idx, -jnp.inf, work)
        exps = [jnp.exp(v - vals[0]) for v in vals]
        den = exps[0]
        for e in exps[1:]:
            den = den + e
        fill = [jnp.zeros_like(den)] * (SUBLANES - TOP_K)
        tope_ref[...] = jnp.concatenate(idxs + fill, axis=0).astype(jnp.int32)
        topg_ref[...] = jnp.concatenate([e / den for e in exps] + fill, axis=0)

    @pl.when(pl.program_id(0) < n_prompt_tiles)
    def _():
        tile(oret_p, g_p, oatt_p, x_p, pe_p)

    @pl.when(pl.program_id(0) >= n_prompt_tiles)
    def _():
        tile(oret_s, g_s, oatt_s, x_s, pe_s)


def _mixer(prompt, sample, weights):
    (gn_w, gn_b, w_out, ln1_g, ln1_b, w_router, b_router, w_pl_gate, b_pl_gate, w_pl_proj) = weights
    tp, ts = prompt[3].shape[0], sample[3].shape[0]
    tm = TOKEN_TILE
    assert tp % tm == 0 and ts % tm == 0
    npt, nst = tp // tm, ts // tm
    t_all = tp + ts
    avg = jnp.asarray(np.kron(np.eye(RET_HEADS), np.full((RET_V_DIM, RET_V_DIM), 1.0 / RET_V_DIM)), BF16)
    w_out_b = w_out.astype(BF16)
    row = lambda v: v.reshape(1, -1)
    consts = [avg, row(gn_w), row(gn_b), w_out_b[:RET_V], w_out_b[RET_V:], row(ln1_g), row(ln1_b),
              w_pl_gate.astype(BF16), row(b_pl_gate), w_pl_proj.astype(BF16), w_router.T, b_router.reshape(-1, 1)]
    p_spec = lambda a: pl.BlockSpec((tm, a.shape[1]), lambda i: (jnp.minimum(i, npt - 1), 0))
    s_spec = lambda a: pl.BlockSpec((tm, a.shape[1]), lambda i: (jnp.maximum(i - npt, 0), 0))
    c_spec = lambda a: pl.BlockSpec(a.shape, lambda i: (0, 0))
    return pl.pallas_call(
        functools.partial(_mixer_kernel, n_prompt_tiles=npt),
        grid=(npt + nst,),
        in_specs=[p_spec(a) for a in prompt] + [s_spec(a) for a in sample] + [c_spec(a) for a in consts],
        out_specs=[pl.BlockSpec((tm * ROW_TILES, LANES), lambda i: (i, 0)),
                   pl.BlockSpec((tm, D_MODEL), lambda i: (i, 0)),
                   pl.BlockSpec((SUBLANES, tm), lambda i: (0, i)),
                   pl.BlockSpec((SUBLANES, tm), lambda i: (0, i))],
        out_shape=[jax.ShapeDtypeStruct((t_all * ROW_TILES, LANES), F32),
                   jax.ShapeDtypeStruct((t_all, D_MODEL), F32),
                   jax.ShapeDtypeStruct((SUBLANES, t_all), jnp.int32),
                   jax.ShapeDtypeStruct((SUBLANES, t_all), F32)],
        compiler_params=_cparams(("arbitrary",)),
        name="mixer_out",
    )(*prompt, *sample, *consts)


def _deinterleave_kernel(w_ref, g_ref, l_ref, t_ref):
    half = t_ref.shape[1] // 2
    for j in range(t_ref.shape[0]):
        rows = slice(j * LANES, (j + 1) * LANES)
        t_ref[j] = w_ref[0, rows, :].T
        g_ref[0, rows, :] = t_ref[j, pl.ds(0, half, stride=2), :].T.astype(BF16)
        l_ref[0, rows, :] = t_ref[j, pl.ds(1, half, stride=2), :].T.astype(BF16)


DEINTERLEAVE_ROWS = 512


def _deinterleave_w1(w_e1):
    e, d, f2 = w_e1.shape
    rows = DEINTERLEAVE_ROWS
    spec = pl.BlockSpec((1, rows, f2 // 2), lambda i, c: (i, c, 0))
    return pl.pallas_call(
        _deinterleave_kernel,
        grid=(e, d // rows),
        in_specs=[pl.BlockSpec((1, rows, f2), lambda i, c: (i, c, 0))],
        out_specs=[spec, spec],
        out_shape=[jax.ShapeDtypeStruct((e, d, f2 // 2), BF16)] * 2,
        scratch_shapes=[pltpu.VMEM((rows // LANES, f2, LANES), F32)],
        compiler_params=_cparams(("arbitrary", "arbitrary")),
        name="deinterleave_w1",
    )(w_e1)


def _routing_plan(top_e, n_tokens):
    a = n_tokens * TOP_K
    mb, tc = MOE_BLOCK, COMBINE_TILE
    flat_e = top_e.reshape(-1)
    order = jnp.argsort(flat_e).astype(jnp.int32)
    experts = jnp.arange(N_EXPERTS, dtype=jnp.int32)
    counts = jnp.sum((flat_e[:, None] == experts[None, :]).astype(jnp.int32), axis=0)
    padded = (counts + mb - 1) // mb * mb
    start = jnp.cumsum(counts) - counts
    pend = jnp.cumsum(padded)
    pstart = pend - padded
    n_blocks = -(-(a + N_EXPERTS * (mb - 1)) // mb)
    block_first = jnp.arange(n_blocks, dtype=jnp.int32) * mb
    block_e = jnp.minimum(jnp.sum((pend[None, :] <= block_first[:, None]).astype(jnp.int32), axis=1),
                          N_EXPERTS - 1).astype(jnp.int32)
    n_used = (pend[-1] // mb).astype(jnp.int32).reshape(1)
    first_b, count_b, start_b = pstart[block_e][:, None], counts[block_e][:, None], start[block_e][:, None]
    off = jnp.arange(n_blocks * mb, dtype=jnp.int32).reshape(n_blocks, mb) - first_b
    valid = off < count_b
    asg = order[jnp.clip(start_b + off, 0, a - 1)]
    t, k = asg // TOP_K, asg % TOP_K
    row_real = ((t // tc) * TOP_K + k) * tc + t % tc
    row_pad = a + (first_b - start_b) + (off - count_b)
    tok = jnp.where(valid, t, 0)
    row = jnp.where(valid, row_real, row_pad)
    idx = jnp.stack([tok, row], axis=1)
    extra = jnp.stack([jnp.zeros((mb,), jnp.int32), n_blocks * mb + jnp.arange(mb, dtype=jnp.int32)])[None]
    return jnp.concatenate([idx, extra], axis=0), block_e, n_used


IDX_RING = 4


def _row_tile(r):
    start = r * ROW_TILES
    return pl.ds(start if isinstance(r, int) else pl.multiple_of(start, ROW_TILES), ROW_TILES)


def _expert_kernel(be_ref, nused_ref, idx_hbm, x_hbm, w1g_ref, w1l_ref, b1g_ref, b1l_ref, w2_ref, b2_ref,
                   y_hbm, xbuf, ybuf, idx_smem, gsem, ssem, isem, *, n_blocks):
    i = pl.program_id(0)
    n_used = nused_ref[0]
    mb = MOE_BLOCK
    rows = mb * ROW_TILES
    cur, other = i % 2, (i + 1) % 2

    def ring(block):
        return (block + IDX_RING) % IDX_RING

    def idx_copy(block):
        src = jnp.where(block < 0, n_blocks, jnp.minimum(block, n_blocks))
        return pltpu.make_async_copy(idx_hbm.at[src], idx_smem.at[ring(block)], isem.at[ring(block)])

    def gather_copy(r, tok, s):
        return pltpu.make_async_copy(x_hbm.at[_row_tile(tok), :], xbuf.at[s, _row_tile(r), :], gsem.at[s])

    def scatter_copy(r, row, s):
        return pltpu.make_async_copy(ybuf.at[s, _row_tile(r), :], y_hbm.at[_row_tile(row), :], ssem.at[s])

    def gathered(s):
        return pltpu.make_async_copy(x_hbm.at[pl.ds(0, rows), :], xbuf.at[s], gsem.at[s])

    def scattered(s):
        return pltpu.make_async_copy(ybuf.at[s], y_hbm.at[pl.ds(0, rows), :], ssem.at[s])

    def issue_rolled(copy, block, column, s):
        def body(r, carry):
            copy(r, idx_smem[ring(block), column, r], s).start()
            return carry
        lax.fori_loop(0, mb, body, 0)

    @pl.when(i == 0)
    def _():
        idx_copy(i).start()
        idx_copy(i - 1).start()
        idx_copy(i + 1).start()
        idx_copy(i).wait()
        idx_copy(i - 1).wait()
        issue_rolled(gather_copy, i, 0, cur)
        ybuf[other] = jnp.zeros((rows, LANES), F32)

    @pl.when(i < n_used)
    def _():
        idx_copy(i + 1).wait()
        idx_copy(i + 2).start()
        gathered(cur).wait()

        @pl.when(i >= 1)
        def _():
            scattered(cur).wait()

        for r in range(mb):
            gather_copy(r, idx_smem[ring(i + 1), 0, r], other).start(priority=r % 2)
        for r in range(mb):
            scatter_copy(r, idx_smem[ring(i - 1), 1, r], other).start(priority=r % 2)
        x = jnp.concatenate([xbuf[cur, pl.ds(s, mb, stride=ROW_TILES), :] for s in range(ROW_TILES)],
                            axis=1).astype(BF16)
        glu = jnp.minimum(_dot(x, w1g_ref[0]) + b1g_ref[0], SWIGLU_LIMIT)
        lin = jnp.clip(_dot(x, w1l_ref[0]) + b1l_ref[0], -SWIGLU_LIMIT, SWIGLU_LIMIT)
        act = glu * jax.nn.sigmoid(SWIGLU_ALPHA * glu) * (lin + 1.0)
        y = _dot(act.astype(BF16), w2_ref[0]) + b2_ref[0]
        for s in range(ROW_TILES):
            ybuf[cur, pl.ds(s, mb, stride=ROW_TILES), :] = y[:, s * LANES:(s + 1) * LANES]

    @pl.when(i == n_used - 1)
    def _():
        issue_rolled(scatter_copy, i, 1, cur)
        scattered(other).wait()
        scattered(cur).wait()
        gathered(other).wait()
        idx_copy(i + 2).wait()

    @pl.when(i >= n_used)
    def _():
        ybuf[cur] = jnp.zeros((rows, LANES), F32)
        fill = pltpu.make_async_copy(ybuf.at[cur], y_hbm.at[pl.ds(pl.multiple_of(i * rows, rows), rows), :],
                                     ssem.at[cur])
        fill.start()
        fill.wait()


def _experts(x1r, idx, block_e, n_used, w1g, w1l, b_e1, w_e2, b_e2):
    n_blocks = idx.shape[0] - 1
    mb = MOE_BLOCK
    rows = mb * ROW_TILES
    b1g, b1l = b_e1[:, None, 0::2], b_e1[:, None, 1::2]
    w2 = w_e2.astype(BF16)
    b2 = b_e2[:, None, :]
    wspec = pl.BlockSpec((1, D_MODEL, D_EXPERT), lambda i, be, nu: (be[i], 0, 0))
    bspec = pl.BlockSpec((1, 1, D_EXPERT), lambda i, be, nu: (be[i], 0, 0))
    return pl.pallas_call(
        functools.partial(_expert_kernel, n_blocks=n_blocks),
        grid_spec=pltpu.PrefetchScalarGridSpec(
            num_scalar_prefetch=2,
            grid=(n_blocks,),
            in_specs=[pl.BlockSpec(memory_space=pl.ANY),
                      pl.BlockSpec(memory_space=pl.ANY),
                      wspec, wspec, bspec, bspec,
                      pl.BlockSpec((1, D_EXPERT, D_MODEL), lambda i, be, nu: (be[i], 0, 0)),
                      pl.BlockSpec((1, 1, D_MODEL), lambda i, be, nu: (be[i], 0, 0))],
            out_specs=pl.BlockSpec(memory_space=pl.ANY),
            scratch_shapes=[pltpu.VMEM((2, rows, LANES), F32),
                            pltpu.VMEM((2, rows, LANES), F32),
                            pltpu.SMEM((IDX_RING, 2, mb), jnp.int32),
                            pltpu.SemaphoreType.DMA((2,)),
                            pltpu.SemaphoreType.DMA((2,)),
                            pltpu.SemaphoreType.DMA((IDX_RING,))]),
        out_shape=jax.ShapeDtypeStruct(((n_blocks + 1) * rows, LANES), F32),
        compiler_params=_cparams(("arbitrary",)),
        name="experts",
    )(block_e, n_used, idx, x1r, w1g, w1l, b1g, b1l, w2, b2)


def _combine_kernel(y_ref, gate_ref, resid_ref, g_ref, b_ref, op_ref, os_ref, *, n_prompt_tiles):
    tm = COMBINE_TILE
    z = resid_ref[...]
    gate = gate_ref[...]
    for kk in range(TOP_K):
        y_k = jnp.concatenate(
            [y_ref[pl.ds(kk * tm * ROW_TILES + s, tm, stride=ROW_TILES), :] for s in range(ROW_TILES)], axis=1)
        z = z + y_k * gate[:, kk:kk + 1]
    out = _layer_norm(z, g_ref[...], b_ref[...])

    @pl.when(pl.program_id(0) < n_prompt_tiles)
    def _():
        op_ref[...] = out

    @pl.when(pl.program_id(0) >= n_prompt_tiles)
    def _():
        os_ref[...] = out


def _combine(ys, top_g, resid, ln2_g, ln2_b, n_prompt):
    t_all = resid.shape[0]
    tm = COMBINE_TILE
    assert n_prompt % tm == 0 and (t_all - n_prompt) % tm == 0
    npt, n_tiles = n_prompt // tm, t_all // tm
    return pl.pallas_call(
        functools.partial(_combine_kernel, n_prompt_tiles=npt),
        grid=(n_tiles,),
        in_specs=[pl.BlockSpec((tm * TOP_K * ROW_TILES, LANES), lambda i: (i, 0)),
                  pl.BlockSpec((tm, top_g.shape[1]), lambda i: (i, 0)),
                  pl.BlockSpec((tm, D_MODEL), lambda i: (i, 0)),
                  pl.BlockSpec((1, D_MODEL), lambda i: (0, 0)),
                  pl.BlockSpec((1, D_MODEL), lambda i: (0, 0))],
        out_specs=[pl.BlockSpec((tm, D_MODEL), lambda i: (jnp.minimum(i, npt - 1), 0)),
                   pl.BlockSpec((tm, D_MODEL), lambda i: (jnp.maximum(i - npt, 0), 0))],
        out_shape=[jax.ShapeDtypeStruct((n_prompt, D_MODEL), F32),
                   jax.ShapeDtypeStruct((t_all - n_prompt, D_MODEL), F32)],
        compiler_params=_cparams(("arbitrary",)),
        name="combine_ln2",
    )(ys, top_g, resid, ln2_g.reshape(1, -1), ln2_b.reshape(1, -1))


def kernel(x_prompt, x_sample, cache_att_k, cache_att_v, state_ret, p_prompt, p_sample, w_in, ret_gn_w, ret_gn_b, w_out, ln1_g, ln1_b, w_router, b_router, w_e1, b_e1, w_e2, b_e2, w_pl_gate, b_pl_gate, w_pl_proj, ln2_g, ln2_b):
    assert w_in.shape[0] == DEPTH == 1
    B, S, _ = x_prompt.shape
    DB, T, _ = x_sample.shape
    H, D = ATT_HEADS, ATT_HEAD_DIM
    l = 0
    w_in_b = w_in[l].astype(BF16)

    pos_p = jnp.arange(S, dtype=F32)
    pd, pt = _in_projection(x_prompt, w_in_b, pos_p, PROJ_NAMES, ("k_a", "v_a"), TOKEN_TILE, False)
    zero_state = jnp.zeros((B, 2, RET_QK // 2, RET_V // 2), F32)
    o_ret_p, st_p = _retention(pd["q_r"], pd["k_r"], pd["v_r"], zero_state, RET_CHUNK, RET_CHUNK)
    o_att_p = _attention_prompt(pd["q_a"], pd["k_a"], pd["v_a"])
    keep = min(MAX_WINDOW, S)
    as_cache = lambda a: a.reshape(B, H, D, S).transpose(0, 3, 1, 2)[None, :, S - keep:]
    new_k_p, new_v_p = as_cache(pt["k_a"]), as_cache(pt["v_a"])
    new_st_p = _blockdiag_to_state(st_p)[None]

    ts = DB * T
    by_step = lambda a: a.transpose(1, 0, 2).reshape(ts, a.shape[-1])
    xs = by_step(x_sample)
    pos_s = jnp.repeat(PAST_LEN + jnp.arange(T, dtype=F32), DB)
    sd, st = _in_projection(xs[None], w_in_b, pos_s, ("g_r", "q_a", "k_a", "v_a"),
                            ("q_r", "k_r", "v_r", "k_a", "v_a"), DB, True)
    state_t = jnp.transpose(state_ret[l], (1, 2, 3, 0))
    o_ret_t, state_new = _retention_sample(st["q_r"], st["k_r"], st["v_r"], state_t)
    o_ret_s = o_ret_t.transpose(0, 2, 1).reshape(ts, RET_V)
    by_batch = lambda a: a[0].reshape(T, DB, -1).transpose(1, 0, 2)
    cache_t = lambda c: jnp.transpose(c[l], (0, 2, 3, 1))
    o_att_s = _attention_sample(by_batch(sd["q_a"]), by_batch(sd["k_a"]), by_batch(sd["v_a"]),
                                cache_t(cache_att_k), cache_t(cache_att_v))
    as_new = lambda a: a.reshape(T, H, D, DB).transpose(3, 0, 1, 2)[None]
    new_k_s, new_v_s = as_new(st["k_a"]), as_new(st["v_a"])
    new_st_s = jnp.transpose(state_new, (3, 0, 1, 2)).astype(state_ret.dtype)[None]

    tp = B * S
    flat2 = lambda a: a.reshape(-1, a.shape[-1])
    prompt = (flat2(o_ret_p), flat2(pd["g_r"]), flat2(o_att_p), flat2(x_prompt), flat2(p_prompt[l]))
    sample = (o_ret_s, sd["g_r"][0], by_step(o_att_s), xs, by_step(p_sample[l]))
    x1r, resid, tope, topg = _mixer(prompt, sample, (ret_gn_w[l], ret_gn_b[l], w_out[l], ln1_g[l], ln1_b[l],
                                                     w_router[l], b_router[l], w_pl_gate[l], b_pl_gate[l],
                                                     w_pl_proj[l]))

    t_all = tp + ts
    idx, block_e, n_used = _routing_plan(tope[:TOP_K].T, t_all)
    w1g, w1l = _deinterleave_w1(w_e1[l])
    ys = _experts(x1r, idx, block_e, n_used, w1g, w1l, b_e1[l], w_e2[l], b_e2[l])
    y_p, y_s = _combine(ys, topg.T, resid, ln2_g[l], ln2_b[l], tp)
    return (y_p.reshape(B, S, D_MODEL), y_s.reshape(T, DB, D_MODEL).transpose(1, 0, 2),
            new_k_p, new_v_p, new_st_p, new_k_s, new_v_s, new_st_s)
```

```python
import functools

import numpy as np
import jax
import jax.numpy as jnp
from jax import lax
from jax.experimental import pallas as pl
from jax.experimental.pallas import tpu as pltpu

F32 = jnp.float32
BF16 = jnp.bfloat16

D_MODEL = 1024
DEPTH = 1
PAST_LEN = 8192
RET_HEADS = 8
RET_QK_DIM = 32
RET_V_DIM = 64
RET_CHUNK = 128
RET_ROPE_BASE = 10000.0
ATT_HEADS = 8
ATT_HEAD_DIM = 64
ROT_DIMS = ATT_HEAD_DIM // 4
ROPE_THETA = 500000.0
DILATED_PATTERNS = ((128, 1), (512, 4), (2048, 16))
MAX_WINDOW = 2048
RET_QK = RET_HEADS * RET_QK_DIM
RET_V = RET_HEADS * RET_V_DIM
ATT_W = ATT_HEADS * ATT_HEAD_DIM
IN_COLS = RET_QK * 2 + RET_V * 2 + ATT_W * 3
N_EXPERTS = 32
TOP_K = 4
D_EXPERT = D_MODEL
SWIGLU_ALPHA = 1.702
SWIGLU_LIMIT = 7.0
PLE_DIM = 256
LN_EPS = 1e-5
GN_EPS = 1e-6
DEEPNORM_ALPHA = (2 * DEPTH) ** 0.25
NEG_INF = -1e30

LANES = 128
SUBLANES = 8
ROW_TILES = D_MODEL // LANES
VMEM_LIMIT = 56 * 1024 * 1024

TOKEN_TILE = 512
MOE_BLOCK = 256
COMBINE_TILE = 512
ATT_BLOCK = 128
ATT_GROUP = 16


def _dot(a, b):
    return jnp.dot(a, b, preferred_element_type=F32)


def _dot_nt(a, b):
    return lax.dot_general(a, b, (((1,), (1,)), ((), ())), preferred_element_type=F32)


def _dot_tn(a, b):
    return lax.dot_general(a, b, (((0,), (0,)), ((), ())), preferred_element_type=F32)


def _cparams(sem):
    return pltpu.CompilerParams(dimension_semantics=sem, vmem_limit_bytes=VMEM_LIMIT)


def _rotary_tables(pos, inv_freq, head_dim, n_heads):
    half = inv_freq.shape[0]
    ang = pos.astype(F32)[:, None] * inv_freq[None, :]
    cos, sin = jnp.cos(ang), jnp.sin(ang)
    rest = head_dim - 2 * half
    n = pos.shape[0]
    cos_h = jnp.concatenate([cos, cos, jnp.ones((n, rest), F32)], axis=1)
    sin_h = jnp.concatenate([-sin, sin, jnp.zeros((n, rest), F32)], axis=1)
    return jnp.tile(cos_h, (1, n_heads)), jnp.tile(sin_h, (1, n_heads))


def _ret_inv_freq():
    return 1.0 / (RET_ROPE_BASE ** jnp.linspace(0.0, 1.0, RET_QK_DIM // 2, dtype=F32))


def _att_inv_freq():
    return ROPE_THETA ** (-jnp.arange(0, ROT_DIMS, 2, dtype=F32) / ROT_DIMS)


def _ret_log_decay():
    return jnp.log(1.0 - 2.0 ** (-5.0 - jnp.arange(RET_HEADS, dtype=F32)))


def _retention_tables(c_true, c_pad):
    lg = _ret_log_decay()
    idx = jnp.arange(c_pad, dtype=F32)
    live = idx < c_true
    diff = idx[:, None] - idx[None, :]
    decay = jnp.where(diff[None] >= 0, jnp.exp(jnp.maximum(diff, 0.0)[None] * lg[:, None, None]), 0.0)
    decay = jnp.where(live[None, :, None] & live[None, None, :], decay, 0.0)
    cross = jnp.exp((idx + 1.0)[:, None] * lg[None, :])
    cross = jnp.where(live[:, None], cross, 0.0)
    kdec = jnp.exp((c_true - 1.0 - idx)[:, None] * lg[None, :])
    kdec = jnp.where(live[:, None], kdec, 0.0)
    sdec = jnp.exp(c_true * lg)
    hh = RET_HEADS // 2
    decay = decay.reshape(2, hh, c_pad, c_pad)
    cross = jnp.repeat(cross, RET_V_DIM, axis=1).reshape(c_pad, 2, hh * RET_V_DIM).transpose(1, 0, 2)
    kdec = jnp.repeat(kdec, RET_QK_DIM, axis=1).reshape(c_pad, 2, hh * RET_QK_DIM).transpose(1, 0, 2)
    sdec = jnp.broadcast_to(jnp.repeat(sdec, RET_V_DIM)[None, :], (hh * RET_QK_DIM, RET_V)) \
        .reshape(hh * RET_QK_DIM, 2, hh * RET_V_DIM).transpose(1, 0, 2)
    return decay, cross, kdec, sdec


def _rotate(h, cos, sin, half, period):
    outs = []
    for j in range(h.shape[1] // LANES):
        blk = h[:, j * LANES:(j + 1) * LANES]
        lane = lax.broadcasted_iota(jnp.int32, blk.shape, 1)
        first = (lane % period) < half
        partner = jnp.where(first, pltpu.roll(blk, LANES - half, 1), pltpu.roll(blk, half, 1))
        outs.append(blk * cos[:, j * LANES:(j + 1) * LANES] + partner * sin[:, j * LANES:(j + 1) * LANES])
    return jnp.concatenate(outs, axis=1)


PROJ_NAMES = ("q_r", "k_r", "v_r", "g_r", "q_a", "k_a", "v_a")
PROJ_WIDTHS = dict(zip(PROJ_NAMES, (RET_QK, RET_QK, RET_V, RET_V, ATT_W, ATT_W, ATT_W)))


def _inproj_kernel(x_ref, w_ref, cr_ref, sr_ref, ca_ref, sa_ref, *out_refs, direct, transposed):
    x = x_ref[0].astype(BF16)
    cr, sr, ca, sa = cr_ref[...], sr_ref[...], ca_ref[...], sa_ref[...]
    finish = {
        "q_r": lambda h: _rotate(h, cr, sr, RET_QK_DIM // 2, RET_QK_DIM),
        "k_r": lambda h: _rotate(h, cr, sr, RET_QK_DIM // 2, RET_QK_DIM) * (RET_QK_DIM ** -0.5),
        "q_a": lambda h: _rotate(h, ca, sa, ROT_DIMS // 2, ATT_HEAD_DIM) * (ATT_HEAD_DIM ** -0.5),
        "k_a": lambda h: _rotate(h, ca, sa, ROT_DIMS // 2, ATT_HEAD_DIM),
    }
    refs = dict(zip([("d", n) for n in direct] + [("t", n) for n in transposed], out_refs))
    c = 0
    for name in PROJ_NAMES:
        width = PROJ_WIDTHS[name]
        if name in direct or name in transposed:
            h = _dot(x, w_ref[:, c:c + width])
            h = finish.get(name, lambda v: v)(h)
            if name in direct:
                refs[("d", name)][0] = h
            if name in transposed:
                refs[("t", name)][0] = h.T
        c += width


def _in_projection(x, w_in_b, pos, direct, transposed, tm, tile_major):
    B, S, _ = x.shape
    assert S % tm == 0 and (B == 1 or not tile_major)
    cr, sr = _rotary_tables(pos, _ret_inv_freq(), RET_QK_DIM, RET_HEADS)
    ca, sa = _rotary_tables(pos, _att_inv_freq(), ATT_HEAD_DIM, ATT_HEADS)
    tab = lambda w: pl.BlockSpec((tm, w), lambda s, b: (s, 0))
    specs = [pl.BlockSpec((1, tm, PROJ_WIDTHS[n]), lambda s, b: (b, s, 0)) for n in direct]
    shapes = [jax.ShapeDtypeStruct((B, S, PROJ_WIDTHS[n]), F32) for n in direct]
    for n in transposed:
        w = PROJ_WIDTHS[n]
        if tile_major:
            specs.append(pl.BlockSpec((1, w, tm), lambda s, b: (s, 0, 0)))
            shapes.append(jax.ShapeDtypeStruct((S // tm, w, tm), F32))
        else:
            specs.append(pl.BlockSpec((1, w, tm), lambda s, b: (b, 0, s)))
            shapes.append(jax.ShapeDtypeStruct((B, w, S), F32))
    outs = pl.pallas_call(
        functools.partial(_inproj_kernel, direct=direct, transposed=transposed),
        grid=(S // tm, B),
        in_specs=[pl.BlockSpec((1, tm, D_MODEL), lambda s, b: (b, s, 0)),
                  pl.BlockSpec((D_MODEL, IN_COLS), lambda s, b: (0, 0)),
                  tab(RET_QK), tab(RET_QK), tab(ATT_W), tab(ATT_W)],
        out_specs=specs,
        out_shape=shapes,
        compiler_params=_cparams(("arbitrary", "arbitrary")),
        name="in_projection",
    )(x, w_in_b, cr, sr, ca, sa)
    return dict(zip(direct, outs[:len(direct)])), dict(zip(transposed, outs[len(direct):]))


RET_GROUP = 4


def _retention_kernel(q_ref, k_ref, v_ref, s0_ref, dec_ref, cross_ref, kdec_ref, sdec_ref,
                      o_ref, so_ref, upd_ref, st_ref, *, chunk, n_chunks):
    hh = RET_HEADS // 2
    qk_w, v_w = hh * RET_QK_DIM, hh * RET_V_DIM
    lane_q = lax.broadcasted_iota(jnp.int32, (chunk, qk_w), 1) // RET_QK_DIM
    lane_v = lax.broadcasted_iota(jnp.int32, (chunk, v_w), 1) // RET_V_DIM
    blockdiag = (lax.broadcasted_iota(jnp.int32, (qk_w, v_w), 0) // RET_QK_DIM
                 == lax.broadcasted_iota(jnp.int32, (qk_w, v_w), 1) // RET_V_DIM)
    cross, kdec, sdec = cross_ref[0], kdec_ref[0], sdec_ref[0]
    rows = lambda ci: pl.ds(ci * chunk, chunk)

    group = min(RET_GROUP, n_chunks)
    for c0 in range(0, n_chunks, group):
        cs = range(c0, min(c0 + group, n_chunks))
        inner = {}
        for ci in cs:
            q, kb = q_ref[0, rows(ci), :], k_ref[0, rows(ci), :].astype(BF16)
            for h in range(hh):
                qm = jnp.where(lane_q == h, q, 0.0).astype(BF16)
                inner[ci, h] = (_dot_nt(qm, kb) * dec_ref[0, h]).astype(BF16)
        for ci in cs:
            k, vb = k_ref[0, rows(ci), :], v_ref[0, rows(ci), :].astype(BF16)
            o = jnp.zeros((chunk, v_w), F32)
            for h in range(hh):
                o = jnp.where(lane_v == h, _dot(inner[ci, h], vb), o)
            o_ref[0, rows(ci), :] = o
            upd_ref[ci] = jnp.where(blockdiag, _dot_tn((k * kdec).astype(BF16), vb), 0.0)

    state = s0_ref[0, 0]
    for ci in range(n_chunks):
        st_ref[ci] = state.astype(BF16)
        state = state * sdec + upd_ref[ci]
    so_ref[0, 0] = state

    for ci in range(n_chunks):
        o_ref[0, rows(ci), :] += _dot(q_ref[0, rows(ci), :].astype(BF16), st_ref[ci]) * cross


def _retention(q, k, v, state_bd, c_true, c_pad):
    B, S, _ = q.shape
    hh = RET_HEADS // 2
    qk_w, v_w = hh * RET_QK_DIM, hh * RET_V_DIM
    decay, cross, kdec, sdec = _retention_tables(c_true, c_pad)
    kern = functools.partial(_retention_kernel, chunk=c_pad, n_chunks=S // c_pad)
    return pl.pallas_call(
        kern,
        grid=(B, 2),
        in_specs=[pl.BlockSpec((1, S, qk_w), lambda b, g: (b, 0, g)),
                  pl.BlockSpec((1, S, qk_w), lambda b, g: (b, 0, g)),
                  pl.BlockSpec((1, S, v_w), lambda b, g: (b, 0, g)),
                  pl.BlockSpec((1, 1, qk_w, v_w), lambda b, g: (b, g, 0, 0)),
                  pl.BlockSpec((1, hh, c_pad, c_pad), lambda b, g: (g, 0, 0, 0)),
                  pl.BlockSpec((1, c_pad, v_w), lambda b, g: (g, 0, 0)),
                  pl.BlockSpec((1, c_pad, qk_w), lambda b, g: (g, 0, 0)),
                  pl.BlockSpec((1, qk_w, v_w), lambda b, g: (g, 0, 0))],
        out_specs=[pl.BlockSpec((1, S, v_w), lambda b, g: (b, 0, g)),
                   pl.BlockSpec((1, 1, qk_w, v_w), lambda b, g: (b, g, 0, 0))],
        out_shape=[jax.ShapeDtypeStruct((B, S, RET_V), F32),
                   jax.ShapeDtypeStruct((B, 2, qk_w, v_w), F32)],
        scratch_shapes=[pltpu.VMEM((S // c_pad, qk_w, v_w), F32),
                        pltpu.VMEM((S // c_pad, qk_w, v_w), BF16)],
        compiler_params=_cparams(("arbitrary", "arbitrary")),
        name="retention",
    )(q, k, v, state_bd, decay, cross, kdec, sdec)


def _blockdiag_to_state(bd):
    B = bd.shape[0]
    hh = RET_HEADS // 2
    s = bd.reshape(B, 2, hh, RET_QK_DIM, hh, RET_V_DIM)
    s = jnp.stack([s[:, :, h, :, h, :] for h in range(hh)], axis=2)
    return s.reshape(B, RET_HEADS, RET_QK_DIM, RET_V_DIM)


def _retention_sample_kernel(gam_ref, q_ref, k_ref, v_ref, s_ref, o_ref, so_ref, *, steps):
    h = pl.program_id(0)
    gam = [gam_ref[h, p] for p in range(steps + 1)]
    q = [q_ref[t] for t in range(steps)]
    k = [k_ref[t] for t in range(steps)]
    v = [v_ref[t] for t in range(steps)]
    o = [jnp.zeros(v[0].shape, F32) for _ in range(steps)]
    for d in range(RET_QK_DIM):
        s_d = s_ref[0, d]
        new = s_d * gam[steps]
        for t in range(steps):
            o[t] = o[t] + q[t][d:d + 1, :] * s_d
            new = new + (k[t][d:d + 1, :] * gam[steps - 1 - t]) * v[t]
        so_ref[0, d] = new
    for i in range(steps):
        o[i] = o[i] * gam[i + 1]
        for j in range(i + 1):
            qk = jnp.sum(q[i] * k[j], axis=0, keepdims=True) * gam[i - j]
            o[i] = o[i] + qk * v[j]
        o_ref[i] = o[i]


def _retention_sample(q_t, k_t, v_t, state):
    steps, _, nb = q_t.shape
    gam = jnp.exp(_ret_log_decay()[:, None] * jnp.arange(steps + 1, dtype=F32)[None, :])
    qk_spec = pl.BlockSpec((steps, RET_QK_DIM, nb), lambda h: (0, h, 0))
    v_spec = pl.BlockSpec((steps, RET_V_DIM, nb), lambda h: (0, h, 0))
    s_spec = pl.BlockSpec((1, RET_QK_DIM, RET_V_DIM, nb), lambda h: (h, 0, 0, 0))
    return pl.pallas_call(
        functools.partial(_retention_sample_kernel, steps=steps),
        grid=(RET_HEADS,),
        in_specs=[pl.BlockSpec(memory_space=pltpu.SMEM), qk_spec, qk_spec, v_spec, s_spec],
        out_specs=[v_spec, s_spec],
        out_shape=[jax.ShapeDtypeStruct(v_t.shape, F32), jax.ShapeDtypeStruct(state.shape, F32)],
        compiler_params=_cparams(("arbitrary",)),
        name="retention_sample",
    )(gam, q_t, k_t, v_t, state)


def _attn_prompt_kernel(q_ref, k_ref, v_ref, o_ref, o_ref_p, lse_ref, *, seq):
    n = ATT_BLOCK
    head0 = lax.broadcasted_iota(jnp.int32, (n, LANES), 1) < ATT_HEAD_DIM
    head0_kv = lax.broadcasted_iota(jnp.int32, (2 * n, LANES), 1) < ATT_HEAD_DIM
    a = lax.broadcasted_iota(jnp.int32, (n, 2 * n), 0)
    j = lax.broadcasted_iota(jnp.int32, (n, 2 * n), 1)
    bias_pc = jnp.where(((j < n) & (j >= a)) | ((j >= n) & (j - n <= a)), 0.0, NEG_INF)
    bias_c = bias_pc[:, n:]
    bias_first = jnp.where(j < n, NEG_INF, bias_pc)

    def rows(start, d):
        if d > 1:
            return pl.ds(start, n, stride=d)
        return pl.ds(start if isinstance(start, int) else pl.multiple_of(start, n), n)

    def blocks(p_idx, d, specs):
        loaded = []
        for cur, prev, has_prev in specs:
            q = q_ref[0, rows(cur, d), :]
            if has_prev is False:
                kk, vv = k_ref[0, rows(cur, d), :], v_ref[0, rows(cur, d), :]
                bias, hkv = bias_c, head0
            else:
                kk = jnp.concatenate([k_ref[0, rows(prev, d), :], k_ref[0, rows(cur, d), :]], axis=0)
                vv = jnp.concatenate([v_ref[0, rows(prev, d), :], v_ref[0, rows(cur, d), :]], axis=0)
                bias = bias_pc if has_prev is True else jnp.where(has_prev, bias_pc, bias_first)
                hkv = head0_kv
            loaded.append((q, kk.astype(BF16), vv, bias, hkv))
        scores = []
        for q, kb, vv, bias, hkv in loaded:
            for h in range(2):
                mine = head0 if h == 0 else jnp.logical_not(head0)
                scores.append(_dot_nt(jnp.where(mine, q, 0.0).astype(BF16), kb) + bias)
        probs, maxes = [], []
        for s in scores:
            m = jnp.max(s, axis=1, keepdims=True)
            probs.append(jnp.exp(s - m).astype(BF16))
            maxes.append(m)
        pvs = []
        for u, (q, kb, vv, bias, hkv) in enumerate(loaded):
            for h in range(2):
                mine_kv = hkv if h == 0 else jnp.logical_not(hkv)
                pvs.append(_dot(probs[2 * u + h], jnp.where(mine_kv, vv, 1.0).astype(BF16)))
        for u, (cur, prev, has_prev) in enumerate(specs):
            pv0, pv1 = pvs[2 * u], pvs[2 * u + 1]
            acc_u = jnp.where(head0, pv0, pv1)
            l_u = pltpu.roll(jnp.where(head0, pv1, pv0), ATT_HEAD_DIM, 1)
            m_u = jnp.where(head0, maxes[2 * u], maxes[2 * u + 1])
            o_ref_p[p_idx, rows(cur, d), :] = acc_u / l_u
            lse_ref[p_idx, rows(cur, d), :] = m_u + jnp.log(l_u)

    G = ATT_GROUP
    for p_idx, (window, d) in enumerate(DILATED_PATTERNS):
        nb = (seq // d) // n
        assert (d * nb) % G == 0 and (nb % G == 0 or G % nb == 0)

        def group(g, carry, p_idx=p_idx, d=d, nb=nb):
            specs = []
            for jj in range(G):
                if nb > G:
                    first = g * G
                    r = 0 if d == 1 else first // nb
                    c = first % nb + jj
                    specs.append((c * (n * d) + r, jnp.maximum(c - 1, 0) * (n * d) + r,
                                  True if jj > 0 else c > 0))
                else:
                    r = g * (G // nb) + jj // nb
                    c = jj % nb
                    specs.append((c * (n * d) + r, (c - 1) * (n * d) + r, c > 0))
            blocks(p_idx, d, specs)
            return carry

        lax.fori_loop(0, d * nb // G, group, 0)

    def merge(c, carry):
        rs = pl.ds(pl.multiple_of(c * n, n), n)
        lse = [lse_ref[p, rs, :] for p in range(len(DILATED_PATTERNS))]
        top = functools.reduce(jnp.maximum, lse)
        w = [jnp.exp(x - top) for x in lse]
        num = functools.reduce(lambda x, y: x + y, [w[p] * o_ref_p[p, rs, :] for p in range(len(w))])
        o_ref[0, rs, :] = num / functools.reduce(lambda x, y: x + y, w)
        return carry

    lax.fori_loop(0, seq // n, merge, 0)


def _attention_prompt(q, k, v):
    B, S, _ = q.shape
    spec = pl.BlockSpec((1, S, LANES), lambda b, g: (b, 0, g))
    return pl.pallas_call(
        functools.partial(_attn_prompt_kernel, seq=S),
        grid=(B, ATT_W // LANES),
        in_specs=[spec, spec, spec],
        out_specs=spec,
        out_shape=jax.ShapeDtypeStruct((B, S, ATT_W), F32),
        scratch_shapes=[pltpu.VMEM((len(DILATED_PATTERNS), S, LANES), F32)] * 2,
        compiler_params=_cparams(("arbitrary", "arbitrary")),
        name="attention_prompt",
    )(q, k, v)


def _attn_sample_kernel(q_ref, kn_ref, vn_ref, kc_ref, vc_ref, cc_ref, cn_ref, o_ref, *, steps):
    q = q_ref[0].astype(BF16)
    width = q.shape[1]
    kc = kc_ref[0].reshape(width, -1).astype(BF16)
    vc = vc_ref[0].reshape(width, -1).astype(BF16)
    cnt_c, cnt_n = cc_ref[...], cn_ref[...]
    s_c = jnp.where(cnt_c > 0.0, _dot(q, kc), NEG_INF)
    s_n = jnp.where(cnt_n > 0.0, _dot_nt(q, kn_ref[0].astype(BF16)), NEG_INF)
    m = jnp.maximum(s_c.max(axis=1, keepdims=True), s_n.max(axis=1, keepdims=True))
    p_c = cnt_c * jnp.exp(s_c - m)
    p_n = cnt_n * jnp.exp(s_n - m)
    den = p_c.sum(axis=1, keepdims=True) + p_n.sum(axis=1, keepdims=True)
    num = _dot_nt(p_c.astype(BF16), vc) + _dot(p_n.astype(BF16), vn_ref[0].astype(BF16))
    rows = lax.broadcasted_iota(jnp.int32, num.shape, 0)
    lanes = lax.broadcasted_iota(jnp.int32, num.shape, 1)
    out = jnp.where(rows % ATT_HEADS == lanes // ATT_HEAD_DIM, num / den, 0.0)
    for t in range(steps):
        o_ref[0, t:t + 1, :] = jnp.sum(out[t * ATT_HEADS:(t + 1) * ATT_HEADS], axis=0, keepdims=True)


def _sample_key_counts(dec_seq, buf, new_pad):
    t_q = np.repeat(np.arange(dec_seq), ATT_HEADS)[:, None]
    back = buf + t_q - np.arange(buf)[None, :]
    cnt_c = np.zeros(back.shape, np.float32)
    for w, d in DILATED_PATTERNS:
        cnt_c += ((back % d == 0) & (back >= d) & (back <= w)).astype(np.float32)
    back = t_q - np.arange(new_pad)[None, :]
    cnt_n = np.zeros(back.shape, np.float32)
    for w, d in DILATED_PATTERNS:
        cnt_n += ((back >= 0) & (back % d == 0) & (back <= w)).astype(np.float32)
    cnt_n *= (np.arange(new_pad) < dec_seq)[None, :]
    return jnp.asarray(cnt_c), jnp.asarray(cnt_n)


def _attention_sample(q, k_new, v_new, cache_k_t, cache_v_t):
    B, T, width = q.shape
    _, H, D, buf = cache_k_t.shape
    new_pad = SUBLANES
    cnt_c, cnt_n = _sample_key_counts(T, buf, new_pad)
    q_rows = (q.reshape(B, T, 1, H, D) * jnp.eye(H, dtype=F32)[None, None, :, :, None]).reshape(B, T * H, width)
    pad = lambda x: jnp.pad(x, ((0, 0), (0, new_pad - T), (0, 0)))
    new_spec = pl.BlockSpec((1, new_pad, width), lambda b: (b, 0, 0))
    cache_spec = pl.BlockSpec((1, H, D, buf), lambda b: (b, 0, 0, 0))
    const = lambda x: pl.BlockSpec(x.shape, lambda b: (0, 0))
    return pl.pallas_call(
        functools.partial(_attn_sample_kernel, steps=T),
        grid=(B,),
        in_specs=[pl.BlockSpec((1, T * H, width), lambda b: (b, 0, 0)), new_spec, new_spec,
                  cache_spec, cache_spec, const(cnt_c), const(cnt_n)],
        out_specs=pl.BlockSpec((1, T, width), lambda b: (b, 0, 0)),
        out_shape=jax.ShapeDtypeStruct((B, T, width), F32),
        compiler_params=_cparams(("arbitrary",)),
        name="attention_sample",
    )(q_rows, pad(k_new), pad(v_new), cache_k_t, cache_v_t, cnt_c, cnt_n)


def _layer_norm(y, g, b):
    mu = jnp.mean(y, axis=-1, keepdims=True)
    d = y - mu
    var = jnp.mean(d * d, axis=-1, keepdims=True)
    return d * lax.rsqrt(var + LN_EPS) * g + b


def _mixer_kernel(oret_p, g_p, oatt_p, x_p, pe_p, oret_s, g_s, oatt_s, x_s, pe_s,
                  avg_ref, gnw_ref, gnb_ref, wor_ref, woa_ref, ln1g_ref, ln1b_ref,
                  wg_ref, bg_ref, wp_ref, wr_ref, br_ref,
                  x1r_ref, resid_ref, tope_ref, topg_ref, *, n_prompt_tiles):
    def tile(oret_ref, g_ref, oatt_ref, x_ref, pe_ref):
        o_ret, g, o_att, x, pe = oret_ref[...], g_ref[...], oatt_ref[...], x_ref[...], pe_ref[...]
        avg = avg_ref[...]

        def head_mean(z):
            hi = z.astype(BF16)
            lo = (z - hi.astype(F32)).astype(BF16)
            return _dot(hi, avg) + _dot(lo, avg)

        d = o_ret - head_mean(o_ret)
        o_n = d * lax.rsqrt(head_mean(d * d) + GN_EPS) * gnw_ref[...] + gnb_ref[...]
        ret = g * jax.nn.sigmoid(g) * o_n
        mix = _dot(ret.astype(BF16), wor_ref[...]) + _dot(o_att.astype(BF16), woa_ref[...])
        x1 = _layer_norm(DEEPNORM_ALPHA * x + mix, ln1g_ref[...], ln1b_ref[...])
        x1b = x1.astype(BF16)
        ple = jax.nn.sigmoid(_dot(x1b, wg_ref[...]) + bg_ref[...]) * _dot(pe.astype(BF16), wp_ref[...])
        resid_ref[...] = DEEPNORM_ALPHA * x1 + ple
        tm = x1.shape[0]
        for s in range(ROW_TILES):
            x1r_ref[pl.ds(s, tm, stride=ROW_TILES), :] = x1[:, s * LANES:(s + 1) * LANES]

        logits = lax.dot_general(wr_ref[...], x1, (((1,), (1,)), ((), ())), preferred_element_type=F32,
                                 precision=lax.Precision.HIGHEST) + br_ref[...]
        row = lax.broadcasted_iota(jnp.int32, logits.shape, 0).astype(F32)
        work = logits
        vals, idxs = [], []
        for _ in range(TOP_K):
            m = jnp.max(work, axis=0, keepdims=True)
            idx = jnp.min(jnp.where(work == m, row, float(N_EXPERTS)), axis=0, keepdims=True)
            vals.append(m)
            idxs.append(idx)
            work = jnp.where(row == idx, -jnp.inf, work)
        exps = [jnp.exp(v - vals[0]) for v in vals]
        den = exps[0]
        for e in exps[1:]:
            den = den + e
        fill = [jnp.zeros_like(den)] * (SUBLANES - TOP_K)
        tope_ref[...] = jnp.concatenate(idxs + fill, axis=0).astype(jnp.int32)
        topg_ref[...] = jnp.concatenate([e / den for e in exps] + fill, axis=0)

    @pl.when(pl.program_id(0) < n_prompt_tiles)
    def _():
        tile(oret_p, g_p, oatt_p, x_p, pe_p)

    @pl.when(pl.program_id(0) >= n_prompt_tiles)
    def _():
        tile(oret_s, g_s, oatt_s, x_s, pe_s)


def _mixer(prompt, sample, weights):
    (gn_w, gn_b, w_out, ln1_g, ln1_b, w_router, b_router, w_pl_gate, b_pl_gate, w_pl_proj) = weights
    tp, ts = prompt[3].shape[0], sample[3].shape[0]
    tm = TOKEN_TILE
    assert tp % tm == 0 and ts % tm == 0
    npt, nst = tp // tm, ts // tm
    t_all = tp + ts
    avg = jnp.asarray(np.kron(np.eye(RET_HEADS), np.full((RET_V_DIM, RET_V_DIM), 1.0 / RET_V_DIM)), BF16)
    w_out_b = w_out.astype(BF16)
    row = lambda v: v.reshape(1, -1)
    consts = [avg, row(gn_w), row(gn_b), w_out_b[:RET_V], w_out_b[RET_V:], row(ln1_g), row(ln1_b),
              w_pl_gate.astype(BF16), row(b_pl_gate), w_pl_proj.astype(BF16), w_router.T, b_router.reshape(-1, 1)]
    p_spec = lambda a: pl.BlockSpec((tm, a.shape[1]), lambda i: (jnp.minimum(i, npt - 1), 0))
    s_spec = lambda a: pl.BlockSpec((tm, a.shape[1]), lambda i: (jnp.maximum(i - npt, 0), 0))
    c_spec = lambda a: pl.BlockSpec(a.shape, lambda i: (0, 0))
    return pl.pallas_call(
        functools.partial(_mixer_kernel, n_prompt_tiles=npt),
        grid=(npt + nst,),
        in_specs=[p_spec(a) for a in prompt] + [s_spec(a) for a in sample] + [c_spec(a) for a in consts],
        out_specs=[pl.BlockSpec((tm * ROW_TILES, LANES), lambda i: (i, 0)),
                   pl.BlockSpec((tm, D_MODEL), lambda i: (i, 0)),
                   pl.BlockSpec((SUBLANES, tm), lambda i: (0, i)),
                   pl.BlockSpec((SUBLANES, tm), lambda i: (0, i))],
        out_shape=[jax.ShapeDtypeStruct((t_all * ROW_TILES, LANES), F32),
                   jax.ShapeDtypeStruct((t_all, D_MODEL), F32),
                   jax.ShapeDtypeStruct((SUBLANES, t_all), jnp.int32),
                   jax.ShapeDtypeStruct((SUBLANES, t_all), F32)],
        compiler_params=_cparams(("arbitrary",)),
        name="mixer_out",
    )(*prompt, *sample, *consts)


def _deinterleave_kernel(w_ref, g_ref, l_ref, t_ref):
    half = t_ref.shape[1] // 2
    for j in range(t_ref.shape[0]):
        rows = slice(j * LANES, (j + 1) * LANES)
        t_ref[j] = w_ref[0, rows, :].T
        g_ref[0, rows, :] = t_ref[j, pl.ds(0, half, stride=2), :].T.astype(BF16)
        l_ref[0, rows, :] = t_ref[j, pl.ds(1, half, stride=2), :].T.astype(BF16)


DEINTERLEAVE_ROWS = 512


def _deinterleave_w1(w_e1):
    e, d, f2 = w_e1.shape
    rows = DEINTERLEAVE_ROWS
    spec = pl.BlockSpec((1, rows, f2 // 2), lambda i, c: (i, c, 0))
    return pl.pallas_call(
        _deinterleave_kernel,
        grid=(e, d // rows),
        in_specs=[pl.BlockSpec((1, rows, f2), lambda i, c: (i, c, 0))],
        out_specs=[spec, spec],
        out_shape=[jax.ShapeDtypeStruct((e, d, f2 // 2), BF16)] * 2,
        scratch_shapes=[pltpu.VMEM((rows // LANES, f2, LANES), F32)],
        compiler_params=_cparams(("arbitrary", "arbitrary")),
        name="deinterleave_w1",
    )(w_e1)


def _routing_plan(top_e, n_tokens):
    a = n_tokens * TOP_K
    mb, tc = MOE_BLOCK, COMBINE_TILE
    flat_e = top_e.reshape(-1)
    bits = max(a - 1, 1).bit_length()
    assert N_EXPERTS << bits < 2 ** 31
    keys = jnp.sort((flat_e << bits) | jnp.arange(a, dtype=jnp.int32))
    order = keys & ((1 << bits) - 1)
    experts = jnp.arange(N_EXPERTS, dtype=jnp.int32)
    counts = jnp.sum((flat_e[:, None] == experts[None, :]).astype(jnp.int32), axis=0)
    padded = (counts + mb - 1) // mb * mb
    start = jnp.cumsum(counts) - counts
    pend = jnp.cumsum(padded)
    pstart = pend - padded
    n_blocks = -(-(a + N_EXPERTS * (mb - 1)) // mb)
    block_first = jnp.arange(n_blocks, dtype=jnp.int32) * mb
    block_e = jnp.minimum(jnp.sum((pend[None, :] <= block_first[:, None]).astype(jnp.int32), axis=1),
                          N_EXPERTS - 1).astype(jnp.int32)
    n_used = (pend[-1] // mb).astype(jnp.int32).reshape(1)
    first_b, count_b, start_b = pstart[block_e][:, None], counts[block_e][:, None], start[block_e][:, None]
    off = jnp.arange(n_blocks * mb, dtype=jnp.int32).reshape(n_blocks, mb) - first_b
    valid = off < count_b
    asg = order[jnp.clip(start_b + off, 0, a - 1)]
    t, k = asg // TOP_K, asg % TOP_K
    row_real = ((t // tc) * TOP_K + k) * tc + t % tc
    row_pad = a + (first_b - start_b) + (off - count_b)
    tok = jnp.where(valid, t, 0)
    row = jnp.where(valid, row_real, row_pad)
    idx = jnp.stack([tok, row], axis=1)
    extra = jnp.stack([jnp.zeros((mb,), jnp.int32), n_blocks * mb + jnp.arange(mb, dtype=jnp.int32)])[None]
    return jnp.concatenate([idx, extra], axis=0), block_e, n_used


IDX_RING = 4


def _row_tile(r):
    start = r * ROW_TILES
    return pl.ds(start if isinstance(r, int) else pl.multiple_of(start, ROW_TILES), ROW_TILES)


def _expert_kernel(be_ref, nused_ref, idx_hbm, x_hbm, w1g_ref, w1l_ref, b1g_ref, b1l_ref, w2_ref, b2_ref,
                   y_hbm, xbuf, ybuf, idx_smem, gsem, ssem, isem, *, n_blocks):
    i = pl.program_id(0)
    n_used = nused_ref[0]
    mb = MOE_BLOCK
    rows = mb * ROW_TILES
    cur, other = i % 2, (i + 1) % 2

    def ring(block):
        return (block + IDX_RING) % IDX_RING

    def idx_copy(block):
        src = jnp.where(block < 0, n_blocks, jnp.minimum(block, n_blocks))
        return pltpu.make_async_copy(idx_hbm.at[src], idx_smem.at[ring(block)], isem.at[ring(block)])

    def gather_copy(r, tok, s):
        return pltpu.make_async_copy(x_hbm.at[_row_tile(tok), :], xbuf.at[s, _row_tile(r), :], gsem.at[s])

    def scatter_copy(r, row, s):
        return pltpu.make_async_copy(ybuf.at[s, _row_tile(r), :], y_hbm.at[_row_tile(row), :], ssem.at[s])

    def gathered(s):
        return pltpu.make_async_copy(x_hbm.at[pl.ds(0, rows), :], xbuf.at[s], gsem.at[s])

    def scattered(s):
        return pltpu.make_async_copy(ybuf.at[s], y_hbm.at[pl.ds(0, rows), :], ssem.at[s])

    def issue_rolled(copy, block, column, s):
        def body(r, carry):
            copy(r, idx_smem[ring(block), column, r], s).start()
            return carry
        lax.fori_loop(0, mb, body, 0)

    @pl.when(i == 0)
    def _():
        idx_copy(i).start()
        idx_copy(i - 1).start()
        idx_copy(i + 1).start()
        idx_copy(i).wait()
        idx_copy(i - 1).wait()
        issue_rolled(gather_copy, i, 0, cur)
        ybuf[other] = jnp.zeros((rows, LANES), F32)

    @pl.when(i < n_used)
    def _():
        idx_copy(i + 1).wait()
        idx_copy(i + 2).start()
        gathered(cur).wait()

        @pl.when(i >= 1)
        def _():
            scattered(cur).wait()

        for r in range(mb):
            gather_copy(r, idx_smem[ring(i + 1), 0, r], other).start(priority=r % 2)
        for r in range(mb):
            scatter_copy(r, idx_smem[ring(i - 1), 1, r], other).start(priority=r % 2)
        x = jnp.concatenate([xbuf[cur, pl.ds(s, mb, stride=ROW_TILES), :] for s in range(ROW_TILES)],
                            axis=1).astype(BF16)
        glu = jnp.minimum(_dot(x, w1g_ref[0]) + b1g_ref[0], SWIGLU_LIMIT)
        lin = jnp.clip(_dot(x, w1l_ref[0]) + b1l_ref[0], -SWIGLU_LIMIT, SWIGLU_LIMIT)
        act = glu * jax.nn.sigmoid(SWIGLU_ALPHA * glu) * (lin + 1.0)
        y = _dot(act.astype(BF16), w2_ref[0]) + b2_ref[0]
        for s in range(ROW_TILES):
            ybuf[cur, pl.ds(s, mb, stride=ROW_TILES), :] = y[:, s * LANES:(s + 1) * LANES]

    @pl.when(i == n_used - 1)
    def _():
        issue_rolled(scatter_copy, i, 1, cur)
        scattered(other).wait()
        scattered(cur).wait()
        gathered(other).wait()
        idx_copy(i + 2).wait()

    @pl.when(i >= n_used)
    def _():
        ybuf[cur] = jnp.zeros((rows, LANES), F32)
        fill = pltpu.make_async_copy(ybuf.at[cur], y_hbm.at[pl.ds(pl.multiple_of(i * rows, rows), rows), :],
                                     ssem.at[cur])
        fill.start()
        fill.wait()


def _experts(x1r, idx, block_e, n_used, w1g, w1l, b_e1, w_e2, b_e2):
    n_blocks = idx.shape[0] - 1
    mb = MOE_BLOCK
    rows = mb * ROW_TILES
    b1g, b1l = b_e1[:, None, 0::2], b_e1[:, None, 1::2]
    w2 = w_e2.astype(BF16)
    b2 = b_e2[:, None, :]
    wspec = pl.BlockSpec((1, D_MODEL, D_EXPERT), lambda i, be, nu: (be[i], 0, 0))
    bspec = pl.BlockSpec((1, 1, D_EXPERT), lambda i, be, nu: (be[i], 0, 0))
    return pl.pallas_call(
        functools.partial(_expert_kernel, n_blocks=n_blocks),
        grid_spec=pltpu.PrefetchScalarGridSpec(
            num_scalar_prefetch=2,
            grid=(n_blocks,),
            in_specs=[pl.BlockSpec(memory_space=pl.ANY),
                      pl.BlockSpec(memory_space=pl.ANY),
                      wspec, wspec, bspec, bspec,
                      pl.BlockSpec((1, D_EXPERT, D_MODEL), lambda i, be, nu: (be[i], 0, 0)),
                      pl.BlockSpec((1, 1, D_MODEL), lambda i, be, nu: (be[i], 0, 0))],
            out_specs=pl.BlockSpec(memory_space=pl.ANY),
            scratch_shapes=[pltpu.VMEM((2, rows, LANES), F32),
                            pltpu.VMEM((2, rows, LANES), F32),
                            pltpu.SMEM((IDX_RING, 2, mb), jnp.int32),
                            pltpu.SemaphoreType.DMA((2,)),
                            pltpu.SemaphoreType.DMA((2,)),
                            pltpu.SemaphoreType.DMA((IDX_RING,))]),
        out_shape=jax.ShapeDtypeStruct(((n_blocks + 1) * rows, LANES), F32),
        compiler_params=_cparams(("arbitrary",)),
        name="experts",
    )(block_e, n_used, idx, x1r, w1g, w1l, b1g, b1l, w2, b2)


def _combine_kernel(y_ref, gate_ref, resid_ref, g_ref, b_ref, op_ref, os_ref, *, n_prompt_tiles):
    tm = COMBINE_TILE
    z = resid_ref[...]
    gate = gate_ref[...]
    for kk in range(TOP_K):
        y_k = jnp.concatenate(
            [y_ref[pl.ds(kk * tm * ROW_TILES + s, tm, stride=ROW_TILES), :] for s in range(ROW_TILES)], axis=1)
        z = z + y_k * gate[:, kk:kk + 1]
    out = _layer_norm(z, g_ref[...], b_ref[...])

    @pl.when(pl.program_id(0) < n_prompt_tiles)
    def _():
        op_ref[...] = out

    @pl.when(pl.program_id(0) >= n_prompt_tiles)
    def _():
        os_ref[...] = out


def _combine(ys, top_g, resid, ln2_g, ln2_b, n_prompt):
    t_all = resid.shape[0]
    tm = COMBINE_TILE
    assert n_prompt % tm == 0 and (t_all - n_prompt) % tm == 0
    npt, n_tiles = n_prompt // tm, t_all // tm
    return pl.pallas_call(
        functools.partial(_combine_kernel, n_prompt_tiles=npt),
        grid=(n_tiles,),
        in_specs=[pl.BlockSpec((tm * TOP_K * ROW_TILES, LANES), lambda i: (i, 0)),
                  pl.BlockSpec((tm, top_g.shape[1]), lambda i: (i, 0)),
                  pl.BlockSpec((tm, D_MODEL), lambda i: (i, 0)),
                  pl.BlockSpec((1, D_MODEL), lambda i: (0, 0)),
                  pl.BlockSpec((1, D_MODEL), lambda i: (0, 0))],
        out_specs=[pl.BlockSpec((tm, D_MODEL), lambda i: (jnp.minimum(i, npt - 1), 0)),
                   pl.BlockSpec((tm, D_MODEL), lambda i: (jnp.maximum(i - npt, 0), 0))],
        out_shape=[jax.ShapeDtypeStruct((n_prompt, D_MODEL), F32),
                   jax.ShapeDtypeStruct((t_all - n_prompt, D_MODEL), F32)],
        compiler_params=_cparams(("arbitrary",)),
        name="combine_ln2",
    )(ys, top_g, resid, ln2_g.reshape(1, -1), ln2_b.reshape(1, -1))


def kernel(x_prompt, x_sample, cache_att_k, cache_att_v, state_ret, p_prompt, p_sample, w_in, ret_gn_w, ret_gn_b, w_out, ln1_g, ln1_b, w_router, b_router, w_e1, b_e1, w_e2, b_e2, w_pl_gate, b_pl_gate, w_pl_proj, ln2_g, ln2_b):
    assert w_in.shape[0] == DEPTH == 1
    B, S, _ = x_prompt.shape
    DB, T, _ = x_sample.shape
    H, D = ATT_HEADS, ATT_HEAD_DIM
    l = 0
    w_in_b = w_in[l].astype(BF16)

    pos_p = jnp.arange(S, dtype=F32)
    pd, pt = _in_projection(x_prompt, w_in_b, pos_p, PROJ_NAMES, ("k_a", "v_a"), TOKEN_TILE, False)
    zero_state = jnp.zeros((B, 2, RET_QK // 2, RET_V // 2), F32)
    o_ret_p, st_p = _retention(pd["q_r"], pd["k_r"], pd["v_r"], zero_state, RET_CHUNK, RET_CHUNK)
    o_att_p = _attention_prompt(pd["q_a"], pd["k_a"], pd["v_a"])
    keep = min(MAX_WINDOW, S)
    as_cache = lambda a: a.reshape(B, H, D, S).transpose(0, 3, 1, 2)[None, :, S - keep:]
    new_k_p, new_v_p = as_cache(pt["k_a"]), as_cache(pt["v_a"])
    new_st_p = _blockdiag_to_state(st_p)[None]

    ts = DB * T
    by_step = lambda a: a.transpose(1, 0, 2).reshape(ts, a.shape[-1])
    xs = by_step(x_sample)
    pos_s = jnp.repeat(PAST_LEN + jnp.arange(T, dtype=F32), DB)
    sd, st = _in_projection(xs[None], w_in_b, pos_s, ("g_r", "q_a", "k_a", "v_a"),
                            ("q_r", "k_r", "v_r", "k_a", "v_a"), DB, True)
    state_t = jnp.transpose(state_ret[l], (1, 2, 3, 0))
    o_ret_t, state_new = _retention_sample(st["q_r"], st["k_r"], st["v_r"], state_t)
    o_ret_s = o_ret_t.transpose(0, 2, 1).reshape(ts, RET_V)
    by_batch = lambda a: a[0].reshape(T, DB, -1).transpose(1, 0, 2)
    cache_t = lambda c: jnp.transpose(c[l], (0, 2, 3, 1))
    o_att_s = _attention_sample(by_batch(sd["q_a"]), by_batch(sd["k_a"]), by_batch(sd["v_a"]),
                                cache_t(cache_att_k), cache_t(cache_att_v))
    as_new = lambda a: a.reshape(T, H, D, DB).transpose(3, 0, 1, 2)[None]
    new_k_s, new_v_s = as_new(st["k_a"]), as_new(st["v_a"])
    new_st_s = jnp.transpose(state_new, (3, 0, 1, 2)).astype(state_ret.dtype)[None]

    tp = B * S
    flat2 = lambda a: a.reshape(-1, a.shape[-1])
    prompt = (flat2(o_ret_p), flat2(pd["g_r"]), flat2(o_att_p), flat2(x_prompt), flat2(p_prompt[l]))
    sample = (o_ret_s, sd["g_r"][0], by_step(o_att_s), xs, by_step(p_sample[l]))
    x1r, resid, tope, topg = _mixer(prompt, sample, (ret_gn_w[l], ret_gn_b[l], w_out[l], ln1_g[l], ln1_b[l],
                                                     w_router[l], b_router[l], w_pl_gate[l], b_pl_gate[l],
                                                     w_pl_proj[l]))

    t_all = tp + ts
    idx, block_e, n_used = _routing_plan(tope[:TOP_K].T, t_all)
    w1g, w1l = _deinterleave_w1(w_e1[l])
    ys = _experts(x1r, idx, block_e, n_used, w1g, w1l, b_e1[l], w_e2[l], b_e2[l])
    y_p, y_s = _combine(ys, topg.T, resid, ln2_g[l], ln2_b[l], tp)
    return (y_p.reshape(B, S, D_MODEL), y_s.reshape(T, DB, D_MODEL).transpose(1, 0, 2),
            new_k_p, new_v_p, new_st_p, new_k_s, new_v_s, new_st_s)
```

```python
import functools

import numpy as np
import jax
import jax.numpy as jnp
from jax import lax
from jax.experimental import pallas as pl
from jax.experimental.pallas import tpu as pltpu

F32 = jnp.float32
BF16 = jnp.bfloat16

D_MODEL = 1024
DEPTH = 1
PAST_LEN = 8192
RET_HEADS = 8
RET_QK_DIM = 32
RET_V_DIM = 64
RET_CHUNK = 128
RET_ROPE_BASE = 10000.0
ATT_HEADS = 8
ATT_HEAD_DIM = 64
ROT_DIMS = ATT_HEAD_DIM // 4
ROPE_THETA = 500000.0
DILATED_PATTERNS = ((128, 1), (512, 4), (2048, 16))
MAX_WINDOW = 2048
RET_QK = RET_HEADS * RET_QK_DIM
RET_V = RET_HEADS * RET_V_DIM
ATT_W = ATT_HEADS * ATT_HEAD_DIM
IN_COLS = RET_QK * 2 + RET_V * 2 + ATT_W * 3
N_EXPERTS = 32
TOP_K = 4
D_EXPERT = D_MODEL
SWIGLU_ALPHA = 1.702
SWIGLU_LIMIT = 7.0
PLE_DIM = 256
LN_EPS = 1e-5
GN_EPS = 1e-6
DEEPNORM_ALPHA = (2 * DEPTH) ** 0.25
NEG_INF = -1e30

LANES = 128
SUBLANES = 8
ROW_TILES = D_MODEL // LANES
VMEM_LIMIT = 56 * 1024 * 1024

TOKEN_TILE = 512
MOE_BLOCK = 256
COMBINE_TILE = 512
ATT_BLOCK = 128
ATT_GROUP = 16


def _dot(a, b):
    return jnp.dot(a, b, preferred_element_type=F32)


def _dot_nt(a, b):
    return lax.dot_general(a, b, (((1,), (1,)), ((), ())), preferred_element_type=F32)


def _dot_tn(a, b):
    return lax.dot_general(a, b, (((0,), (0,)), ((), ())), preferred_element_type=F32)


def _cparams(sem):
    return pltpu.CompilerParams(dimension_semantics=sem, vmem_limit_bytes=VMEM_LIMIT)


def _rotary_tables(pos, inv_freq, head_dim, n_heads):
    half = inv_freq.shape[0]
    ang = pos.astype(F32)[:, None] * inv_freq[None, :]
    cos, sin = jnp.cos(ang), jnp.sin(ang)
    rest = head_dim - 2 * half
    n = pos.shape[0]
    cos_h = jnp.concatenate([cos, cos, jnp.ones((n, rest), F32)], axis=1)
    sin_h = jnp.concatenate([-sin, sin, jnp.zeros((n, rest), F32)], axis=1)
    return jnp.tile(cos_h, (1, n_heads)), jnp.tile(sin_h, (1, n_heads))


def _ret_inv_freq():
    return 1.0 / (RET_ROPE_BASE ** jnp.linspace(0.0, 1.0, RET_QK_DIM // 2, dtype=F32))


def _att_inv_freq():
    return ROPE_THETA ** (-jnp.arange(0, ROT_DIMS, 2, dtype=F32) / ROT_DIMS)


def _ret_log_decay():
    return jnp.log(1.0 - 2.0 ** (-5.0 - jnp.arange(RET_HEADS, dtype=F32)))


def _retention_tables(c_true, c_pad):
    lg = _ret_log_decay()
    idx = jnp.arange(c_pad, dtype=F32)
    live = idx < c_true
    diff = idx[:, None] - idx[None, :]
    decay = jnp.where(diff[None] >= 0, jnp.exp(jnp.maximum(diff, 0.0)[None] * lg[:, None, None]), 0.0)
    decay = jnp.where(live[None, :, None] & live[None, None, :], decay, 0.0)
    cross = jnp.exp((idx + 1.0)[:, None] * lg[None, :])
    cross = jnp.where(live[:, None], cross, 0.0)
    kdec = jnp.exp((c_true - 1.0 - idx)[:, None] * lg[None, :])
    kdec = jnp.where(live[:, None], kdec, 0.0)
    sdec = jnp.exp(c_true * lg)
    hh = RET_HEADS // 2
    decay = decay.reshape(2, hh, c_pad, c_pad)
    cross = jnp.repeat(cross, RET_V_DIM, axis=1).reshape(c_pad, 2, hh * RET_V_DIM).transpose(1, 0, 2)
    kdec = jnp.repeat(kdec, RET_QK_DIM, axis=1).reshape(c_pad, 2, hh * RET_QK_DIM).transpose(1, 0, 2)
    sdec = jnp.broadcast_to(jnp.repeat(sdec, RET_V_DIM)[None, :], (hh * RET_QK_DIM, RET_V)) \
        .reshape(hh * RET_QK_DIM, 2, hh * RET_V_DIM).transpose(1, 0, 2)
    return decay, cross, kdec, sdec


def _rotate(h, cos, sin, half, period):
    outs = []
    for j in range(h.shape[1] // LANES):
        blk = h[:, j * LANES:(j + 1) * LANES]
        lane = lax.broadcasted_iota(jnp.int32, blk.shape, 1)
        first = (lane % period) < half
        partner = jnp.where(first, pltpu.roll(blk, LANES - half, 1), pltpu.roll(blk, half, 1))
        outs.append(blk * cos[:, j * LANES:(j + 1) * LANES] + partner * sin[:, j * LANES:(j + 1) * LANES])
    return jnp.concatenate(outs, axis=1)


PROJ_NAMES = ("q_r", "k_r", "v_r", "g_r", "q_a", "k_a", "v_a")
PROJ_WIDTHS = dict(zip(PROJ_NAMES, (RET_QK, RET_QK, RET_V, RET_V, ATT_W, ATT_W, ATT_W)))


def _inproj_kernel(x_ref, w_ref, cr_ref, sr_ref, ca_ref, sa_ref, *out_refs, direct, transposed):
    x = x_ref[0].astype(BF16)
    cr, sr, ca, sa = cr_ref[...], sr_ref[...], ca_ref[...], sa_ref[...]
    finish = {
        "q_r": lambda h: _rotate(h, cr, sr, RET_QK_DIM // 2, RET_QK_DIM),
        "k_r": lambda h: _rotate(h, cr, sr, RET_QK_DIM // 2, RET_QK_DIM) * (RET_QK_DIM ** -0.5),
        "q_a": lambda h: _rotate(h, ca, sa, ROT_DIMS // 2, ATT_HEAD_DIM) * (ATT_HEAD_DIM ** -0.5),
        "k_a": lambda h: _rotate(h, ca, sa, ROT_DIMS // 2, ATT_HEAD_DIM),
    }
    refs = dict(zip([("d", n) for n in direct] + [("t", n) for n in transposed], out_refs))
    c = 0
    for name in PROJ_NAMES:
        width = PROJ_WIDTHS[name]
        if name in direct or name in transposed:
            h = _dot(x, w_ref[:, c:c + width])
            h = finish.get(name, lambda v: v)(h)
            if name in direct:
                refs[("d", name)][0] = h
            if name in transposed:
                refs[("t", name)][0] = h.T
        c += width


def _in_projection(x, w_in_b, pos, direct, transposed, tm, tile_major):
    B, S, _ = x.shape
    assert S % tm == 0 and (B == 1 or not tile_major)
    cr, sr = _rotary_tables(pos, _ret_inv_freq(), RET_QK_DIM, RET_HEADS)
    ca, sa = _rotary_tables(pos, _att_inv_freq(), ATT_HEAD_DIM, ATT_HEADS)
    tab = lambda w: pl.BlockSpec((tm, w), lambda s, b: (s, 0))
    specs = [pl.BlockSpec((1, tm, PROJ_WIDTHS[n]), lambda s, b: (b, s, 0)) for n in direct]
    shapes = [jax.ShapeDtypeStruct((B, S, PROJ_WIDTHS[n]), F32) for n in direct]
    for n in transposed:
        w = PROJ_WIDTHS[n]
        if tile_major:
            specs.append(pl.BlockSpec((1, w, tm), lambda s, b: (s, 0, 0)))
            shapes.append(jax.ShapeDtypeStruct((S // tm, w, tm), F32))
        else:
            specs.append(pl.BlockSpec((1, w, tm), lambda s, b: (b, 0, s)))
            shapes.append(jax.ShapeDtypeStruct((B, w, S), F32))
    outs = pl.pallas_call(
        functools.partial(_inproj_kernel, direct=direct, transposed=transposed),
        grid=(S // tm, B),
        in_specs=[pl.BlockSpec((1, tm, D_MODEL), lambda s, b: (b, s, 0)),
                  pl.BlockSpec((D_MODEL, IN_COLS), lambda s, b: (0, 0)),
                  tab(RET_QK), tab(RET_QK), tab(ATT_W), tab(ATT_W)],
        out_specs=specs,
        out_shape=shapes,
        compiler_params=_cparams(("arbitrary", "arbitrary")),
        name="in_projection",
    )(x, w_in_b, cr, sr, ca, sa)
    return dict(zip(direct, outs[:len(direct)])), dict(zip(transposed, outs[len(direct):]))


RET_GROUP = 4


def _retention_kernel(q_ref, k_ref, v_ref, s0_ref, dec_ref, cross_ref, kdec_ref, sdec_ref,
                      o_ref, so_ref, upd_ref, st_ref, *, chunk, n_chunks):
    hh = RET_HEADS // 2
    qk_w, v_w = hh * RET_QK_DIM, hh * RET_V_DIM
    lane_q = lax.broadcasted_iota(jnp.int32, (chunk, qk_w), 1) // RET_QK_DIM
    lane_v = lax.broadcasted_iota(jnp.int32, (chunk, v_w), 1) // RET_V_DIM
    blockdiag = (lax.broadcasted_iota(jnp.int32, (qk_w, v_w), 0) // RET_QK_DIM
                 == lax.broadcasted_iota(jnp.int32, (qk_w, v_w), 1) // RET_V_DIM)
    cross, kdec, sdec = cross_ref[0], kdec_ref[0], sdec_ref[0]
    rows = lambda ci: pl.ds(ci * chunk, chunk)

    group = min(RET_GROUP, n_chunks)
    for c0 in range(0, n_chunks, group):
        cs = range(c0, min(c0 + group, n_chunks))
        inner = {}
        for ci in cs:
            q, kb = q_ref[0, rows(ci), :], k_ref[0, rows(ci), :].astype(BF16)
            for h in range(hh):
                qm = jnp.where(lane_q == h, q, 0.0).astype(BF16)
                inner[ci, h] = (_dot_nt(qm, kb) * dec_ref[0, h]).astype(BF16)
        for ci in cs:
            k, vb = k_ref[0, rows(ci), :], v_ref[0, rows(ci), :].astype(BF16)
            o = jnp.zeros((chunk, v_w), F32)
            for h in range(hh):
                o = jnp.where(lane_v == h, _dot(inner[ci, h], vb), o)
            o_ref[0, rows(ci), :] = o
            upd_ref[ci] = jnp.where(blockdiag, _dot_tn((k * kdec).astype(BF16), vb), 0.0)

    state = s0_ref[0, 0]
    for ci in range(n_chunks):
        st_ref[ci] = state.astype(BF16)
        state = state * sdec + upd_ref[ci]
    so_ref[0, 0] = state

    for ci in range(n_chunks):
        o_ref[0, rows(ci), :] += _dot(q_ref[0, rows(ci), :].astype(BF16), st_ref[ci]) * cross


def _retention(q, k, v, state_bd, c_true, c_pad):
    B, S, _ = q.shape
    hh = RET_HEADS // 2
    qk_w, v_w = hh * RET_QK_DIM, hh * RET_V_DIM
    decay, cross, kdec, sdec = _retention_tables(c_true, c_pad)
    kern = functools.partial(_retention_kernel, chunk=c_pad, n_chunks=S // c_pad)
    return pl.pallas_call(
        kern,
        grid=(B, 2),
        in_specs=[pl.BlockSpec((1, S, qk_w), lambda b, g: (b, 0, g)),
                  pl.BlockSpec((1, S, qk_w), lambda b, g: (b, 0, g)),
                  pl.BlockSpec((1, S, v_w), lambda b, g: (b, 0, g)),
                  pl.BlockSpec((1, 1, qk_w, v_w), lambda b, g: (b, g, 0, 0)),
                  pl.BlockSpec((1, hh, c_pad, c_pad), lambda b, g: (g, 0, 0, 0)),
                  pl.BlockSpec((1, c_pad, v_w), lambda b, g: (g, 0, 0)),
                  pl.BlockSpec((1, c_pad, qk_w), lambda b, g: (g, 0, 0)),
                  pl.BlockSpec((1, qk_w, v_w), lambda b, g: (g, 0, 0))],
        out_specs=[pl.BlockSpec((1, S, v_w), lambda b, g: (b, 0, g)),
                   pl.BlockSpec((1, 1, qk_w, v_w), lambda b, g: (b, g, 0, 0))],
        out_shape=[jax.ShapeDtypeStruct((B, S, RET_V), F32),
                   jax.ShapeDtypeStruct((B, 2, qk_w, v_w), F32)],
        scratch_shapes=[pltpu.VMEM((S // c_pad, qk_w, v_w), F32),
                        pltpu.VMEM((S // c_pad, qk_w, v_w), BF16)],
        compiler_params=_cparams(("arbitrary", "arbitrary")),
        name="retention",
    )(q, k, v, state_bd, decay, cross, kdec, sdec)


def _blockdiag_to_state(bd):
    B = bd.shape[0]
    hh = RET_HEADS // 2
    s = bd.reshape(B, 2, hh, RET_QK_DIM, hh, RET_V_DIM)
    s = jnp.stack([s[:, :, h, :, h, :] for h in range(hh)], axis=2)
    return s.reshape(B, RET_HEADS, RET_QK_DIM, RET_V_DIM)


def _retention_sample_kernel(gam_ref, q_ref, k_ref, v_ref, s_ref, o_ref, so_ref, *, steps):
    h = pl.program_id(0)
    gam = [gam_ref[h, p] for p in range(steps + 1)]
    q = [q_ref[t] for t in range(steps)]
    k = [k_ref[t] for t in range(steps)]
    v = [v_ref[t] for t in range(steps)]
    o = [jnp.zeros(v[0].shape, F32) for _ in range(steps)]
    for d in range(RET_QK_DIM):
        s_d = s_ref[0, d]
        new = s_d * gam[steps]
        for t in range(steps):
            o[t] = o[t] + q[t][d:d + 1, :] * s_d
            new = new + (k[t][d:d + 1, :] * gam[steps - 1 - t]) * v[t]
        so_ref[0, d] = new
    for i in range(steps):
        o[i] = o[i] * gam[i + 1]
        for j in range(i + 1):
            qk = jnp.sum(q[i] * k[j], axis=0, keepdims=True) * gam[i - j]
            o[i] = o[i] + qk * v[j]
        o_ref[i] = o[i]


def _retention_sample(q_t, k_t, v_t, state):
    steps, _, nb = q_t.shape
    gam = jnp.exp(_ret_log_decay()[:, None] * jnp.arange(steps + 1, dtype=F32)[None, :])
    qk_spec = pl.BlockSpec((steps, RET_QK_DIM, nb), lambda h: (0, h, 0))
    v_spec = pl.BlockSpec((steps, RET_V_DIM, nb), lambda h: (0, h, 0))
    s_spec = pl.BlockSpec((1, RET_QK_DIM, RET_V_DIM, nb), lambda h: (h, 0, 0, 0))
    return pl.pallas_call(
        functools.partial(_retention_sample_kernel, steps=steps),
        grid=(RET_HEADS,),
        in_specs=[pl.BlockSpec(memory_space=pltpu.SMEM), qk_spec, qk_spec, v_spec, s_spec],
        out_specs=[v_spec, s_spec],
        out_shape=[jax.ShapeDtypeStruct(v_t.shape, F32), jax.ShapeDtypeStruct(state.shape, F32)],
        compiler_params=_cparams(("arbitrary",)),
        name="retention_sample",
    )(gam, q_t, k_t, v_t, state)


def _attn_prompt_kernel(q_ref, k_ref, v_ref, o_ref, o_ref_p, lse_ref, *, seq):
    n = ATT_BLOCK
    head0 = lax.broadcasted_iota(jnp.int32, (n, LANES), 1) < ATT_HEAD_DIM
    head0_kv = lax.broadcasted_iota(jnp.int32, (2 * n, LANES), 1) < ATT_HEAD_DIM
    a = lax.broadcasted_iota(jnp.int32, (n, 2 * n), 0)
    j = lax.broadcasted_iota(jnp.int32, (n, 2 * n), 1)
    bias_pc = jnp.where(((j < n) & (j >= a)) | ((j >= n) & (j - n <= a)), 0.0, NEG_INF)
    bias_c = bias_pc[:, n:]
    bias_first = jnp.where(j < n, NEG_INF, bias_pc)

    def rows(start, d):
        if d > 1:
            return pl.ds(start, n, stride=d)
        return pl.ds(start if isinstance(start, int) else pl.multiple_of(start, n), n)

    def blocks(p_idx, d, specs):
        loaded = []
        for cur, prev, has_prev in specs:
            q = q_ref[0, rows(cur, d), :]
            if has_prev is False:
                kk, vv = k_ref[0, rows(cur, d), :], v_ref[0, rows(cur, d), :]
                bias, hkv = bias_c, head0
            else:
                kk = jnp.concatenate([k_ref[0, rows(prev, d), :], k_ref[0, rows(cur, d), :]], axis=0)
                vv = jnp.concatenate([v_ref[0, rows(prev, d), :], v_ref[0, rows(cur, d), :]], axis=0)
                bias = bias_pc if has_prev is True else jnp.where(has_prev, bias_pc, bias_first)
                hkv = head0_kv
            loaded.append((q, kk.astype(BF16), vv, bias, hkv))
        scores = []
        for q, kb, vv, bias, hkv in loaded:
            for h in range(2):
                mine = head0 if h == 0 else jnp.logical_not(head0)
                scores.append(_dot_nt(jnp.where(mine, q, 0.0).astype(BF16), kb) + bias)
        probs, maxes = [], []
        for s in scores:
            m = jnp.max(s, axis=1, keepdims=True)
            probs.append(jnp.exp(s - m).astype(BF16))
            maxes.append(m)
        pvs = []
        for u, (q, kb, vv, bias, hkv) in enumerate(loaded):
            for h in range(2):
                mine_kv = hkv if h == 0 else jnp.logical_not(hkv)
                pvs.append(_dot(probs[2 * u + h], jnp.where(mine_kv, vv, 1.0).astype(BF16)))
        for u, (cur, prev, has_prev) in enumerate(specs):
            pv0, pv1 = pvs[2 * u], pvs[2 * u + 1]
            acc_u = jnp.where(head0, pv0, pv1)
            l_u = pltpu.roll(jnp.where(head0, pv1, pv0), ATT_HEAD_DIM, 1)
            m_u = jnp.where(head0, maxes[2 * u], maxes[2 * u + 1])
            o_ref_p[p_idx, rows(cur, d), :] = acc_u / l_u
            lse_ref[p_idx, rows(cur, d), :] = m_u + jnp.log(l_u)

    G = ATT_GROUP
    for p_idx, (window, d) in enumerate(DILATED_PATTERNS):
        nb = (seq // d) // n
        assert (d * nb) % G == 0 and (nb % G == 0 or G % nb == 0)

        def group(g, carry, p_idx=p_idx, d=d, nb=nb):
            specs = []
            for jj in range(G):
                if nb > G:
                    first = g * G
                    r = 0 if d == 1 else first // nb
                    c = first % nb + jj
                    specs.append((c * (n * d) + r, jnp.maximum(c - 1, 0) * (n * d) + r,
                                  True if jj > 0 else c > 0))
                else:
                    r = g * (G // nb) + jj // nb
                    c = jj % nb
                    specs.append((c * (n * d) + r, (c - 1) * (n * d) + r, c > 0))
            blocks(p_idx, d, specs)
            return carry

        lax.fori_loop(0, d * nb // G, group, 0)

    def merge(c, carry):
        rs = pl.ds(pl.multiple_of(c * n, n), n)
        lse = [lse_ref[p, rs, :] for p in range(len(DILATED_PATTERNS))]
        top = functools.reduce(jnp.maximum, lse)
        w = [jnp.exp(x - top) for x in lse]
        num = functools.reduce(lambda x, y: x + y, [w[p] * o_ref_p[p, rs, :] for p in range(len(w))])
        o_ref[0, rs, :] = num / functools.reduce(lambda x, y: x + y, w)
        return carry

    lax.fori_loop(0, seq // n, merge, 0)


def _attention_prompt(q, k, v):
    B, S, _ = q.shape
    spec = pl.BlockSpec((1, S, LANES), lambda b, g: (b, 0, g))
    return pl.pallas_call(
        functools.partial(_attn_prompt_kernel, seq=S),
        grid=(B, ATT_W // LANES),
        in_specs=[spec, spec, spec],
        out_specs=spec,
        out_shape=jax.ShapeDtypeStruct((B, S, ATT_W), F32),
        scratch_shapes=[pltpu.VMEM((len(DILATED_PATTERNS), S, LANES), F32)] * 2,
        compiler_params=_cparams(("arbitrary", "arbitrary")),
        name="attention_prompt",
    )(q, k, v)


def _attn_sample_kernel(q_ref, kn_ref, vn_ref, kc_ref, vc_ref, cc_ref, cn_ref, o_ref, *, steps):
    q = q_ref[0].astype(BF16)
    width = q.shape[1]
    kc = kc_ref[0].reshape(width, -1).astype(BF16)
    vc = vc_ref[0].reshape(width, -1).astype(BF16)
    cnt_c, cnt_n = cc_ref[...], cn_ref[...]
    s_c = jnp.where(cnt_c > 0.0, _dot(q, kc), NEG_INF)
    s_n = jnp.where(cnt_n > 0.0, _dot_nt(q, kn_ref[0].astype(BF16)), NEG_INF)
    m = jnp.maximum(s_c.max(axis=1, keepdims=True), s_n.max(axis=1, keepdims=True))
    p_c = cnt_c * jnp.exp(s_c - m)
    p_n = cnt_n * jnp.exp(s_n - m)
    den = p_c.sum(axis=1, keepdims=True) + p_n.sum(axis=1, keepdims=True)
    num = _dot_nt(p_c.astype(BF16), vc) + _dot(p_n.astype(BF16), vn_ref[0].astype(BF16))
    rows = lax.broadcasted_iota(jnp.int32, num.shape, 0)
    lanes = lax.broadcasted_iota(jnp.int32, num.shape, 1)
    out = jnp.where(rows % ATT_HEADS == lanes // ATT_HEAD_DIM, num / den, 0.0)
    for t in range(steps):
        o_ref[0, t:t + 1, :] = jnp.sum(out[t * ATT_HEADS:(t + 1) * ATT_HEADS], axis=0, keepdims=True)


def _sample_key_counts(dec_seq, buf, new_pad):
    t_q = np.repeat(np.arange(dec_seq), ATT_HEADS)[:, None]
    back = buf + t_q - np.arange(buf)[None, :]
    cnt_c = np.zeros(back.shape, np.float32)
    for w, d in DILATED_PATTERNS:
        cnt_c += ((back % d == 0) & (back >= d) & (back <= w)).astype(np.float32)
    back = t_q - np.arange(new_pad)[None, :]
    cnt_n = np.zeros(back.shape, np.float32)
    for w, d in DILATED_PATTERNS:
        cnt_n += ((back >= 0) & (back % d == 0) & (back <= w)).astype(np.float32)
    cnt_n *= (np.arange(new_pad) < dec_seq)[None, :]
    return jnp.asarray(cnt_c), jnp.asarray(cnt_n)


def _attention_sample(q, k_new, v_new, cache_k_t, cache_v_t):
    B, T, width = q.shape
    _, H, D, buf = cache_k_t.shape
    new_pad = SUBLANES
    cnt_c, cnt_n = _sample_key_counts(T, buf, new_pad)
    q_rows = (q.reshape(B, T, 1, H, D) * jnp.eye(H, dtype=F32)[None, None, :, :, None]).reshape(B, T * H, width)
    pad = lambda x: jnp.pad(x, ((0, 0), (0, new_pad - T), (0, 0)))
    new_spec = pl.BlockSpec((1, new_pad, width), lambda b: (b, 0, 0))
    cache_spec = pl.BlockSpec((1, H, D, buf), lambda b: (b, 0, 0, 0))
    const = lambda x: pl.BlockSpec(x.shape, lambda b: (0, 0))
    return pl.pallas_call(
        functools.partial(_attn_sample_kernel, steps=T),
        grid=(B,),
        in_specs=[pl.BlockSpec((1, T * H, width), lambda b: (b, 0, 0)), new_spec, new_spec,
                  cache_spec, cache_spec, const(cnt_c), const(cnt_n)],
        out_specs=pl.BlockSpec((1, T, width), lambda b: (b, 0, 0)),
        out_shape=jax.ShapeDtypeStruct((B, T, width), F32),
        compiler_params=_cparams(("arbitrary",)),
        name="attention_sample",
    )(q_rows, pad(k_new), pad(v_new), cache_k_t, cache_v_t, cnt_c, cnt_n)


def _layer_norm(y, g, b):
    mu = jnp.mean(y, axis=-1, keepdims=True)
    d = y - mu
    var = jnp.mean(d * d, axis=-1, keepdims=True)
    return d * lax.rsqrt(var + LN_EPS) * g + b


def _mixer_kernel(oret_p, g_p, oatt_p, x_p, pe_p, oret_s, g_s, oatt_s, x_s, pe_s,
                  avg_ref, gnw_ref, gnb_ref, wor_ref, woa_ref, ln1g_ref, ln1b_ref,
                  wg_ref, bg_ref, wp_ref, wr_ref, br_ref,
                  x1r_ref, resid_ref, tope_ref, topg_ref, *, n_prompt_tiles):
    def tile(oret_ref, g_ref, oatt_ref, x_ref, pe_ref):
        o_ret, g, o_att, x, pe = oret_ref[...], g_ref[...], oatt_ref[...], x_ref[...], pe_ref[...]
        avg = avg_ref[...]

        def head_mean(z):
            hi = z.astype(BF16)
            lo = (z - hi.astype(F32)).astype(BF16)
            return _dot(hi, avg) + _dot(lo, avg)

        d = o_ret - head_mean(o_ret)
        o_n = d * lax.rsqrt(head_mean(d * d) + GN_EPS) * gnw_ref[...] + gnb_ref[...]
        ret = g * jax.nn.sigmoid(g) * o_n
        mix = _dot(ret.astype(BF16), wor_ref[...]) + _dot(o_att.astype(BF16), woa_ref[...])
        x1 = _layer_norm(DEEPNORM_ALPHA * x + mix, ln1g_ref[...], ln1b_ref[...])
        x1b = x1.astype(BF16)
        ple = jax.nn.sigmoid(_dot(x1b, wg_ref[...]) + bg_ref[...]) * _dot(pe.astype(BF16), wp_ref[...])
        resid_ref[...] = DEEPNORM_ALPHA * x1 + ple
        tm = x1.shape[0]
        for s in range(ROW_TILES):
            x1r_ref[pl.ds(s, tm, stride=ROW_TILES), :] = x1[:, s * LANES:(s + 1) * LANES]

        logits = lax.dot_general(wr_ref[...], x1, (((1,), (1,)), ((), ())), preferred_element_type=F32,
                                 precision=lax.Precision.HIGHEST) + br_ref[...]
        row = lax.broadcasted_iota(jnp.int32, logits.shape, 0).astype(F32)
        work = logits
        vals, idxs = [], []
        for _ in range(TOP_K):
            m = jnp.max(work, axis=0, keepdims=True)
            idx = jnp.min(jnp.where(work == m, row, float(N_EXPERTS)), axis=0, keepdims=True)
            vals.append(m)
            idxs.append(idx)
            work = jnp.where(row == idx, -jnp.inf, work)
        exps = [jnp.exp(v - vals[0]) for v in vals]
        den = exps[0]
        for e in exps[1:]:
            den = den + e
        fill = [jnp.zeros_like(den)] * (SUBLANES - TOP_K)
        tope_ref[...] = jnp.concatenate(idxs + fill, axis=0).astype(jnp.int32)
        topg_ref[...] = jnp.concatenate([e / den for e in exps] + fill, axis=0)

    @pl.when(pl.program_id(0) < n_prompt_tiles)
    def _():
        tile(oret_p, g_p, oatt_p, x_p, pe_p)

    @pl.when(pl.program_id(0) >= n_prompt_tiles)
    def _():
        tile(oret_s, g_s, oatt_s, x_s, pe_s)


def _mixer(prompt, sample, weights):
    (gn_w, gn_b, w_out, ln1_g, ln1_b, w_router, b_router, w_pl_gate, b_pl_gate, w_pl_proj) = weights
    tp, ts = prompt[3].shape[0], sample[3].shape[0]
    tm = TOKEN_TILE
    assert tp % tm == 0 and ts % tm == 0
    npt, nst = tp // tm, ts // tm
    t_all = tp + ts
    avg = jnp.asarray(np.kron(np.eye(RET_HEADS), np.full((RET_V_DIM, RET_V_DIM), 1.0 / RET_V_DIM)), BF16)
    w_out_b = w_out.astype(BF16)
    row = lambda v: v.reshape(1, -1)
    consts = [avg, row(gn_w), row(gn_b), w_out_b[:RET_V], w_out_b[RET_V:], row(ln1_g), row(ln1_b),
              w_pl_gate.astype(BF16), row(b_pl_gate), w_pl_proj.astype(BF16), w_router.T, b_router.reshape(-1, 1)]
    p_spec = lambda a: pl.BlockSpec((tm, a.shape[1]), lambda i: (jnp.minimum(i, npt - 1), 0))
    s_spec = lambda a: pl.BlockSpec((tm, a.shape[1]), lambda i: (jnp.maximum(i - npt, 0), 0))
    c_spec = lambda a: pl.BlockSpec(a.shape, lambda i: (0, 0))
    return pl.pallas_call(
        functools.partial(_mixer_kernel, n_prompt_tiles=npt),
        grid=(npt + nst,),
        in_specs=[p_spec(a) for a in prompt] + [s_spec(a) for a in sample] + [c_spec(a) for a in consts],
        out_specs=[pl.BlockSpec((tm * ROW_TILES, LANES), lambda i: (i, 0)),
                   pl.BlockSpec((tm, D_MODEL), lambda i: (i, 0)),
                   pl.BlockSpec((SUBLANES, tm), lambda i: (0, i)),
                   pl.BlockSpec((SUBLANES, tm), lambda i: (0, i))],
        out_shape=[jax.ShapeDtypeStruct((t_all * ROW_TILES, LANES), F32),
                   jax.ShapeDtypeStruct((t_all, D_MODEL), F32),
                   jax.ShapeDtypeStruct((SUBLANES, t_all), jnp.int32),
                   jax.ShapeDtypeStruct((SUBLANES, t_all), F32)],
        compiler_params=_cparams(("arbitrary",)),
        name="mixer_out",
    )(*prompt, *sample, *consts)


def _deinterleave_kernel(w_ref, g_ref, l_ref, t_ref):
    half = t_ref.shape[1] // 2
    for j in range(t_ref.shape[0]):
        rows = slice(j * LANES, (j + 1) * LANES)
        t_ref[j] = w_ref[0, rows, :].T
        g_ref[0, rows, :] = t_ref[j, pl.ds(0, half, stride=2), :].T.astype(BF16)
        l_ref[0, rows, :] = t_ref[j, pl.ds(1, half, stride=2), :].T.astype(BF16)


DEINTERLEAVE_ROWS = 512


def _deinterleave_w1(w_e1):
    e, d, f2 = w_e1.shape
    rows = DEINTERLEAVE_ROWS
    spec = pl.BlockSpec((1, rows, f2 // 2), lambda i, c: (i, c, 0))
    return pl.pallas_call(
        _deinterleave_kernel,
        grid=(e, d // rows),
        in_specs=[pl.BlockSpec((1, rows, f2), lambda i, c: (i, c, 0))],
        out_specs=[spec, spec],
        out_shape=[jax.ShapeDtypeStruct((e, d, f2 // 2), BF16)] * 2,
        scratch_shapes=[pltpu.VMEM((rows // LANES, f2, LANES), F32)],
        compiler_params=_cparams(("arbitrary", "arbitrary")),
        name="deinterleave_w1",
    )(w_e1)


def _routing_plan(top_e, n_tokens):
    a = n_tokens * TOP_K
    mb, tc = MOE_BLOCK, COMBINE_TILE
    flat_e = top_e.reshape(-1)
    bits = max(a - 1, 1).bit_length()
    assert N_EXPERTS << bits < 2 ** 31
    keys = jnp.sort((flat_e << bits) | jnp.arange(a, dtype=jnp.int32))
    order = keys & ((1 << bits) - 1)
    experts = jnp.arange(N_EXPERTS, dtype=jnp.int32)
    counts = jnp.sum((flat_e[:, None] == experts[None, :]).astype(jnp.int32), axis=0)
    padded = (counts + mb - 1) // mb * mb
    start = jnp.cumsum(counts) - counts
    pend = jnp.cumsum(padded)
    pstart = pend - padded
    n_blocks = -(-(a + N_EXPERTS * (mb - 1)) // mb)
    block_first = jnp.arange(n_blocks, dtype=jnp.int32) * mb
    block_e = jnp.minimum(jnp.sum((pend[None, :] <= block_first[:, None]).astype(jnp.int32), axis=1),
                          N_EXPERTS - 1).astype(jnp.int32)
    n_used = (pend[-1] // mb).astype(jnp.int32).reshape(1)
    first_b, count_b, start_b = pstart[block_e][:, None], counts[block_e][:, None], start[block_e][:, None]
    off = jnp.arange(n_blocks * mb, dtype=jnp.int32).reshape(n_blocks, mb) - first_b
    valid = off < count_b
    asg = order[jnp.clip(start_b + off, 0, a - 1)]
    t, k = asg // TOP_K, asg % TOP_K
    row_real = ((t // tc) * TOP_K + k) * tc + t % tc
    row_pad = a + (first_b - start_b) + (off - count_b)
    tok = jnp.where(valid, t, 0)
    row = jnp.where(valid, row_real, row_pad)
    idx = jnp.stack([tok, row], axis=1)
    extra = jnp.stack([jnp.zeros((mb,), jnp.int32), n_blocks * mb + jnp.arange(mb, dtype=jnp.int32)])[None]
    return jnp.concatenate([idx, extra], axis=0), block_e, n_used


IDX_RING = 4


def _row_tile(r):
    start = r * ROW_TILES
    return pl.ds(start if isinstance(r, int) else pl.multiple_of(start, ROW_TILES), ROW_TILES)


def _expert_kernel(be_ref, nused_ref, idx_hbm, x_hbm, w1g_ref, w1l_ref, b1g_ref, b1l_ref, w2_ref, b2_ref,
                   y_hbm, xbuf, ybuf, idx_smem, gsem, ssem, isem, *, n_blocks):
    i = pl.program_id(0)
    n_used = nused_ref[0]
    mb = MOE_BLOCK
    rows = mb * ROW_TILES
    cur, other = i % 2, (i + 1) % 2

    def ring(block):
        return (block + IDX_RING) % IDX_RING

    def idx_copy(block):
        src = jnp.where(block < 0, n_blocks, jnp.minimum(block, n_blocks))
        return pltpu.make_async_copy(idx_hbm.at[src], idx_smem.at[ring(block)], isem.at[ring(block)])

    def gather_copy(r, tok, s):
        return pltpu.make_async_copy(x_hbm.at[_row_tile(tok), :], xbuf.at[s, _row_tile(r), :], gsem.at[s])

    def scatter_copy(r, row, s):
        return pltpu.make_async_copy(ybuf.at[s, _row_tile(r), :], y_hbm.at[_row_tile(row), :], ssem.at[s])

    def gathered(s):
        return pltpu.make_async_copy(x_hbm.at[pl.ds(0, rows), :], xbuf.at[s], gsem.at[s])

    def scattered(s):
        return pltpu.make_async_copy(ybuf.at[s], y_hbm.at[pl.ds(0, rows), :], ssem.at[s])

    def issue_rolled(copy, block, column, s):
        def body(r, carry):
            copy(r, idx_smem[ring(block), column, r], s).start()
            return carry
        lax.fori_loop(0, mb, body, 0)

    @pl.when(i == 0)
    def _():
        idx_copy(i).start()
        idx_copy(i - 1).start()
        idx_copy(i + 1).start()
        idx_copy(i).wait()
        idx_copy(i - 1).wait()
        issue_rolled(gather_copy, i, 0, cur)
        ybuf[other] = jnp.zeros((rows, LANES), F32)

    @pl.when(i < n_used)
    def _():
        idx_copy(i + 1).wait()
        idx_copy(i + 2).start()
        gathered(cur).wait()

        @pl.when(i >= 1)
        def _():
            scattered(cur).wait()

        for r in range(mb):
            gather_copy(r, idx_smem[ring(i + 1), 0, r], other).start(priority=1)
        for r in range(mb):
            scatter_copy(r, idx_smem[ring(i - 1), 1, r], other).start(priority=0)
        x = jnp.concatenate([xbuf[cur, pl.ds(s, mb, stride=ROW_TILES), :] for s in range(ROW_TILES)],
                            axis=1).astype(BF16)
        glu = jnp.minimum(_dot(x, w1g_ref[0]) + b1g_ref[0], SWIGLU_LIMIT)
        lin = jnp.clip(_dot(x, w1l_ref[0]) + b1l_ref[0], -SWIGLU_LIMIT, SWIGLU_LIMIT)
        act = glu * jax.nn.sigmoid(SWIGLU_ALPHA * glu) * (lin + 1.0)
        y = _dot(act.astype(BF16), w2_ref[0]) + b2_ref[0]
        for s in range(ROW_TILES):
            ybuf[cur, pl.ds(s, mb, stride=ROW_TILES), :] = y[:, s * LANES:(s + 1) * LANES]

    @pl.when(i == n_used - 1)
    def _():
        issue_rolled(scatter_copy, i, 1, cur)
        scattered(other).wait()
        scattered(cur).wait()
        gathered(other).wait()
        idx_copy(i + 2).wait()

    @pl.when(i >= n_used)
    def _():
        ybuf[cur] = jnp.zeros((rows, LANES), F32)
        fill = pltpu.make_async_copy(ybuf.at[cur], y_hbm.at[pl.ds(pl.multiple_of(i * rows, rows), rows), :],
                                     ssem.at[cur])
        fill.start()
        fill.wait()


def _experts(x1r, idx, block_e, n_used, w1g, w1l, b_e1, w_e2, b_e2):
    n_blocks = idx.shape[0] - 1
    mb = MOE_BLOCK
    rows = mb * ROW_TILES
    b1g, b1l = b_e1[:, None, 0::2], b_e1[:, None, 1::2]
    w2 = w_e2.astype(BF16)
    b2 = b_e2[:, None, :]
    wspec = pl.BlockSpec((1, D_MODEL, D_EXPERT), lambda i, be, nu: (be[i], 0, 0))
    bspec = pl.BlockSpec((1, 1, D_EXPERT), lambda i, be, nu: (be[i], 0, 0))
    return pl.pallas_call(
        functools.partial(_expert_kernel, n_blocks=n_blocks),
        grid_spec=pltpu.PrefetchScalarGridSpec(
            num_scalar_prefetch=2,
            grid=(n_blocks,),
            in_specs=[pl.BlockSpec(memory_space=pl.ANY),
                      pl.BlockSpec(memory_space=pl.ANY),
                      wspec, wspec, bspec, bspec,
                      pl.BlockSpec((1, D_EXPERT, D_MODEL), lambda i, be, nu: (be[i], 0, 0)),
                      pl.BlockSpec((1, 1, D_MODEL), lambda i, be, nu: (be[i], 0, 0))],
            out_specs=pl.BlockSpec(memory_space=pl.ANY),
            scratch_shapes=[pltpu.VMEM((2, rows, LANES), F32),
                            pltpu.VMEM((2, rows, LANES), F32),
                            pltpu.SMEM((IDX_RING, 2, mb), jnp.int32),
                            pltpu.SemaphoreType.DMA((2,)),
                            pltpu.SemaphoreType.DMA((2,)),
                            pltpu.SemaphoreType.DMA((IDX_RING,))]),
        out_shape=jax.ShapeDtypeStruct(((n_blocks + 1) * rows, LANES), F32),
        compiler_params=_cparams(("arbitrary",)),
        name="experts",
    )(block_e, n_used, idx, x1r, w1g, w1l, b1g, b1l, w2, b2)


def _combine_kernel(y_ref, gate_ref, resid_ref, g_ref, b_ref, op_ref, os_ref, *, n_prompt_tiles):
    tm = COMBINE_TILE
    z = resid_ref[...]
    gate = gate_ref[...]
    for kk in range(TOP_K):
        y_k = jnp.concatenate(
            [y_ref[pl.ds(kk * tm * ROW_TILES + s, tm, stride=ROW_TILES), :] for s in range(ROW_TILES)], axis=1)
        z = z + y_k * gate[:, kk:kk + 1]
    out = _layer_norm(z, g_ref[...], b_ref[...])

    @pl.when(pl.program_id(0) < n_prompt_tiles)
    def _():
        op_ref[...] = out

    @pl.when(pl.program_id(0) >= n_prompt_tiles)
    def _():
        os_ref[...] = out


def _combine(ys, top_g, resid, ln2_g, ln2_b, n_prompt):
    t_all = resid.shape[0]
    tm = COMBINE_TILE
    assert n_prompt % tm == 0 and (t_all - n_prompt) % tm == 0
    npt, n_tiles = n_prompt // tm, t_all // tm
    return pl.pallas_call(
        functools.partial(_combine_kernel, n_prompt_tiles=npt),
        grid=(n_tiles,),
        in_specs=[pl.BlockSpec((tm * TOP_K * ROW_TILES, LANES), lambda i: (i, 0)),
                  pl.BlockSpec((tm, top_g.shape[1]), lambda i: (i, 0)),
                  pl.BlockSpec((tm, D_MODEL), lambda i: (i, 0)),
                  pl.BlockSpec((1, D_MODEL), lambda i: (0, 0)),
                  pl.BlockSpec((1, D_MODEL), lambda i: (0, 0))],
        out_specs=[pl.BlockSpec((tm, D_MODEL), lambda i: (jnp.minimum(i, npt - 1), 0)),
                   pl.BlockSpec((tm, D_MODEL), lambda i: (jnp.maximum(i - npt, 0), 0))],
        out_shape=[jax.ShapeDtypeStruct((n_prompt, D_MODEL), F32),
                   jax.ShapeDtypeStruct((t_all - n_prompt, D_MODEL), F32)],
        compiler_params=_cparams(("arbitrary",)),
        name="combine_ln2",
    )(ys, top_g, resid, ln2_g.reshape(1, -1), ln2_b.reshape(1, -1))


def kernel(x_prompt, x_sample, cache_att_k, cache_att_v, state_ret, p_prompt, p_sample, w_in, ret_gn_w, ret_gn_b, w_out, ln1_g, ln1_b, w_router, b_router, w_e1, b_e1, w_e2, b_e2, w_pl_gate, b_pl_gate, w_pl_proj, ln2_g, ln2_b):
    assert w_in.shape[0] == DEPTH == 1
    B, S, _ = x_prompt.shape
    DB, T, _ = x_sample.shape
    H, D = ATT_HEADS, ATT_HEAD_DIM
    l = 0
    w_in_b = w_in[l].astype(BF16)

    pos_p = jnp.arange(S, dtype=F32)
    pd, pt = _in_projection(x_prompt, w_in_b, pos_p, PROJ_NAMES, ("k_a", "v_a"), TOKEN_TILE, False)
    zero_state = jnp.zeros((B, 2, RET_QK // 2, RET_V // 2), F32)
    o_ret_p, st_p = _retention(pd["q_r"], pd["k_r"], pd["v_r"], zero_state, RET_CHUNK, RET_CHUNK)
    o_att_p = _attention_prompt(pd["q_a"], pd["k_a"], pd["v_a"])
    keep = min(MAX_WINDOW, S)
    as_cache = lambda a: a.reshape(B, H, D, S).transpose(0, 3, 1, 2)[None, :, S - keep:]
    new_k_p, new_v_p = as_cache(pt["k_a"]), as_cache(pt["v_a"])
    new_st_p = _blockdiag_to_state(st_p)[None]

    ts = DB * T
    by_step = lambda a: a.transpose(1, 0, 2).reshape(ts, a.shape[-1])
    xs = by_step(x_sample)
    pos_s = jnp.repeat(PAST_LEN + jnp.arange(T, dtype=F32), DB)
    sd, st = _in_projection(xs[None], w_in_b, pos_s, ("g_r", "q_a", "k_a", "v_a"),
                            ("q_r", "k_r", "v_r", "k_a", "v_a"), DB, True)
    state_t = jnp.transpose(state_ret[l], (1, 2, 3, 0))
    o_ret_t, state_new = _retention_sample(st["q_r"], st["k_r"], st["v_r"], state_t)
    o_ret_s = o_ret_t.transpose(0, 2, 1).reshape(ts, RET_V)
    by_batch = lambda a: a[0].reshape(T, DB, -1).transpose(1, 0, 2)
    cache_t = lambda c: jnp.transpose(c[l], (0, 2, 3, 1))
    o_att_s = _attention_sample(by_batch(sd["q_a"]), by_batch(sd["k_a"]), by_batch(sd["v_a"]),
                                cache_t(cache_att_k), cache_t(cache_att_v))
    as_new = lambda a: a.reshape(T, H, D, DB).transpose(3, 0, 1, 2)[None]
    new_k_s, new_v_s = as_new(st["k_a"]), as_new(st["v_a"])
    new_st_s = jnp.transpose(state_new, (3, 0, 1, 2)).astype(state_ret.dtype)[None]

    tp = B * S
    flat2 = lambda a: a.reshape(-1, a.shape[-1])
    prompt = (flat2(o_ret_p), flat2(pd["g_r"]), flat2(o_att_p), flat2(x_prompt), flat2(p_prompt[l]))
    sample = (o_ret_s, sd["g_r"][0], by_step(o_att_s), xs, by_step(p_sample[l]))
    x1r, resid, tope, topg = _mixer(prompt, sample, (ret_gn_w[l], ret_gn_b[l], w_out[l], ln1_g[l], ln1_b[l],
                                                     w_router[l], b_router[l], w_pl_gate[l], b_pl_gate[l],
                                                     w_pl_proj[l]))

    t_all = tp + ts
    idx, block_e, n_used = _routing_plan(tope[:TOP_K].T, t_all)
    w1g, w1l = _deinterleave_w1(w_e1[l])
    ys = _experts(x1r, idx, block_e, n_used, w1g, w1l, b_e1[l], w_e2[l], b_e2[l])
    y_p, y_s = _combine(ys, topg.T, resid, ln2_g[l], ln2_b[l], tp)
    return (y_p.reshape(B, S, D_MODEL), y_s.reshape(T, DB, D_MODEL).transpose(1, 0, 2),
            new_k_p, new_v_p, new_st_p, new_k_s, new_v_s, new_st_s)
```
